```python
import jax, jax.numpy as jnp
from jax import lax
import numpy as np

D_MODEL = 1024
BATCH = 8
SEQ = 2048
DEPTH = 2

GRID_W = 64
CTX_LEN = 256
HEAD_DIM = 64
ATTN_HEADS = 8
ATTN_WIDTH = ATTN_HEADS * HEAD_DIM
LRU_WIDTH = D_MODEL // 2
LRU_BLOCKS = 8
LRU_BLOCK = LRU_WIDTH // LRU_BLOCKS
LRU_C = 8.0
CONV_W = 4
NA_KH = 8
NA_KW = 16
MIX_WIDTH = ATTN_WIDTH + LRU_WIDTH
IN_COLS = 3 * ATTN_WIDTH + 2 * LRU_WIDTH
SPLITS = (ATTN_WIDTH, 2 * ATTN_WIDTH, 3 * ATTN_WIDTH, 3 * ATTN_WIDTH + LRU_WIDTH)
D_FF = 3584
N_EXPERTS = 8
TOP_K = 2
N_DENSE = (DEPTH + 1) // 2
N_MOE = DEPTH // 2
EPS = 1e-6
NEG_INF = -1e30

kernel_name = 'hybrid_natten_rglru_moe_dit'


def rms_norm(x, g):
    xf = x.astype(jnp.float32)
    y = xf * lax.rsqrt(jnp.mean(xf * xf, axis=-1, keepdims=True) + EPS)
    return (y * g.astype(jnp.float32)).astype(x.dtype)


def modulate(h, shift, scale):
    return h * (1.0 + scale) + shift


def to_heads(t):
    b, s, _ = t.shape
    return t.reshape(b, s, ATTN_HEADS, HEAD_DIM).transpose(0, 2, 1, 3)


def from_heads(t):
    b, h, s, d = t.shape
    return t.transpose(0, 2, 1, 3).reshape(b, s, h * d)


def neighborhood_attention(q, k, v, k_ctx, v_ctx, rpb):
    b, h, s, dh = q.shape
    rows = s // GRID_W
    kh = min(NA_KH, rows)
    kw = NA_KW
    r = jnp.arange(rows)
    col = jnp.arange(GRID_W)
    row_start = jnp.clip(r - kh // 2, 0, rows - kh)
    row_idx = row_start[:, None] + jnp.arange(kh)[None, :]
    col_start = jnp.clip(col - kw // 2, 0, GRID_W - kw)
    col_ok = (col[None, :] >= col_start[:, None]) & (col[None, :] < col_start[:, None] + kw)
    scale = dh ** -0.5
    qg = q.reshape(b, h, rows, GRID_W, dh)
    kg = k.reshape(b, h, rows, GRID_W, dh)[:, :, row_idx]
    vg = v.reshape(b, h, rows, GRID_W, dh)[:, :, row_idx]
    s_win = jnp.einsum('bhrqd,bhrjkd->bhrqjk', qg, kg).astype(jnp.float32) * scale
    dr = row_idx - r[:, None] + (NA_KH - 1)
    dc = jnp.clip(col[None, :] - col[:, None], -(NA_KW - 1), NA_KW - 1) + (NA_KW - 1)
    bias = rpb.astype(jnp.float32)[:, dr[:, None, :, None], dc[None, :, None, :]]
    s_win = jnp.where(col_ok[:, None, :], s_win + bias, NEG_INF)
    s_ctx = jnp.einsum('bhrqd,bhld->bhrql', qg, k_ctx).astype(jnp.float32) * scale
    n_win = kh * GRID_W
    scores = jnp.concatenate([s_win.reshape(b, h, rows, GRID_W, n_win), s_ctx], axis=-1)
    p = jax.nn.softmax(scores, axis=-1).astype(v.dtype)
    p_win = p[..., :n_win].reshape(b, h, rows, GRID_W, kh, GRID_W)
    p_ctx = p[..., n_win:]
    out = jnp.einsum('bhrqjk,bhrjkd->bhrqd', p_win, vg) + jnp.einsum('bhrql,bhld->bhrqd', p_ctx, v_ctx)
    return out.reshape(b, h, s, dh)


def context_attention(q, k, v):
    s = jnp.einsum('bhqd,bhkd->bhqk', q, k).astype(jnp.float32) * (q.shape[-1] ** -0.5)
    p = jax.nn.softmax(s, axis=-1).astype(v.dtype)
    return jnp.einsum('bhqk,bhkd->bhqd', p, v)


def centred_depthwise_conv(x, w, bias):
    t = x.shape[1]
    lo = (CONV_W - 1) // 2
    xp = jnp.pad(x, ((0, 0), (lo, CONV_W - 1 - lo), (0, 0)))
    y = bias + xp[:, 0:t] * w[0]
    for j in range(1, CONV_W):
        y = y + xp[:, j:j + t] * w[j]
    return y


def rglru_coeffs(u, wa, ba, wx, bx, lam):
    b, t, _ = u.shape
    ub = u.reshape(b, t, LRU_BLOCKS, LRU_BLOCK)
    gate_r = jax.nn.sigmoid((jnp.einsum('btni,nij->btnj', ub, wa).reshape(b, t, LRU_WIDTH) + ba).astype(jnp.float32))
    gate_i = jax.nn.sigmoid((jnp.einsum('btni,nij->btnj', ub, wx).reshape(b, t, LRU_WIDTH) + bx).astype(jnp.float32))
    log_a = -LRU_C * gate_r * jax.nn.softplus(-lam.astype(jnp.float32))
    a = jnp.exp(log_a)
    mult = jnp.sqrt(-jnp.expm1(2.0 * log_a))
    return a, mult * gate_i * u.astype(jnp.float32)


def _affine_combine(left, right):
    a_l, b_l = left
    a_r, b_r = right
    return a_l * a_r, a_r * b_l + b_r


def linear_scan(a, b, h0, reverse):
    if h0 is not None:
        edge = -1 if reverse else 0
        b = b.at[:, edge].add(a[:, edge] * h0)
    _, h = lax.associative_scan(_affine_combine, (a, b), reverse=reverse, axis=1)
    return h


def mixing(h_lat, h_ctx, w_in, w_out, rpb, conv_w, conv_b, wa, ba, wx, bx, lam, with_ctx_out):
    q, k, v, xr, gr = jnp.split(h_lat @ w_in, SPLITS, axis=-1)
    qc, kc, vc, xrc, grc = jnp.split(h_ctx @ w_in, SPLITS, axis=-1)
    kc_h, vc_h = to_heads(kc), to_heads(vc)
    o_attn = from_heads(neighborhood_attention(to_heads(q), to_heads(k), to_heads(v), kc_h, vc_h, rpb))
    u_lat = centred_depthwise_conv(xr, conv_w, conv_b)
    u_ctx = centred_depthwise_conv(xrc, conv_w, conv_b)
    y_lat = None
    y_ctx = None
    for d, rev in enumerate((False, True)):
        a_c, b_c = rglru_coeffs(u_ctx, wa[d], ba[d], wx[d], bx[d], lam[d])
        h_c = linear_scan(a_c, b_c, None, rev)
        h_end = h_c[:, 0] if rev else h_c[:, -1]
        a_l, b_l = rglru_coeffs(u_lat, wa[d], ba[d], wx[d], bx[d], lam[d])
        h_l = linear_scan(a_l, b_l, h_end, rev)
        y_lat = h_l if y_lat is None else y_lat + h_l
        if with_ctx_out:
            y_ctx = h_c if y_ctx is None else y_ctx + h_c
    o_lru = jax.nn.gelu(gr) * y_lat.astype(gr.dtype)
    out_lat = jnp.concatenate([o_attn, o_lru], axis=-1) @ w_out
    if not with_ctx_out:
        return out_lat, None
    o_attn_c = from_heads(context_attention(to_heads(qc), kc_h, vc_h))
    o_lru_c = jax.nn.gelu(grc) * y_ctx.astype(grc.dtype)
    out_ctx = jnp.concatenate([o_attn_c, o_lru_c], axis=-1) @ w_out
    return out_lat, out_ctx


def swiglu(h, w1, w3, w2):
    return (jax.nn.silu(h @ w1) * (h @ w3)) @ w2


def moe_swiglu(h, router, router_b, w1, w3, w2):
    logits = (h @ router).astype(jnp.float32) + router_b.astype(jnp.float32)
    top_val, top_idx = lax.top_k(logits, TOP_K)
    wts = jax.nn.softmax(top_val, axis=-1)
    gates = jnp.sum(jax.nn.one_hot(top_idx, N_EXPERTS, dtype=jnp.float32) * wts[..., None], axis=-2).astype(h.dtype)
    y = gates[..., 0:1] * swiglu(h, w1[0], w3[0], w2[0])
    for e in range(1, N_EXPERTS):
        y = y + gates[..., e:e + 1] * swiglu(h, w1[e], w3[e], w2[e])
    return y


def setup_inputs(seed: int = 0) -> dict:
    key = jax.random.key(seed)
    ks = jax.random.split(key, 28)
    f32 = jnp.float32
    D = D_MODEL

    def nrm(k, shape, s):
        return jax.random.normal(k, shape, f32) * s

    u = jax.random.uniform(ks[17], (DEPTH, 2, LRU_WIDTH), f32, 0.9, 0.999)
    a0 = u ** (1.0 / LRU_C)
    lam = jnp.log(a0) - jnp.log1p(-a0)
    return {
        'x': nrm(ks[0], (BATCH, SEQ, D), 1.0),
        'c': nrm(ks[1], (BATCH, D), 1.0),
        'ctx': nrm(ks[2], (BATCH, CTX_LEN, D), 1.0),
        'c_ctx': nrm(ks[3], (D,), 1.0),
        'ada_w': nrm(ks[4], (DEPTH, D, 6 * D), 0.3 * D ** -0.5),
        'ada_b': nrm(ks[5], (DEPTH, 6 * D), 0.02),
        'mix_norm_g': 1.0 + nrm(ks[6], (DEPTH, D), 0.05),
        'ffn_norm_g': 1.0 + nrm(ks[7], (DEPTH, D), 0.05),
        'w_in': nrm(ks[8], (DEPTH, D, IN_COLS), D ** -0.5),
        'w_out': nrm(ks[9], (DEPTH, MIX_WIDTH, D), MIX_WIDTH ** -0.5),
        'na_rpb': nrm(ks[10], (DEPTH, ATTN_HEADS, 2 * NA_KH - 1, 2 * NA_KW - 1), 0.02),
        'conv_w': nrm(ks[11], (DEPTH, CONV_W, LRU_WIDTH), CONV_W ** -0.5),
        'conv_b': nrm(ks[12], (DEPTH, LRU_WIDTH), 0.02),
        'lru_wa': nrm(ks[13], (DEPTH, 2, LRU_BLOCKS, LRU_BLOCK, LRU_BLOCK), LRU_BLOCK ** -0.5),
        'lru_ba': nrm(ks[14], (DEPTH, 2, LRU_WIDTH), 0.02),
        'lru_wx': nrm(ks[15], (DEPTH, 2, LRU_BLOCKS, LRU_BLOCK, LRU_BLOCK), LRU_BLOCK ** -0.5),
        'lru_bx': nrm(ks[16], (DEPTH, 2, LRU_WIDTH), 0.02),
        'lru_lam': lam,
        'ffn_w1': nrm(ks[18], (N_DENSE, D, D_FF), D ** -0.5),
        'ffn_w3': nrm(ks[19], (N_DENSE, D, D_FF), D ** -0.5),
        'ffn_w2': nrm(ks[20], (N_DENSE, D_FF, D), D_FF ** -0.5),
        'moe_router': nrm(ks[21], (N_MOE, D, N_EXPERTS), D ** -0.5),
        'moe_router_b': nrm(ks[22], (N_MOE, N_EXPERTS), 0.01),
        'moe_w1': nrm(ks[23], (N_MOE, N_EXPERTS, D, D_FF), D ** -0.5),
        'moe_w3': nrm(ks[24], (N_MOE, N_EXPERTS, D, D_FF), D ** -0.5),
        'moe_w2': nrm(ks[25], (N_MOE, N_EXPERTS, D_FF, D), D_FF ** -0.5),
        'final_g': 1.0 + nrm(ks[26], (D,), 0.05),
    }


def reference(x, c, ctx, c_ctx, ada_w, ada_b, mix_norm_g, ffn_norm_g, w_in, w_out, na_rpb, conv_w, conv_b, lru_wa, lru_ba, lru_wx, lru_bx, lru_lam, ffn_w1, ffn_w3, ffn_w2, moe_router, moe_router_b, moe_w1, moe_w3, moe_w2, final_g):
    silu_c = jax.nn.silu(c)[:, None, :]
    silu_cc = jax.nn.silu(c_ctx)[None, None, :]
    h_lat, h_ctx = x, ctx
    for i in range(DEPTH):
        last = i == DEPTH - 1
        sh1, sc1, g1, sh2, sc2, g2 = jnp.split(silu_c @ ada_w[i] + ada_b[i], 6, axis=-1)
        csh1, csc1, cg1, csh2, csc2, cg2 = jnp.split(silu_cc @ ada_w[i] + ada_b[i], 6, axis=-1)
        a_lat = modulate(rms_norm(h_lat, mix_norm_g[i]), sh1, sc1)
        a_ctx = modulate(rms_norm(h_ctx, mix_norm_g[i]), csh1, csc1)
        o_lat, o_ctx = mixing(a_lat, a_ctx, w_in[i], w_out[i], na_rpb[i], conv_w[i], conv_b[i],
                              lru_wa[i], lru_ba[i], lru_wx[i], lru_bx[i], lru_lam[i], not last)
        h_lat = h_lat + g1 * o_lat
        if i % 2 == 0:
            j = i // 2
            ffn = lambda t, j=j: swiglu(t, ffn_w1[j], ffn_w3[j], ffn_w2[j])
        else:
            j = i // 2
            ffn = lambda t, j=j: moe_swiglu(t, moe_router[j], moe_router_b[j], moe_w1[j], moe_w3[j], moe_w2[j])
        h_lat = h_lat + g2 * ffn(modulate(rms_norm(h_lat, ffn_norm_g[i]), sh2, sc2))
        if not last:
            h_ctx = h_ctx + cg1 * o_ctx
            h_ctx = h_ctx + cg2 * ffn(modulate(rms_norm(h_ctx, ffn_norm_g[i]), csh2, csc2))
    return rms_norm(h_lat, final_g)
```

```python
import functools

import jax
import jax.numpy as jnp
from jax import lax
from jax.experimental import pallas as pl
from jax.experimental.pallas import tpu as pltpu

F32 = jnp.float32
BF16 = jnp.bfloat16
I32 = jnp.int32

GRID_W = 64
HEAD_DIM = 64
N_HEADS = 8
NA_KH = 8
NA_KW = 16
LRU_BLOCK = 64
LRU_C = 8.0
CONV_W = 4
N_EXPERTS = 8
EPS = 1e-6
NEG_INF = -1e30
LANES = 128
SUBLANES = 8
VMEM_BYTES = 64 * 1024 * 1024

FF_CHUNK = 512
ROW_TILE_FFN = 1024
ROW_TILE = 512
GATHER_TILE = 256


def _params(semantics, vmem_mb):
    return pltpu.CompilerParams(dimension_semantics=semantics,
                                vmem_limit_bytes=min(vmem_mb * 1024 * 1024, VMEM_BYTES - 4 * 1024 * 1024))


def _rms_mod(x, g, sh, sc):
    y = x * lax.rsqrt(jnp.mean(x * x, axis=-1, keepdims=True) + EPS)
    return (y * g) * (1.0 + sc) + sh


def _ada_kernel(c_ref, w_ref, b_ref, o_ref):
    c = c_ref[...]
    s = c * jax.nn.sigmoid(c)
    o_ref[...] = jnp.dot(s, w_ref[...], preferred_element_type=F32,
                         precision=lax.Precision.HIGHEST) + b_ref[...]


def _ada_call(c_all, ada_w, ada_b):
    depth, d, n = ada_w.shape
    rows = c_all.shape[0]
    tn = 1024
    return pl.pallas_call(
        _ada_kernel,
        grid=(depth, n // tn),
        in_specs=[pl.BlockSpec((rows, d), lambda l, j: (0, 0)),
                  pl.BlockSpec((None, d, tn), lambda l, j: (l, 0, j)),
                  pl.BlockSpec((None, 1, tn), lambda l, j: (l, 0, j))],
        out_specs=pl.BlockSpec((None, rows, tn), lambda l, j: (l, 0, j)),
        out_shape=jax.ShapeDtypeStruct((depth, rows, n), F32),
        compiler_params=_params(("arbitrary", "arbitrary"), 24),
        name="ada_mod",
    )(c_all, ada_w, ada_b.reshape(depth, 1, n))


def _in_proj_kernel(x_ref, g_ref, sh_ref, sc_ref, w_ref, q_ref, k_ref, v_ref, xr_ref, gr_ref, *, aw, lw):
    a = _rms_mod(x_ref[...], g_ref[...], sh_ref[...], sc_ref[...]).astype(BF16)
    r = jnp.dot(a, w_ref[...], preferred_element_type=F32)
    q_ref[...] = (r[:, 0:aw] * (HEAD_DIM ** -0.5)).astype(BF16)
    k_ref[...] = r[:, aw:2 * aw].astype(BF16)
    v_ref[...] = r[:, 2 * aw:3 * aw].astype(BF16)
    xr_ref[...] = r[:, 3 * aw:3 * aw + lw]
    gr_ref[...] = r[:, 3 * aw + lw:3 * aw + 2 * lw]


def _mod_spec(d, col, layer, rows_per_mod, tm, fixed_row):
    if fixed_row is None:
        return pl.BlockSpec((None, None, 1, d), lambda i, *_: (layer, (i * tm) // rows_per_mod, 0, col))
    return pl.BlockSpec((None, None, 1, d), lambda i, *_: (layer, fixed_row, 0, col))


def _in_proj_call(x, g, mods, layer, w_bf, rows_per_mod, fixed_row, aw, lw):
    r, d = x.shape
    tm = min(ROW_TILE, r)
    ncol = w_bf.shape[1]
    ms = functools.partial(_mod_spec, d, layer=layer, rows_per_mod=rows_per_mod, tm=tm, fixed_row=fixed_row)
    row = lambda w: pl.BlockSpec((tm, w), lambda i: (i, 0))
    return pl.pallas_call(
        functools.partial(_in_proj_kernel, aw=aw, lw=lw),
        grid=(r // tm,),
        in_specs=[row(d), pl.BlockSpec((1, d), lambda i: (0, 0)), ms(col=0), ms(col=1),
                  pl.BlockSpec((d, ncol), lambda i: (0, 0))],
        out_specs=[row(aw), row(aw), row(aw), row(lw), row(lw)],
        out_shape=[jax.ShapeDtypeStruct((r, aw), BF16)] * 3 + [jax.ShapeDtypeStruct((r, lw), F32)] * 2,
        compiler_params=_params(("arbitrary",), 40),
        name="in_proj",
    )(x, g, mods, mods, w_bf)


def _softmax_pv(parts):
    m = None
    for s, _ in parts:
        mi = jnp.max(s, axis=-1, keepdims=True)
        m = mi if m is None else jnp.maximum(m, mi)
    l = None
    acc = None
    for s, v in parts:
        p = jnp.exp(s - m)
        li = jnp.sum(p, axis=-1, keepdims=True)
        ai = jnp.dot(p.astype(BF16), v, preferred_element_type=F32)
        l = li if l is None else l + li
        acc = ai if acc is None else acc + ai
    return acc * (1.0 / l)


def _qk(q, k):
    return lax.dot_general(q, k, (((1,), (1,)), ((), ())), preferred_element_type=F32)


def _na_kernel(q_ref, kl_ref, vl_ref, kc_ref, vc_ref, bias_ref, o_ref, *, rows):
    r = pl.program_id(1)
    rs = jnp.clip(r - NA_KH // 2, 0, rows - NA_KH)
    delta = r - rs
    k0 = pl.multiple_of(rs * GRID_W, GRID_W)
    nwin = NA_KH * GRID_W
    lane = lax.broadcasted_iota(I32, (GRID_W, 2 * HEAD_DIM), 1)
    first = lane < HEAD_DIM
    for pair in range(N_HEADS // 2):
        sl = slice(pair * 2 * HEAD_DIM, (pair + 1) * 2 * HEAD_DIM)
        q2 = q_ref[:, sl]
        kw = kl_ref[pl.ds(k0, nwin), sl]
        vw = vl_ref[pl.ds(k0, nwin), sl]
        kc = kc_ref[:, sl]
        vc = vc_ref[:, sl]
        outs = []
        for sub in range(2):
            qh = jnp.where(first if sub == 0 else ~first, q2, jnp.zeros_like(q2))
            s_w = _qk(qh, kw) + bias_ref[delta, pair * 2 + sub]
            s_c = _qk(qh, kc)
            outs.append(_softmax_pv([(s_w, vw), (s_c, vc)]))
        o_ref[:, sl] = jnp.where(first, outs[0], outs[1]).astype(o_ref.dtype)


def _na_call(q, kl, vl, kc, vc, bias, batch):
    s = q.shape[0] // batch
    l = kc.shape[0] // batch
    w = q.shape[1]
    rows = s // GRID_W
    nb = bias.shape
    return pl.pallas_call(
        functools.partial(_na_kernel, rows=rows),
        grid=(batch, rows),
        in_specs=[pl.BlockSpec((GRID_W, w), lambda b, r: (b * rows + r, 0)),
                  pl.BlockSpec((s, w), lambda b, r: (b, 0)),
                  pl.BlockSpec((s, w), lambda b, r: (b, 0)),
                  pl.BlockSpec((l, w), lambda b, r: (b, 0)),
                  pl.BlockSpec((l, w), lambda b, r: (b, 0)),
                  pl.BlockSpec(nb, lambda b, r: (0, 0, 0, 0))],
        out_specs=pl.BlockSpec((GRID_W, w), lambda b, r: (b * rows + r, 0)),
        out_shape=jax.ShapeDtypeStruct(q.shape, BF16),
        compiler_params=_params(("arbitrary", "arbitrary"), 48),
        name="na_attention",
    )(q, kl, vl, kc, vc, bias)


def _ctx_attn_kernel(q_ref, k_ref, v_ref, o_ref):
    rows = q_ref.shape[0]
    lane = lax.broadcasted_iota(I32, (rows, 2 * HEAD_DIM), 1)
    first = lane < HEAD_DIM
    for pair in range(N_HEADS // 2):
        sl = slice(pair * 2 * HEAD_DIM, (pair + 1) * 2 * HEAD_DIM)
        q2 = q_ref[:, sl]
        k = k_ref[:, sl]
        v = v_ref[:, sl]
        outs = []
        for sub in range(2):
            qh = jnp.where(first if sub == 0 else ~first, q2, jnp.zeros_like(q2))
            outs.append(_softmax_pv([(_qk(qh, k), v)]))
        o_ref[:, sl] = jnp.where(first, outs[0], outs[1]).astype(o_ref.dtype)


def _ctx_attn_call(q, k, v, batch):
    l = q.shape[0] // batch
    w = q.shape[1]
    spec = pl.BlockSpec((l, w), lambda b: (b, 0))
    return pl.pallas_call(
        _ctx_attn_kernel,
        grid=(batch,),
        in_specs=[spec, spec, spec],
        out_specs=spec,
        out_shape=jax.ShapeDtypeStruct(q.shape, BF16),
        compiler_params=_params(("arbitrary",), 24),
        name="ctx_attention",
    )(q, k, v)


def _na_bias_table(rpb, rows):
    kh = min(NA_KH, rows)
    col = jnp.arange(GRID_W)
    col_start = jnp.clip(col - NA_KW // 2, 0, GRID_W - NA_KW)
    col_ok = (col[None, :] >= col_start[:, None]) & (col[None, :] < col_start[:, None] + NA_KW)
    dc = jnp.clip(col[None, :] - col[:, None], -(NA_KW - 1), NA_KW - 1) + (NA_KW - 1)
    dr = jnp.arange(kh)[None, :] - jnp.arange(NA_KH)[:, None] + (NA_KH - 1)
    b = rpb.astype(F32)[:, dr[:, None, :, None], dc[None, :, None, :]]
    b = jnp.where(col_ok[None, None, :, None, :], b, NEG_INF)
    return b.transpose(1, 0, 2, 3, 4).reshape(NA_KH, rpb.shape[0], GRID_W, kh * GRID_W)


def _softplus(x):
    return jnp.maximum(x, 0.0) + jnp.log1p(jnp.exp(-jnp.abs(x)))


def _group_scan(a, b, reverse):
    n = a.shape[0]
    sub = lax.broadcasted_iota(I32, a.shape, 0) & (SUBLANES - 1)
    s = 1
    while s < SUBLANES:
        if reverse:
            ok = sub < SUBLANES - s
            shift = n - s
        else:
            ok = sub >= s
            shift = s
        a_n = jnp.where(ok, pltpu.roll(a, shift, 0), 1.0)
        b_n = jnp.where(ok, pltpu.roll(b, shift, 0), 0.0)
        b = a * b_n + b
        a = a * a_n
        s *= 2
    return a, b


def _lru_kernel(xl_ref, gl_ref, xc_ref, gc_ref, cw_ref, cb_ref, wg_ref, bg_ref, lam_ref, ol_ref, oc_ref,
                pad_ref, al_ref, bl_ref, ac_ref, bc_ref, *, chunk):
    width = xl_ref.shape[1]
    sp = [_softplus(-lam_ref[d:d + 1, :]) for d in range(2)]
    cb = cb_ref[...]
    cw = [cw_ref[j:j + 1, :] for j in range(CONV_W)]
    zero8 = jnp.zeros((SUBLANES, width), F32)

    def coeffs(x_ref, a_ref, b_ref):
        t = x_ref.shape[0]
        ch = min(chunk, t)
        pad_ref[0:SUBLANES, :] = zero8
        pad_ref[SUBLANES:SUBLANES + t, :] = x_ref[...]
        pad_ref[SUBLANES + t:2 * SUBLANES + t, :] = zero8

        def body(c, carry):
            r0 = pl.multiple_of(c * ch, SUBLANES)
            win = pad_ref[pl.ds(r0, ch + 2 * SUBLANES), :]
            u = cb
            for j in range(CONV_W):
                o = SUBLANES - 1 + j
                u = u + win[o:o + ch, :] * cw[j]
            ub = u.astype(BF16)
            for d in range(2):
                ga = jnp.dot(ub, wg_ref[d, 0], preferred_element_type=F32) + bg_ref[d, 0:1, :]
                gi = jnp.dot(ub, wg_ref[d, 1], preferred_element_type=F32) + bg_ref[d, 1:2, :]
                log_a = (-LRU_C) * jax.nn.sigmoid(ga) * sp[d]
                a = jnp.exp(log_a)
                bco = jnp.sqrt(1.0 - a * a) * jax.nn.sigmoid(gi) * u
                a, bco = _group_scan(a, bco, reverse=(d == 1))
                a_ref[d, pl.ds(r0, ch), :] = a
                b_ref[d, pl.ds(r0, ch), :] = bco
            return carry

        lax.fori_loop(0, t // ch, body, 0)

    def carry_pass(a_ref, b_ref, cf, cr):
        n = a_ref.shape[1] // SUBLANES

        def body(j, carry):
            cf, cr = carry
            rf = pl.multiple_of(j * SUBLANES, SUBLANES)
            rr = pl.multiple_of((n - 1 - j) * SUBLANES, SUBLANES)
            hf = a_ref[0, pl.ds(rf, SUBLANES), :] * cf + b_ref[0, pl.ds(rf, SUBLANES), :]
            hr = a_ref[1, pl.ds(rr, SUBLANES), :] * cr + b_ref[1, pl.ds(rr, SUBLANES), :]
            b_ref[0, pl.ds(rf, SUBLANES), :] = hf
            b_ref[1, pl.ds(rr, SUBLANES), :] = hr
            return (jnp.broadcast_to(hf[SUBLANES - 1:SUBLANES, :], hf.shape),
                    jnp.broadcast_to(hr[0:1, :], hr.shape))

        return lax.fori_loop(0, n, body, (cf, cr))

    def emit(g_ref, b_ref, o_ref):
        t = g_ref.shape[0]
        ch = min(chunk, t)

        def body(c, carry):
            r0 = pl.multiple_of(c * ch, SUBLANES)
            y = b_ref[0, pl.ds(r0, ch), :] + b_ref[1, pl.ds(r0, ch), :]
            o_ref[pl.ds(r0, ch), :] = (jax.nn.gelu(g_ref[pl.ds(r0, ch), :]) * y).astype(o_ref.dtype)
            return carry

        lax.fori_loop(0, t // ch, body, 0)

    coeffs(xc_ref, ac_ref, bc_ref)
    coeffs(xl_ref, al_ref, bl_ref)
    cf, cr = carry_pass(ac_ref, bc_ref, zero8, zero8)
    carry_pass(al_ref, bl_ref, cf, cr)
    emit(gl_ref, bl_ref, ol_ref)
    emit(gc_ref, bc_ref, oc_ref)


def _lru_call(xr_l, gr_l, xr_c, gr_c, conv_w, conv_b, wg, bg, lam, batch):
    s = xr_l.shape[0] // batch
    l = xr_c.shape[0] // batch
    lw = xr_l.shape[1]
    half = wg.shape[-1]
    nh = lw // half
    lat = pl.BlockSpec((s, half), lambda b, c: (b, c))
    ctx = pl.BlockSpec((l, half), lambda b, c: (b, c))
    return pl.pallas_call(
        functools.partial(_lru_kernel, chunk=256),
        grid=(batch, nh),
        in_specs=[lat, lat, ctx, ctx,
                  pl.BlockSpec((CONV_W, half), lambda b, c: (0, c)),
                  pl.BlockSpec((1, half), lambda b, c: (0, c)),
                  pl.BlockSpec((2, 2, None, half, half), lambda b, c: (0, 0, c, 0, 0)),
                  pl.BlockSpec((2, 2, half), lambda b, c: (0, 0, c)),
                  pl.BlockSpec((2, half), lambda b, c: (0, c))],
        out_specs=[lat, ctx],
        out_shape=[jax.ShapeDtypeStruct(xr_l.shape, BF16), jax.ShapeDtypeStruct(xr_c.shape, BF16)],
        scratch_shapes=[pltpu.VMEM((s + 2 * SUBLANES, half), F32),
                        pltpu.VMEM((2, s, half), F32), pltpu.VMEM((2, s, half), F32),
                        pltpu.VMEM((2, l, half), F32), pltpu.VMEM((2, l, half), F32)],
        compiler_params=_params(("arbitrary", "arbitrary"), 48),
        name="rglru",
    )(xr_l, gr_l, xr_c, gr_c, conv_w, conv_b.reshape(1, lw), wg, bg, lam)


def _lru_gate_weights(wa, ba, wx, bx, half):
    nblk = wa.shape[1]
    lw = nblk * LRU_BLOCK
    per = half // LRU_BLOCK

    def dense(w):
        w = w.reshape(2, nblk // per, per, LRU_BLOCK, LRU_BLOCK)
        eye = jnp.eye(per, dtype=w.dtype)
        full = w[:, :, :, :, None, :] * eye[None, None, :, None, :, None]
        return full.reshape(2, nblk // per, half, half)

    wg = jnp.stack([dense(wa), dense(wx)], axis=1).astype(BF16)
    bg = jnp.stack([ba, bx], axis=1).astype(F32)
    return wg, bg


def _out_proj_kernel(oa_ref, ol_ref, h_ref, g1_ref, w_ref, o_ref, *, aw):
    o = jnp.dot(oa_ref[...], w_ref[0:aw, :], preferred_element_type=F32)
    o = o + jnp.dot(ol_ref[...], w_ref[aw:, :], preferred_element_type=F32)
    o_ref[...] = h_ref[...] + g1_ref[...] * o


def _out_proj_call(oa, ol, h, mods, layer, w_bf, rows_per_mod, fixed_row):
    r, d = h.shape
    aw = oa.shape[1]
    lw = ol.shape[1]
    tm = min(ROW_TILE, r)
    ms = functools.partial(_mod_spec, d, layer=layer, rows_per_mod=rows_per_mod, tm=tm, fixed_row=fixed_row)
    row = lambda w: pl.BlockSpec((tm, w), lambda i: (i, 0))
    return pl.pallas_call(
        functools.partial(_out_proj_kernel, aw=aw),
        grid=(r // tm,),
        in_specs=[row(aw), row(lw), row(d), ms(col=2), pl.BlockSpec((aw + lw, d), lambda i: (0, 0))],
        out_specs=row(d),
        out_shape=jax.ShapeDtypeStruct((r, d), F32),
        compiler_params=_params(("arbitrary",), 32),
        name="out_proj",
    )(oa, ol, h, mods, w_bf)


def _swiglu_step(a_bf, w1_ref, w3_ref, w2_ref, acc_ref):
    g = jnp.dot(a_bf, w1_ref[...].astype(BF16), preferred_element_type=F32)
    u = jnp.dot(a_bf, w3_ref[...].astype(BF16), preferred_element_type=F32)
    hmid = (g * jax.nn.sigmoid(g) * u).astype(BF16)
    acc_ref[...] += jnp.dot(hmid, w2_ref[...].astype(BF16), preferred_element_type=F32)


def _ffn_kernel(x_ref, g_ref, sh_ref, sc_ref, g2_ref, w1_ref, w3_ref, w2_ref, o_ref, a_ref, acc_ref):
    c = pl.program_id(1)

    @pl.when(c == 0)
    def _():
        a_ref[...] = _rms_mod(x_ref[...], g_ref[...], sh_ref[...], sc_ref[...]).astype(BF16)
        acc_ref[...] = jnp.zeros_like(acc_ref)

    _swiglu_step(a_ref[...], w1_ref, w3_ref, w2_ref, acc_ref)

    @pl.when(c == pl.num_programs(1) - 1)
    def _():
        o_ref[...] = x_ref[...] + g2_ref[...] * acc_ref[...]


def _ffn_call(h, g, mods, layer, w1, w3, w2, rows_per_mod, fixed_row):
    r, d = h.shape
    dff = w1.shape[1]
    tm = min(ROW_TILE_FFN, r)
    ms = functools.partial(_mod_spec, d, layer=layer, rows_per_mod=rows_per_mod, tm=tm, fixed_row=fixed_row)
    return pl.pallas_call(
        _ffn_kernel,
        grid=(r // tm, dff // FF_CHUNK),
        in_specs=[pl.BlockSpec((tm, d), lambda i, c: (i, 0)),
                  pl.BlockSpec((1, d), lambda i, c: (0, 0)),
                  ms(col=3), ms(col=4), ms(col=5),
                  pl.BlockSpec((d, FF_CHUNK), lambda i, c: (0, c)),
                  pl.BlockSpec((d, FF_CHUNK), lambda i, c: (0, c)),
                  pl.BlockSpec((FF_CHUNK, d), lambda i, c: (c, 0))],
        out_specs=pl.BlockSpec((tm, d), lambda i, c: (i, 0)),
        out_shape=jax.ShapeDtypeStruct((r, d), F32),
        scratch_shapes=[pltpu.VMEM((tm, d), BF16), pltpu.VMEM((tm, d), F32)],
        compiler_params=_params(("arbitrary", "arbitrary"), 56),
        name="ffn_dense",
    )(h, g, mods, mods, mods, w1, w3, w2)


def _gffn_kernel(te_ref, nu_ref, x_ref, w1_ref, w3_ref, w2_ref, o_ref, a_ref, acc_ref):
    i = pl.program_id(0)
    c = pl.program_id(1)

    @pl.when(i < nu_ref[0])
    def _():
        @pl.when(c == 0)
        def _():
            a_ref[...] = x_ref[...].astype(BF16)
            acc_ref[...] = jnp.zeros_like(acc_ref)

        _swiglu_step(a_ref[...], w1_ref, w3_ref, w2_ref, acc_ref)

        @pl.when(c == pl.num_programs(1) - 1)
        def _():
            o_ref[...] = acc_ref[...]

    @pl.when((i >= nu_ref[0]) & (c == pl.num_programs(1) - 1))
    def _():
        o_ref[...] = jnp.zeros_like(o_ref)


def _gffn_call(tile_expert, n_used, xg, w1, w3, w2):
    rg, d = xg.shape
    dff = w1.shape[2]
    tm = ROW_TILE_FFN
    nc = dff // FF_CHUNK

    def tile(i, nu):
        return jnp.minimum(i, nu[0] - 1)

    def chunk(i, c, nu):
        return jnp.where(i < nu[0], c, nc - 1)

    grid_spec = pltpu.PrefetchScalarGridSpec(
        num_scalar_prefetch=2,
        grid=(rg // tm, nc),
        in_specs=[pl.BlockSpec((tm, d), lambda i, c, te, nu: (tile(i, nu), 0)),
                  pl.BlockSpec((None, d, FF_CHUNK), lambda i, c, te, nu: (te[tile(i, nu)], 0, chunk(i, c, nu))),
                  pl.BlockSpec((None, d, FF_CHUNK), lambda i, c, te, nu: (te[tile(i, nu)], 0, chunk(i, c, nu))),
                  pl.BlockSpec((None, FF_CHUNK, d), lambda i, c, te, nu: (te[tile(i, nu)], chunk(i, c, nu), 0))],
        out_specs=pl.BlockSpec((tm, d), lambda i, c, te, nu: (i, 0)),
        scratch_shapes=[pltpu.VMEM((tm, d), BF16), pltpu.VMEM((tm, d), F32)])
    return pl.pallas_call(
        _gffn_kernel,
        grid_spec=grid_spec,
        out_shape=jax.ShapeDtypeStruct((rg, d), F32),
        compiler_params=_params(("arbitrary", "arbitrary"), 56),
        name="ffn_grouped",
    )(tile_expert, n_used, xg, w1, w3, w2)


def _route_kernel(x_ref, g_ref, sh_ref, sc_ref, wr_ref, br_ref, a_ref, tab_ref, gcol_ref, cnt_ref, carry_ref):
    i = pl.program_id(0)
    tm = x_ref.shape[0]

    @pl.when(i == 0)
    def _():
        carry_ref[...] = jnp.zeros_like(carry_ref)

    a = _rms_mod(x_ref[...], g_ref[...], sh_ref[...], sc_ref[...])
    a_ref[...] = a
    lane = lax.broadcasted_iota(I32, (tm, LANES), 1).astype(F32)
    logits = jnp.dot(a, wr_ref[...], preferred_element_type=F32, precision=lax.Precision.HIGHEST) + br_ref[...]
    logits = jnp.where(lane < N_EXPERTS, logits, NEG_INF)
    m1 = jnp.max(logits, axis=-1, keepdims=True)
    i1 = jnp.min(jnp.where(logits == m1, lane, float(LANES)), axis=-1, keepdims=True)
    rest = jnp.where(lane == i1, 2.0 * NEG_INF, logits)
    m2 = jnp.max(rest, axis=-1, keepdims=True)
    i2 = jnp.min(jnp.where(rest == m2, lane, float(LANES)), axis=-1, keepdims=True)
    e = jnp.exp(m2 - m1)
    w1 = 1.0 / (1.0 + e)
    w2 = e * w1
    sel1 = lane == i1
    sel2 = lane == i2
    onehot = jnp.where(sel1 | sel2, 1.0, 0.0)
    row = lax.broadcasted_iota(I32, (tm, tm), 0)
    col = lax.broadcasted_iota(I32, (tm, tm), 1)
    before = jnp.where(col < row, 1.0, 0.0).astype(BF16)
    cum = jnp.dot(before, onehot.astype(BF16), preferred_element_type=F32) + carry_ref[...]
    r1 = jnp.sum(jnp.where(sel1, cum, 0.0), axis=-1, keepdims=True)
    r2 = jnp.sum(jnp.where(sel2, cum, 0.0), axis=-1, keepdims=True)
    carry_ref[...] += jnp.sum(onehot, axis=0, keepdims=True)
    cnt_ref[...] = carry_ref[...]
    cols = (jnp.where(lane == 0, i1, 0.0) + jnp.where(lane == 1, i2, 0.0)
            + jnp.where(lane == 2, r1, 0.0) + jnp.where(lane == 3, r2, 0.0))
    tab_ref[...] = cols.T[0:SUBLANES, :]
    gcol_ref[...] = (jnp.where(lane == 0, w1, 0.0) + jnp.where(lane == 1, w2, 0.0))[:, 0:SUBLANES]


def _route_call(h, g, mods, layer, wr, br, rows_per_mod):
    n, d = h.shape
    tm = min(ROW_TILE, n)
    ms = functools.partial(_mod_spec, d, layer=layer, rows_per_mod=rows_per_mod, tm=tm, fixed_row=None)
    return pl.pallas_call(
        _route_kernel,
        grid=(n // tm,),
        in_specs=[pl.BlockSpec((tm, d), lambda i: (i, 0)),
                  pl.BlockSpec((1, d), lambda i: (0, 0)), ms(col=3), ms(col=4),
                  pl.BlockSpec((d, LANES), lambda i: (0, 0)),
                  pl.BlockSpec((1, LANES), lambda i: (0, 0))],
        out_specs=[pl.BlockSpec((tm, d), lambda i: (i, 0)),
                   pl.BlockSpec((SUBLANES, tm), lambda i: (0, i)),
                   pl.BlockSpec((tm, SUBLANES), lambda i: (i, 0)),
                   pl.BlockSpec((1, LANES), lambda i: (0, 0))],
        out_shape=[jax.ShapeDtypeStruct((n, d), F32),
                   jax.ShapeDtypeStruct((SUBLANES, n), F32),
                   jax.ShapeDtypeStruct((n, SUBLANES), F32),
                   jax.ShapeDtypeStruct((1, LANES), F32)],
        scratch_shapes=[pltpu.VMEM((1, LANES), F32)],
        compiler_params=_params(("arbitrary",), 40),
        name="moe_route",
    )(h, g, mods, mods, wr, br)


def _dispatch_kernel(off_ref, tab_ref, a_hbm, xg_in, xg_out, pos_ref, sem):
    del xg_in
    i = pl.program_id(0)
    tm = tab_ref.shape[1]

    def row_copy(src, dst):
        return pltpu.make_async_copy(a_hbm.at[pl.ds(src, 1)], xg_out.at[pl.ds(dst, 1)], sem)

    def issue(t, carry):
        for k in range(2):
            p = off_ref[tab_ref[k, t]] + tab_ref[2 + k, t]
            pos_ref[k, t] = p
            row_copy(i * tm + t, p).start()
        return carry

    lax.fori_loop(0, tm, issue, 0)

    def drain(t, carry):
        for k in range(2):
            row_copy(0, 0).wait()
        return carry

    lax.fori_loop(0, tm, drain, 0)


def _dispatch_call(off, tab_i, a, rows_g):
    n, d = a.shape
    tm = min(ROW_TILE, n)
    grid_spec = pltpu.PrefetchScalarGridSpec(
        num_scalar_prefetch=1,
        grid=(n // tm,),
        in_specs=[pl.BlockSpec((4, tm), lambda i, off: (0, i), memory_space=pltpu.SMEM),
                  pl.BlockSpec(memory_space=pl.ANY),
                  pl.BlockSpec(memory_space=pl.ANY)],
        out_specs=[pl.BlockSpec(memory_space=pl.ANY),
                   pl.BlockSpec((2, tm), lambda i, off: (0, i), memory_space=pltpu.SMEM)],
        scratch_shapes=[pltpu.SemaphoreType.DMA(())])
    xg0 = jnp.zeros((rows_g, d), a.dtype)
    return pl.pallas_call(
        _dispatch_kernel,
        grid_spec=grid_spec,
        out_shape=[jax.ShapeDtypeStruct((rows_g, d), a.dtype), jax.ShapeDtypeStruct((2, n), I32)],
        input_output_aliases={3: 0},
        compiler_params=_params(("arbitrary",), 16),
        name="moe_dispatch",
    )(off, tab_i, a, xg0)


def _combine_kernel(pos_ref, posn_ref, y_hbm, h_ref, gcol_ref, g2_ref, fg_ref, o_ref, buf_ref, sem):
    i = pl.program_id(0)
    n = pl.num_programs(0)
    tm = h_ref.shape[0]

    def row_copy(src, slot, k, t):
        return pltpu.make_async_copy(y_hbm.at[pl.ds(src, 1)], buf_ref.at[slot, k, pl.ds(t, 1)], sem.at[slot])

    def issue(p_ref, slot):
        def body(t, carry):
            for k in range(2):
                row_copy(p_ref[k, t], slot, k, t).start()
            return carry
        lax.fori_loop(0, tm, body, 0)

    @pl.when(i == 0)
    def _():
        issue(pos_ref, 0)

    @pl.when(i + 1 < n)
    def _():
        issue(posn_ref, (i + 1) % 2)

    slot = i % 2

    def drain(t, carry):
        for k in range(2):
            row_copy(0, slot, k, 0).wait()
        return carry

    lax.fori_loop(0, tm, drain, 0)
    y = gcol_ref[:, 0:1] * buf_ref[slot, 0] + gcol_ref[:, 1:2] * buf_ref[slot, 1]
    hn = h_ref[...] + g2_ref[...] * y
    o_ref[...] = hn * lax.rsqrt(jnp.mean(hn * hn, axis=-1, keepdims=True) + EPS) * fg_ref[...]


def _combine_call(pos, y, h, gcol, mods, layer, final_g, rows_per_mod):
    n, d = h.shape
    tm = min(GATHER_TILE, n)
    nt = n // tm
    ms = functools.partial(_mod_spec, d, layer=layer, rows_per_mod=rows_per_mod, tm=tm, fixed_row=None)
    return pl.pallas_call(
        _combine_kernel,
        grid=(nt,),
        in_specs=[pl.BlockSpec((2, tm), lambda i: (0, i), memory_space=pltpu.SMEM),
                  pl.BlockSpec((2, tm), lambda i: (0, jnp.minimum(i + 1, nt - 1)), memory_space=pltpu.SMEM),
                  pl.BlockSpec(memory_space=pl.ANY),
                  pl.BlockSpec((tm, d), lambda i: (i, 0)),
                  pl.BlockSpec((tm, SUBLANES), lambda i: (i, 0)),
                  ms(col=5),
                  pl.BlockSpec((1, d), lambda i: (0, 0))],
        out_specs=pl.BlockSpec((tm, d), lambda i: (i, 0)),
        out_shape=jax.ShapeDtypeStruct((n, d), F32),
        scratch_shapes=[pltpu.VMEM((2, 2, tm, d), F32), pltpu.SemaphoreType.DMA((2,))],
        compiler_params=_params(("arbitrary",), 24),
        name="moe_combine",
    )(pos, pos, y, h, gcol, mods, final_g)


def _moe_layer(h, g, mods, layer, router, router_b, w1, w3, w2, final_g, rows_per_mod):
    n, d = h.shape
    wr = jnp.zeros((d, LANES), F32).at[:, :N_EXPERTS].set(router)
    br = jnp.zeros((1, LANES), F32).at[0, :N_EXPERTS].set(router_b)
    a, tab, gcol, cnt = _route_call(h, g, mods, layer, wr, br, rows_per_mod)
    tm = ROW_TILE_FFN
    n_tiles = (2 * n) // tm + N_EXPERTS
    counts = cnt[0, :N_EXPERTS].astype(I32)
    tiles = (counts + tm - 1) // tm
    ends = jnp.cumsum(tiles)
    off = ((ends - tiles) * tm).astype(I32)
    tile_expert = jnp.minimum(jnp.searchsorted(ends, jnp.arange(n_tiles, dtype=I32), side="right"),
                              N_EXPERTS - 1).astype(I32)
    n_used = ends[-1:].astype(I32)
    xg, pos = _dispatch_call(off, tab[0:4].astype(I32), a, n_tiles * tm)
    y = _gffn_call(tile_expert, n_used, xg, w1, w3, w2)
    return _combine_call(pos, y, h, gcol, mods, layer, final_g, rows_per_mod)


def kernel(x, c, ctx, c_ctx, ada_w, ada_b, mix_norm_g, ffn_norm_g, w_in, w_out, na_rpb, conv_w, conv_b, lru_wa, lru_ba, lru_wx, lru_bx, lru_lam, ffn_w1, ffn_w3, ffn_w2, moe_router, moe_router_b, moe_w1, moe_w3, moe_w2, final_g):
    batch, seq, d = x.shape
    ctx_len = ctx.shape[1]
    depth = ada_w.shape[0]
    lw = conv_w.shape[-1]
    aw = (w_in.shape[-1] - 2 * lw) // 3
    rows = seq // GRID_W
    ctx_row = batch

    mod_rows = -(-(batch + 1) // SUBLANES) * SUBLANES
    c_all = jnp.zeros((mod_rows, d), F32).at[:batch].set(c).at[ctx_row].set(c_ctx)
    mods = _ada_call(c_all, ada_w, ada_b).reshape(depth, mod_rows, 1, 6 * d)

    h_lat = x.reshape(batch * seq, d)
    h_ctx = ctx.reshape(batch * ctx_len, d)
    half = min(lw, 2 * LANES)
    out = None
    for i in range(depth):
        last = i == depth - 1
        w_in_bf = w_in[i].astype(BF16)
        w_out_bf = w_out[i].astype(BF16)
        g_mix = mix_norm_g[i].reshape(1, d)
        g_ffn = ffn_norm_g[i].reshape(1, d)
        q_l, k_l, v_l, xr_l, gr_l = _in_proj_call(h_lat, g_mix, mods, i, w_in_bf, seq, None, aw, lw)
        q_c, k_c, v_c, xr_c, gr_c = _in_proj_call(h_ctx, g_mix, mods, i, w_in_bf, None, ctx_row, aw, lw)
        bias = _na_bias_table(na_rpb[i], rows)
        oa_l = _na_call(q_l, k_l, v_l, k_c, v_c, bias, batch)
        wg, bg = _lru_gate_weights(lru_wa[i], lru_ba[i], lru_wx[i], lru_bx[i], half)
        ol_l, ol_c = _lru_call(xr_l, gr_l, xr_c, gr_c, conv_w[i], conv_b[i], wg, bg, lru_lam[i], batch)
        h_lat = _out_proj_call(oa_l, ol_l, h_lat, mods, i, w_out_bf, seq, None)
        if not last:
            oa_c = _ctx_attn_call(q_c, k_c, v_c, batch)
            h_ctx = _out_proj_call(oa_c, ol_c, h_ctx, mods, i, w_out_bf, None, ctx_row)
        j = i // 2
        if i % 2 == 0:
            h_lat = _ffn_call(h_lat, g_ffn, mods, i, ffn_w1[j], ffn_w3[j], ffn_w2[j], seq, None)
            if not last:
                h_ctx = _ffn_call(h_ctx, g_ffn, mods, i, ffn_w1[j], ffn_w3[j], ffn_w2[j], None, ctx_row)
        else:
            assert last, "the routed layer is fused with the final norm"
            out = _moe_layer(h_lat, g_ffn, mods, i, moe_router[j], moe_router_b[j],
                             moe_w1[j], moe_w3[j], moe_w2[j], final_g.reshape(1, d), seq)
    return out.reshape(batch, seq, d)
```

```python
import functools

import jax
import jax.numpy as jnp
from jax import lax
from jax.experimental import pallas as pl
from jax.experimental.pallas import tpu as pltpu

F32 = jnp.float32
BF16 = jnp.bfloat16
I32 = jnp.int32

GRID_W = 64
HEAD_DIM = 64
N_HEADS = 8
NA_KH = 8
NA_KW = 16
LRU_BLOCK = 64
LRU_C = 8.0
CONV_W = 4
N_EXPERTS = 8
EPS = 1e-6
NEG_INF = -1e30
LANES = 128
SUBLANES = 8
VMEM_BYTES = 64 * 1024 * 1024

FF_CHUNK = 512
ROW_TILE_FFN = 1024
ROW_TILE = 512
GATHER_TILE = 256


def _params(semantics, vmem_mb):
    return pltpu.CompilerParams(dimension_semantics=semantics,
                                vmem_limit_bytes=min(vmem_mb * 1024 * 1024, VMEM_BYTES - 4 * 1024 * 1024))


def _rms_mod(x, g, sh, sc):
    y = x * lax.rsqrt(jnp.mean(x * x, axis=-1, keepdims=True) + EPS)
    return (y * g) * (1.0 + sc) + sh


def _ada_kernel(c_ref, w_ref, b_ref, o_ref):
    c = c_ref[...]
    s = c * jax.nn.sigmoid(c)
    o_ref[...] = jnp.dot(s, w_ref[...], preferred_element_type=F32,
                         precision=lax.Precision.HIGHEST) + b_ref[...]


def _ada_call(c_all, ada_w, ada_b):
    depth, d, n = ada_w.shape
    rows = c_all.shape[0]
    tn = 1024
    return pl.pallas_call(
        _ada_kernel,
        grid=(depth, n // tn),
        in_specs=[pl.BlockSpec((rows, d), lambda l, j: (0, 0)),
                  pl.BlockSpec((None, d, tn), lambda l, j: (l, 0, j)),
                  pl.BlockSpec((None, 1, tn), lambda l, j: (l, 0, j))],
        out_specs=pl.BlockSpec((None, rows, tn), lambda l, j: (l, 0, j)),
        out_shape=jax.ShapeDtypeStruct((depth, rows, n), F32),
        compiler_params=_params(("arbitrary", "arbitrary"), 24),
        name="ada_mod",
    )(c_all, ada_w, ada_b.reshape(depth, 1, n))


def _in_proj_kernel(x_ref, g_ref, sh_ref, sc_ref, w_ref, q_ref, k_ref, v_ref, xr_ref, gr_ref, *, aw, lw):
    a = _rms_mod(x_ref[...], g_ref[...], sh_ref[...], sc_ref[...]).astype(BF16)
    r = jnp.dot(a, w_ref[...], preferred_element_type=F32)
    q_ref[...] = (r[:, 0:aw] * (HEAD_DIM ** -0.5)).astype(BF16)
    k_ref[...] = r[:, aw:2 * aw].astype(BF16)
    v_ref[...] = r[:, 2 * aw:3 * aw].astype(BF16)
    xr_ref[...] = r[:, 3 * aw:3 * aw + lw]
    gr_ref[...] = r[:, 3 * aw + lw:3 * aw + 2 * lw]


def _mod_spec(d, col, layer, rows_per_mod, tm, fixed_row):
    if fixed_row is None:
        return pl.BlockSpec((None, None, 1, d), lambda i, *_: (layer, (i * tm) // rows_per_mod, 0, col))
    return pl.BlockSpec((None, None, 1, d), lambda i, *_: (layer, fixed_row, 0, col))


def _in_proj_call(x, g, mods, layer, w_bf, rows_per_mod, fixed_row, aw, lw):
    r, d = x.shape
    tm = min(ROW_TILE, r)
    ncol = w_bf.shape[1]
    ms = functools.partial(_mod_spec, d, layer=layer, rows_per_mod=rows_per_mod, tm=tm, fixed_row=fixed_row)
    row = lambda w: pl.BlockSpec((tm, w), lambda i: (i, 0))
    return pl.pallas_call(
        functools.partial(_in_proj_kernel, aw=aw, lw=lw),
        grid=(r // tm,),
        in_specs=[row(d), pl.BlockSpec((1, d), lambda i: (0, 0)), ms(col=0), ms(col=1),
                  pl.BlockSpec((d, ncol), lambda i: (0, 0))],
        out_specs=[row(aw), row(aw), row(aw), row(lw), row(lw)],
        out_shape=[jax.ShapeDtypeStruct((r, aw), BF16)] * 3 + [jax.ShapeDtypeStruct((r, lw), F32)] * 2,
        compiler_params=_params(("arbitrary",), 40),
        name="in_proj",
    )(x, g, mods, mods, w_bf)


def _softmax_pv(parts):
    m = None
    for s, _ in parts:
        mi = jnp.max(s, axis=-1, keepdims=True)
        m = mi if m is None else jnp.maximum(m, mi)
    l = None
    acc = None
    for s, v in parts:
        p = jnp.exp(s - m)
        li = jnp.sum(p, axis=-1, keepdims=True)
        ai = jnp.dot(p.astype(BF16), v, preferred_element_type=F32)
        l = li if l is None else l + li
        acc = ai if acc is None else acc + ai
    return acc * (1.0 / l)


def _qk(q, k):
    return lax.dot_general(q, k, (((1,), (1,)), ((), ())), preferred_element_type=F32)


def _build_na_bias(rpb_ref, bias_ref):
    lane = lax.broadcasted_iota(I32, (GRID_W, LANES), 1)
    q = lax.broadcasted_iota(I32, (GRID_W, LANES), 0)
    kcol = lane & (GRID_W - 1)
    cs = jnp.clip(q - NA_KW // 2, 0, GRID_W - NA_KW)
    ok = (kcol >= cs) & (kcol < cs + NA_KW)
    low = lane < GRID_W
    for delta in range(NA_KH):
        for h in range(N_HEADS):
            for jp in range(NA_KH // 2):
                halves = []
                for j in (2 * jp, 2 * jp + 1):
                    dr = j - delta + NA_KH - 1
                    w = jnp.broadcast_to(rpb_ref[h, dr:dr + 1, :], (GRID_W, LANES))
                    base = (j % 2) * GRID_W - (NA_KW - 1)
                    halves.append(pltpu.roll(w, base % LANES, 1, stride=1, stride_axis=0))
                t = jnp.where(low, halves[0], halves[1])
                bias_ref[delta, h, :, jp * LANES:(jp + 1) * LANES] = jnp.where(ok, t, NEG_INF)


def _na_kernel(q_ref, kl_ref, vl_ref, kc_ref, vc_ref, rpb_ref, o_ref, bias_ref, s_ref, p_ref, *, rows):
    @pl.when((pl.program_id(0) == 0) & (pl.program_id(1) == 0))
    def _():
        _build_na_bias(rpb_ref, bias_ref)

    r = pl.program_id(1)
    rs = jnp.clip(r - NA_KH // 2, 0, rows - NA_KH)
    delta = r - rs
    k0 = pl.multiple_of(rs * GRID_W, GRID_W)
    nwin = NA_KH * GRID_W
    lane = lax.broadcasted_iota(I32, (GRID_W, 2 * HEAD_DIM), 1)
    first = lane < HEAD_DIM
    for pair in range(N_HEADS // 2):
        sl = slice(pair * 2 * HEAD_DIM, (pair + 1) * 2 * HEAD_DIM)
        q2 = q_ref[:, sl]
        keys = jnp.concatenate([kl_ref[pl.ds(k0, nwin), sl], kc_ref[:, sl]], axis=0)
        for sub in range(2):
            h = pair * 2 + sub
            qh = jnp.where(first if sub == 0 else ~first, q2, jnp.zeros_like(q2))
            s = _qk(qh, keys)
            s_ref[h, :, 0:nwin] = s[:, 0:nwin] + bias_ref[delta, h]
            s_ref[h, :, nwin:] = s[:, nwin:]
    s = s_ref[...]
    p = jnp.exp(s - jnp.max(s, axis=-1, keepdims=True))
    inv = 1.0 / jnp.sum(p, axis=-1, keepdims=True)
    p_ref[...] = p.astype(BF16)
    for pair in range(N_HEADS // 2):
        sl = slice(pair * 2 * HEAD_DIM, (pair + 1) * 2 * HEAD_DIM)
        vals = jnp.concatenate([vl_ref[pl.ds(k0, nwin), sl], vc_ref[:, sl]], axis=0)
        outs = [jnp.dot(p_ref[pair * 2 + sub], vals, preferred_element_type=F32) * inv[pair * 2 + sub]
                for sub in range(2)]
        o_ref[:, sl] = jnp.where(first, outs[0], outs[1]).astype(o_ref.dtype)


def _na_call(q, kl, vl, kc, vc, rpb, batch):
    s = q.shape[0] // batch
    l = kc.shape[0] // batch
    w = q.shape[1]
    rows = s // GRID_W
    assert rows >= NA_KH
    nkeys = NA_KH * GRID_W + l
    return pl.pallas_call(
        functools.partial(_na_kernel, rows=rows),
        grid=(batch, rows),
        in_specs=[pl.BlockSpec((GRID_W, w), lambda b, r: (b * rows + r, 0)),
                  pl.BlockSpec((s, w), lambda b, r: (b, 0)),
                  pl.BlockSpec((s, w), lambda b, r: (b, 0)),
                  pl.BlockSpec((l, w), lambda b, r: (b, 0)),
                  pl.BlockSpec((l, w), lambda b, r: (b, 0)),
                  pl.BlockSpec(rpb.shape, lambda b, r: (0, 0, 0))],
        out_specs=pl.BlockSpec((GRID_W, w), lambda b, r: (b * rows + r, 0)),
        out_shape=jax.ShapeDtypeStruct(q.shape, BF16),
        scratch_shapes=[pltpu.VMEM((NA_KH, N_HEADS, GRID_W, NA_KH * GRID_W), F32),
                        pltpu.VMEM((N_HEADS, GRID_W, nkeys), F32),
                        pltpu.VMEM((N_HEADS, GRID_W, nkeys), BF16)],
        compiler_params=_params(("arbitrary", "arbitrary"), 48),
        name="na_attention",
    )(q, kl, vl, kc, vc, rpb)


def _pad_rpb(rpb):
    h, nr, nc = rpb.shape
    return jnp.zeros((h, 2 * NA_KH, LANES), F32).at[:, :nr, :nc].set(rpb.astype(F32))


def _ctx_attn_kernel(q_ref, k_ref, v_ref, o_ref):
    rows = q_ref.shape[0]
    lane = lax.broadcasted_iota(I32, (rows, 2 * HEAD_DIM), 1)
    first = lane < HEAD_DIM
    for pair in range(N_HEADS // 2):
        sl = slice(pair * 2 * HEAD_DIM, (pair + 1) * 2 * HEAD_DIM)
        q2 = q_ref[:, sl]
        k = k_ref[:, sl]
        v = v_ref[:, sl]
        outs = []
        for sub in range(2):
            qh = jnp.where(first if sub == 0 else ~first, q2, jnp.zeros_like(q2))
            outs.append(_softmax_pv([(_qk(qh, k), v)]))
        o_ref[:, sl] = jnp.where(first, outs[0], outs[1]).astype(o_ref.dtype)


def _ctx_attn_call(q, k, v, batch):
    l = q.shape[0] // batch
    w = q.shape[1]
    spec = pl.BlockSpec((l, w), lambda b: (b, 0))
    return pl.pallas_call(
        _ctx_attn_kernel,
        grid=(batch,),
        in_specs=[spec, spec, spec],
        out_specs=spec,
        out_shape=jax.ShapeDtypeStruct(q.shape, BF16),
        compiler_params=_params(("arbitrary",), 24),
        name="ctx_attention",
    )(q, k, v)


def _softplus(x):
    return jnp.maximum(x, 0.0) + jnp.log1p(jnp.exp(-jnp.abs(x)))


def _group_scan(a, b, reverse):
    n = a.shape[0]
    sub = lax.broadcasted_iota(I32, a.shape, 0) & (SUBLANES - 1)
    s = 1
    while s < SUBLANES:
        if reverse:
            ok = sub < SUBLANES - s
            shift = n - s
        else:
            ok = sub >= s
            shift = s
        a_n = jnp.where(ok, pltpu.roll(a, shift, 0), 1.0)
        b_n = jnp.where(ok, pltpu.roll(b, shift, 0), 0.0)
        b = a * b_n + b
        a = a * a_n
        s *= 2
    return a, b


def _lru_kernel(xl_ref, gl_ref, xc_ref, gc_ref, cw_ref, cb_ref, wg_ref, bg_ref, lam_ref, ol_ref, oc_ref,
                pad_ref, al_ref, bl_ref, ac_ref, bc_ref, *, chunk):
    width = xl_ref.shape[1]
    sp = [_softplus(-lam_ref[d:d + 1, :]) for d in range(2)]
    cb = cb_ref[...]
    cw = [cw_ref[j:j + 1, :] for j in range(CONV_W)]
    zero8 = jnp.zeros((SUBLANES, width), F32)

    def coeffs(x_ref, a_ref, b_ref):
        t = x_ref.shape[0]
        ch = min(chunk, t)
        pad_ref[0:SUBLANES, :] = zero8
        pad_ref[SUBLANES:SUBLANES + t, :] = x_ref[...]
        pad_ref[SUBLANES + t:2 * SUBLANES + t, :] = zero8

        def body(c, carry):
            r0 = pl.multiple_of(c * ch, SUBLANES)
            win = pad_ref[pl.ds(r0, ch + 2 * SUBLANES), :]
            u = cb
            for j in range(CONV_W):
                o = SUBLANES - 1 + j
                u = u + win[o:o + ch, :] * cw[j]
            ub = u.astype(BF16)
            for d in range(2):
                ga = jnp.dot(ub, wg_ref[d, 0], preferred_element_type=F32) + bg_ref[d, 0:1, :]
                gi = jnp.dot(ub, wg_ref[d, 1], preferred_element_type=F32) + bg_ref[d, 1:2, :]
                log_a = (-LRU_C) * jax.nn.sigmoid(ga) * sp[d]
                a = jnp.exp(log_a)
                bco = jnp.sqrt(1.0 - a * a) * jax.nn.sigmoid(gi) * u
                a, bco = _group_scan(a, bco, reverse=(d == 1))
                a_ref[d, pl.ds(r0, ch), :] = a
                b_ref[d, pl.ds(r0, ch), :] = bco
            return carry

        lax.fori_loop(0, t // ch, body, 0)

    def carry_pass(a_ref, b_ref, cf, cr):
        n = a_ref.shape[1] // SUBLANES

        def body(j, carry):
            cf, cr = carry
            rf = pl.multiple_of(j * SUBLANES, SUBLANES)
            rr = pl.multiple_of((n - 1 - j) * SUBLANES, SUBLANES)
            hf = a_ref[0, pl.ds(rf, SUBLANES), :] * cf + b_ref[0, pl.ds(rf, SUBLANES), :]
            hr = a_ref[1, pl.ds(rr, SUBLANES), :] * cr + b_ref[1, pl.ds(rr, SUBLANES), :]
            b_ref[0, pl.ds(rf, SUBLANES), :] = hf
            b_ref[1, pl.ds(rr, SUBLANES), :] = hr
            return (jnp.broadcast_to(hf[SUBLANES - 1:SUBLANES, :], hf.shape),
                    jnp.broadcast_to(hr[0:1, :], hr.shape))

        return lax.fori_loop(0, n, body, (cf, cr))

    def emit(g_ref, b_ref, o_ref):
        t = g_ref.shape[0]
        ch = min(chunk, t)

        def body(c, carry):
            r0 = pl.multiple_of(c * ch, SUBLANES)
            y = b_ref[0, pl.ds(r0, ch), :] + b_ref[1, pl.ds(r0, ch), :]
            o_ref[pl.ds(r0, ch), :] = (jax.nn.gelu(g_ref[pl.ds(r0, ch), :]) * y).astype(o_ref.dtype)
            return carry

        lax.fori_loop(0, t // ch, body, 0)

    coeffs(xc_ref, ac_ref, bc_ref)
    coeffs(xl_ref, al_ref, bl_ref)
    cf, cr = carry_pass(ac_ref, bc_ref, zero8, zero8)
    carry_pass(al_ref, bl_ref, cf, cr)
    emit(gl_ref, bl_ref, ol_ref)
    emit(gc_ref, bc_ref, oc_ref)


def _lru_call(xr_l, gr_l, xr_c, gr_c, conv_w, conv_b, wg, bg, lam, batch):
    s = xr_l.shape[0] // batch
    l = xr_c.shape[0] // batch
    lw = xr_l.shape[1]
    half = wg.shape[-1]
    nh = lw // half
    lat = pl.BlockSpec((s, half), lambda b, c: (b, c))
    ctx = pl.BlockSpec((l, half), lambda b, c: (b, c))
    return pl.pallas_call(
        functools.partial(_lru_kernel, chunk=256),
        grid=(batch, nh),
        in_specs=[lat, lat, ctx, ctx,
                  pl.BlockSpec((CONV_W, half), lambda b, c: (0, c)),
                  pl.BlockSpec((1, half), lambda b, c: (0, c)),
                  pl.BlockSpec((2, 2, None, half, half), lambda b, c: (0, 0, c, 0, 0)),
                  pl.BlockSpec((2, 2, half), lambda b, c: (0, 0, c)),
                  pl.BlockSpec((2, half), lambda b, c: (0, c))],
        out_specs=[lat, ctx],
        out_shape=[jax.ShapeDtypeStruct(xr_l.shape, BF16), jax.ShapeDtypeStruct(xr_c.shape, BF16)],
        scratch_shapes=[pltpu.VMEM((s + 2 * SUBLANES, half), F32),
                        pltpu.VMEM((2, s, half), F32), pltpu.VMEM((2, s, half), F32),
                        pltpu.VMEM((2, l, half), F32), pltpu.VMEM((2, l, half), F32)],
        compiler_params=_params(("arbitrary", "arbitrary"), 48),
        name="rglru",
    )(xr_l, gr_l, xr_c, gr_c, conv_w, conv_b.reshape(1, lw), wg, bg, lam)


def _lru_gate_weights(wa, ba, wx, bx, half):
    nblk = wa.shape[1]
    lw = nblk * LRU_BLOCK
    per = half // LRU_BLOCK

    def dense(w):
        w = w.reshape(2, nblk // per, per, LRU_BLOCK, LRU_BLOCK)
        eye = jnp.eye(per, dtype=w.dtype)
        full = w[:, :, :, :, None, :] * eye[None, None, :, None, :, None]
        return full.reshape(2, nblk // per, half, half)

    wg = jnp.stack([dense(wa), dense(wx)], axis=1).astype(BF16)
    bg = jnp.stack([ba, bx], axis=1).astype(F32)
    return wg, bg


def _out_proj_kernel(oa_ref, ol_ref, h_ref, g1_ref, w_ref, o_ref, *, aw):
    o = jnp.dot(oa_ref[...], w_ref[0:aw, :], preferred_element_type=F32)
    o = o + jnp.dot(ol_ref[...], w_ref[aw:, :], preferred_element_type=F32)
    o_ref[...] = h_ref[...] + g1_ref[...] * o


def _out_proj_call(oa, ol, h, mods, layer, w_bf, rows_per_mod, fixed_row):
    r, d = h.shape
    aw = oa.shape[1]
    lw = ol.shape[1]
    tm = min(ROW_TILE, r)
    ms = functools.partial(_mod_spec, d, layer=layer, rows_per_mod=rows_per_mod, tm=tm, fixed_row=fixed_row)
    row = lambda w: pl.BlockSpec((tm, w), lambda i: (i, 0))
    return pl.pallas_call(
        functools.partial(_out_proj_kernel, aw=aw),
        grid=(r // tm,),
        in_specs=[row(aw), row(lw), row(d), ms(col=2), pl.BlockSpec((aw + lw, d), lambda i: (0, 0))],
        out_specs=row(d),
        out_shape=jax.ShapeDtypeStruct((r, d), F32),
        compiler_params=_params(("arbitrary",), 32),
        name="out_proj",
    )(oa, ol, h, mods, w_bf)


def _swiglu_step(a_bf, w1_ref, w3_ref, w2_ref, acc_ref):
    g = jnp.dot(a_bf, w1_ref[...].astype(BF16), preferred_element_type=F32)
    u = jnp.dot(a_bf, w3_ref[...].astype(BF16), preferred_element_type=F32)
    hmid = (g * jax.nn.sigmoid(g) * u).astype(BF16)
    acc_ref[...] += jnp.dot(hmid, w2_ref[...].astype(BF16), preferred_element_type=F32)


def _ffn_kernel(x_ref, g_ref, sh_ref, sc_ref, g2_ref, w1_ref, w3_ref, w2_ref, o_ref, a_ref, acc_ref):
    c = pl.program_id(1)

    @pl.when(c == 0)
    def _():
        a_ref[...] = _rms_mod(x_ref[...], g_ref[...], sh_ref[...], sc_ref[...]).astype(BF16)
        acc_ref[...] = jnp.zeros_like(acc_ref)

    _swiglu_step(a_ref[...], w1_ref, w3_ref, w2_ref, acc_ref)

    @pl.when(c == pl.num_programs(1) - 1)
    def _():
        o_ref[...] = x_ref[...] + g2_ref[...] * acc_ref[...]


def _ffn_call(h, g, mods, layer, w1, w3, w2, rows_per_mod, fixed_row):
    r, d = h.shape
    dff = w1.shape[1]
    tm = min(ROW_TILE_FFN, r)
    ms = functools.partial(_mod_spec, d, layer=layer, rows_per_mod=rows_per_mod, tm=tm, fixed_row=fixed_row)
    return pl.pallas_call(
        _ffn_kernel,
        grid=(r // tm, dff // FF_CHUNK),
        in_specs=[pl.BlockSpec((tm, d), lambda i, c: (i, 0)),
                  pl.BlockSpec((1, d), lambda i, c: (0, 0)),
                  ms(col=3), ms(col=4), ms(col=5),
                  pl.BlockSpec((d, FF_CHUNK), lambda i, c: (0, c)),
                  pl.BlockSpec((d, FF_CHUNK), lambda i, c: (0, c)),
                  pl.BlockSpec((FF_CHUNK, d), lambda i, c: (c, 0))],
        out_specs=pl.BlockSpec((tm, d), lambda i, c: (i, 0)),
        out_shape=jax.ShapeDtypeStruct((r, d), F32),
        scratch_shapes=[pltpu.VMEM((tm, d), BF16), pltpu.VMEM((tm, d), F32)],
        compiler_params=_params(("arbitrary", "arbitrary"), 56),
        name="ffn_dense",
    )(h, g, mods, mods, mods, w1, w3, w2)


def _to_row_tiles(x, o_ref):
    t, d = x.shape
    ns = d // LANES
    for s in range(ns):
        o_ref[pl.ds(s, t, stride=ns), :] = x[:, s * LANES:(s + 1) * LANES]


def _from_row_tiles(x_ref, idx, t, ns):
    return jnp.concatenate([x_ref[idx + (pl.ds(s, t, stride=ns), slice(None))] for s in range(ns)], axis=1)


def _gffn_kernel(te_ref, nu_ref, x_ref, w1_ref, w3_ref, w2_ref, o_ref, a_ref, acc_ref):
    i = pl.program_id(0)
    c = pl.program_id(1)
    tm, d = a_ref.shape

    @pl.when(i < nu_ref[0])
    def _():
        @pl.when(c == 0)
        def _():
            a_ref[...] = _from_row_tiles(x_ref, (), tm, d // LANES).astype(BF16)
            acc_ref[...] = jnp.zeros_like(acc_ref)

        _swiglu_step(a_ref[...], w1_ref, w3_ref, w2_ref, acc_ref)

        @pl.when(c == pl.num_programs(1) - 1)
        def _():
            _to_row_tiles(acc_ref[...], o_ref)

    @pl.when((i >= nu_ref[0]) & (c == pl.num_programs(1) - 1))
    def _():
        o_ref[...] = jnp.zeros_like(o_ref)


def _gffn_call(tile_expert, n_used, xg, w1, w3, w2):
    d = w1.shape[1]
    ns = d // LANES
    rg = xg.shape[0] // ns
    dff = w1.shape[2]
    tm = ROW_TILE_FFN
    nc = dff // FF_CHUNK

    def tile(i, nu):
        return jnp.minimum(i, nu[0] - 1)

    def chunk(i, c, nu):
        return jnp.where(i < nu[0], c, nc - 1)

    grid_spec = pltpu.PrefetchScalarGridSpec(
        num_scalar_prefetch=2,
        grid=(rg // tm, nc),
        in_specs=[pl.BlockSpec((tm * ns, LANES), lambda i, c, te, nu: (tile(i, nu), 0)),
                  pl.BlockSpec((None, d, FF_CHUNK), lambda i, c, te, nu: (te[tile(i, nu)], 0, chunk(i, c, nu))),
                  pl.BlockSpec((None, d, FF_CHUNK), lambda i, c, te, nu: (te[tile(i, nu)], 0, chunk(i, c, nu))),
                  pl.BlockSpec((None, FF_CHUNK, d), lambda i, c, te, nu: (te[tile(i, nu)], chunk(i, c, nu), 0))],
        out_specs=pl.BlockSpec((tm * ns, LANES), lambda i, c, te, nu: (i, 0)),
        scratch_shapes=[pltpu.VMEM((tm, d), BF16), pltpu.VMEM((tm, d), F32)])
    return pl.pallas_call(
        _gffn_kernel,
        grid_spec=grid_spec,
        out_shape=jax.ShapeDtypeStruct((rg * ns, LANES), F32),
        compiler_params=_params(("arbitrary", "arbitrary"), 56),
        name="ffn_grouped",
    )(tile_expert, n_used, xg, w1, w3, w2)


def _route_kernel(x_ref, g_ref, sh_ref, sc_ref, wr_ref, br_ref, a_ref, tab_ref, gcol_ref, cnt_ref, carry_ref):
    i = pl.program_id(0)
    tm = x_ref.shape[0]

    @pl.when(i == 0)
    def _():
        carry_ref[...] = jnp.zeros_like(carry_ref)

    a = _rms_mod(x_ref[...], g_ref[...], sh_ref[...], sc_ref[...])
    _to_row_tiles(a, a_ref)
    lane = lax.broadcasted_iota(I32, (tm, LANES), 1).astype(F32)
    logits = jnp.dot(a, wr_ref[...], preferred_element_type=F32, precision=lax.Precision.HIGHEST) + br_ref[...]
    logits = jnp.where(lane < N_EXPERTS, logits, NEG_INF)
    m1 = jnp.max(logits, axis=-1, keepdims=True)
    i1 = jnp.min(jnp.where(logits == m1, lane, float(LANES)), axis=-1, keepdims=True)
    rest = jnp.where(lane == i1, 2.0 * NEG_INF, logits)
    m2 = jnp.max(rest, axis=-1, keepdims=True)
    i2 = jnp.min(jnp.where(rest == m2, lane, float(LANES)), axis=-1, keepdims=True)
    e = jnp.exp(m2 - m1)
    w1 = 1.0 / (1.0 + e)
    w2 = e * w1
    sel1 = lane == i1
    sel2 = lane == i2
    onehot = jnp.where(sel1 | sel2, 1.0, 0.0)
    row = lax.broadcasted_iota(I32, (tm, tm), 0)
    col = lax.broadcasted_iota(I32, (tm, tm), 1)
    before = jnp.where(col < row, 1.0, 0.0).astype(BF16)
    cum = jnp.dot(before, onehot.astype(BF16), preferred_element_type=F32) + carry_ref[...]
    r1 = jnp.sum(jnp.where(sel1, cum, 0.0), axis=-1, keepdims=True)
    r2 = jnp.sum(jnp.where(sel2, cum, 0.0), axis=-1, keepdims=True)
    carry_ref[...] += jnp.sum(onehot, axis=0, keepdims=True)
    cnt_ref[...] = carry_ref[...]
    cols = (jnp.where(lane == 0, i1, 0.0) + jnp.where(lane == 1, i2, 0.0)
            + jnp.where(lane == 2, r1, 0.0) + jnp.where(lane == 3, r2, 0.0))
    tab_ref[...] = cols.T[0:SUBLANES, :]
    gcol_ref[...] = (jnp.where(lane == 0, w1, 0.0) + jnp.where(lane == 1, w2, 0.0))[:, 0:SUBLANES]


def _route_call(h, g, mods, layer, wr, br, rows_per_mod):
    n, d = h.shape
    tm = min(ROW_TILE, n)
    ms = functools.partial(_mod_spec, d, layer=layer, rows_per_mod=rows_per_mod, tm=tm, fixed_row=None)
    return pl.pallas_call(
        _route_kernel,
        grid=(n // tm,),
        in_specs=[pl.BlockSpec((tm, d), lambda i: (i, 0)),
                  pl.BlockSpec((1, d), lambda i: (0, 0)), ms(col=3), ms(col=4),
                  pl.BlockSpec((d, LANES), lambda i: (0, 0)),
                  pl.BlockSpec((1, LANES), lambda i: (0, 0))],
        out_specs=[pl.BlockSpec((tm * (d // LANES), LANES), lambda i: (i, 0)),
                   pl.BlockSpec((SUBLANES, tm), lambda i: (0, i)),
                   pl.BlockSpec((tm, SUBLANES), lambda i: (i, 0)),
                   pl.BlockSpec((1, LANES), lambda i: (0, 0))],
        out_shape=[jax.ShapeDtypeStruct((n * (d // LANES), LANES), F32),
                   jax.ShapeDtypeStruct((SUBLANES, n), F32),
                   jax.ShapeDtypeStruct((n, SUBLANES), F32),
                   jax.ShapeDtypeStruct((1, LANES), F32)],
        scratch_shapes=[pltpu.VMEM((1, LANES), F32)],
        compiler_params=_params(("arbitrary",), 40),
        name="moe_route",
    )(h, g, mods, mods, wr, br)


DMA_UNROLL = 4


def _dispatch_kernel(off_ref, tab_ref, a_ref, xg_in, xg_out, pos_ref, sem, *, ns):
    del xg_in
    tm = tab_ref.shape[1]

    def row_copy(src, dst):
        return pltpu.make_async_copy(a_ref.at[pl.ds(pl.multiple_of(src * ns, ns), ns)],
                                     xg_out.at[pl.ds(pl.multiple_of(dst * ns, ns), ns)], sem)

    def issue(j, carry):
        for u in range(DMA_UNROLL):
            t = j * DMA_UNROLL + u
            for k in range(2):
                p = off_ref[tab_ref[k, t]] + tab_ref[2 + k, t]
                pos_ref[k, t] = p
                row_copy(t, p).start()
        return carry

    lax.fori_loop(0, tm // DMA_UNROLL, issue, 0)

    def drain(j, carry):
        for _ in range(2 * DMA_UNROLL):
            row_copy(0, 0).wait()
        return carry

    lax.fori_loop(0, tm // DMA_UNROLL, drain, 0)


def _dispatch_call(off, tab_i, a, rows_g, ns):
    n = a.shape[0] // ns
    tm = min(ROW_TILE, n)
    grid_spec = pltpu.PrefetchScalarGridSpec(
        num_scalar_prefetch=1,
        grid=(n // tm,),
        in_specs=[pl.BlockSpec((4, tm), lambda i, off: (0, i), memory_space=pltpu.SMEM),
                  pl.BlockSpec((tm * ns, LANES), lambda i, off: (i, 0)),
                  pl.BlockSpec(memory_space=pl.ANY)],
        out_specs=[pl.BlockSpec(memory_space=pl.ANY),
                   pl.BlockSpec((2, tm), lambda i, off: (0, i), memory_space=pltpu.SMEM)],
        scratch_shapes=[pltpu.SemaphoreType.DMA(())])
    xg0 = jnp.zeros((rows_g * ns, LANES), a.dtype)
    return pl.pallas_call(
        functools.partial(_dispatch_kernel, ns=ns),
        grid_spec=grid_spec,
        out_shape=[jax.ShapeDtypeStruct((rows_g * ns, LANES), a.dtype), jax.ShapeDtypeStruct((2, n), I32)],
        input_output_aliases={3: 0},
        compiler_params=_params(("arbitrary",), 16),
        name="moe_dispatch",
    )(off, tab_i, a, xg0)


def _combine_kernel(pos_ref, posn_ref, y_hbm, h_ref, gcol_ref, g2_ref, fg_ref, o_ref, buf_ref, sem):
    i = pl.program_id(0)
    n = pl.num_programs(0)
    tm, d = h_ref.shape
    ns = d // LANES

    def row_copy(src, slot, k, t):
        return pltpu.make_async_copy(y_hbm.at[pl.ds(pl.multiple_of(src * ns, ns), ns)],
                                     buf_ref.at[slot, k, pl.ds(pl.multiple_of(t * ns, ns), ns)], sem.at[slot])

    def issue(p_ref, slot):
        def body(j, carry):
            for u in range(DMA_UNROLL):
                t = j * DMA_UNROLL + u
                for k in range(2):
                    row_copy(p_ref[k, t], slot, k, t).start()
            return carry
        lax.fori_loop(0, tm // DMA_UNROLL, body, 0)

    @pl.when(i == 0)
    def _():
        issue(pos_ref, 0)

    @pl.when(i + 1 < n)
    def _():
        issue(posn_ref, (i + 1) % 2)

    slot = i % 2

    def drain(j, carry):
        for _ in range(2 * DMA_UNROLL):
            row_copy(0, slot, 0, 0).wait()
        return carry

    lax.fori_loop(0, tm // DMA_UNROLL, drain, 0)
    y = (gcol_ref[:, 0:1] * _from_row_tiles(buf_ref, (slot, 0), tm, ns)
         + gcol_ref[:, 1:2] * _from_row_tiles(buf_ref, (slot, 1), tm, ns))
    hn = h_ref[...] + g2_ref[...] * y
    o_ref[...] = hn * lax.rsqrt(jnp.mean(hn * hn, axis=-1, keepdims=True) + EPS) * fg_ref[...]


def _combine_call(pos, y, h, gcol, mods, layer, final_g, rows_per_mod):
    n, d = h.shape
    tm = min(GATHER_TILE, n)
    nt = n // tm
    ms = functools.partial(_mod_spec, d, layer=layer, rows_per_mod=rows_per_mod, tm=tm, fixed_row=None)
    return pl.pallas_call(
        _combine_kernel,
        grid=(nt,),
        in_specs=[pl.BlockSpec((2, tm), lambda i: (0, i), memory_space=pltpu.SMEM),
                  pl.BlockSpec((2, tm), lambda i: (0, jnp.minimum(i + 1, nt - 1)), memory_space=pltpu.SMEM),
                  pl.BlockSpec(memory_space=pl.ANY),
                  pl.BlockSpec((tm, d), lambda i: (i, 0)),
                  pl.BlockSpec((tm, SUBLANES), lambda i: (i, 0)),
                  ms(col=5),
                  pl.BlockSpec((1, d), lambda i: (0, 0))],
        out_specs=pl.BlockSpec((tm, d), lambda i: (i, 0)),
        out_shape=jax.ShapeDtypeStruct((n, d), F32),
        scratch_shapes=[pltpu.VMEM((2, 2, tm * (d // LANES), LANES), F32), pltpu.SemaphoreType.DMA((2,))],
        compiler_params=_params(("arbitrary",), 24),
        name="moe_combine",
    )(pos, pos, y, h, gcol, mods, final_g)


def _moe_layer(h, g, mods, layer, router, router_b, w1, w3, w2, final_g, rows_per_mod):
    n, d = h.shape
    wr = jnp.zeros((d, LANES), F32).at[:, :N_EXPERTS].set(router)
    br = jnp.zeros((1, LANES), F32).at[0, :N_EXPERTS].set(router_b)
    a, tab, gcol, cnt = _route_call(h, g, mods, layer, wr, br, rows_per_mod)
    tm = ROW_TILE_FFN
    n_tiles = (2 * n) // tm + N_EXPERTS
    counts = cnt[0, :N_EXPERTS].astype(I32)
    tiles = (counts + tm - 1) // tm
    ends = jnp.cumsum(tiles)
    off = ((ends - tiles) * tm).astype(I32)
    tile_expert = jnp.minimum(jnp.sum(jnp.arange(n_tiles, dtype=I32)[:, None] >= ends[None, :], axis=1),
                              N_EXPERTS - 1).astype(I32)
    n_used = ends[-1:].astype(I32)
    assert d // LANES == SUBLANES, "one token must fill one (8, 128) tile of the row-tile layout"
    xg, pos = _dispatch_call(off, tab[0:4].astype(I32), a, n_tiles * tm, d // LANES)
    y = _gffn_call(tile_expert, n_used, xg, w1, w3, w2)
    return _combine_call(pos, y, h, gcol, mods, layer, final_g, rows_per_mod)


def kernel(x, c, ctx, c_ctx, ada_w, ada_b, mix_norm_g, ffn_norm_g, w_in, w_out, na_rpb, conv_w, conv_b, lru_wa, lru_ba, lru_wx, lru_bx, lru_lam, ffn_w1, ffn_w3, ffn_w2, moe_router, moe_router_b, moe_w1, moe_w3, moe_w2, final_g):
    batch, seq, d = x.shape
    ctx_len = ctx.shape[1]
    depth = ada_w.shape[0]
    lw = conv_w.shape[-1]
    aw = (w_in.shape[-1] - 2 * lw) // 3
    rows = seq // GRID_W
    ctx_row = batch

    mod_rows = -(-(batch + 1) // SUBLANES) * SUBLANES
    c_all = jnp.zeros((mod_rows, d), F32).at[:batch].set(c).at[ctx_row].set(c_ctx)
    mods = _ada_call(c_all, ada_w, ada_b).reshape(depth, mod_rows, 1, 6 * d)

    h_lat = x.reshape(batch * seq, d)
    h_ctx = ctx.reshape(batch * ctx_len, d)
    half = min(lw, 2 * LANES)
    out = None
    for i in range(depth):
        last = i == depth - 1
        w_in_bf = w_in[i].astype(BF16)
        w_out_bf = w_out[i].astype(BF16)
        g_mix = mix_norm_g[i].reshape(1, d)
        g_ffn = ffn_norm_g[i].reshape(1, d)
        q_l, k_l, v_l, xr_l, gr_l = _in_proj_call(h_lat, g_mix, mods, i, w_in_bf, seq, None, aw, lw)
        q_c, k_c, v_c, xr_c, gr_c = _in_proj_call(h_ctx, g_mix, mods, i, w_in_bf, None, ctx_row, aw, lw)
        oa_l = _na_call(q_l, k_l, v_l, k_c, v_c, _pad_rpb(na_rpb[i]), batch)
        wg, bg = _lru_gate_weights(lru_wa[i], lru_ba[i], lru_wx[i], lru_bx[i], half)
        ol_l, ol_c = _lru_call(xr_l, gr_l, xr_c, gr_c, conv_w[i], conv_b[i], wg, bg, lru_lam[i], batch)
        h_lat = _out_proj_call(oa_l, ol_l, h_lat, mods, i, w_out_bf, seq, None)
        if not last:
            oa_c = _ctx_attn_call(q_c, k_c, v_c, batch)
            h_ctx = _out_proj_call(oa_c, ol_c, h_ctx, mods, i, w_out_bf, None, ctx_row)
        j = i // 2
        if i % 2 == 0:
            h_lat = _ffn_call(h_lat, g_ffn, mods, i, ffn_w1[j], ffn_w3[j], ffn_w2[j], seq, None)
            if not last:
                h_ctx = _ffn_call(h_ctx, g_ffn, mods, i, ffn_w1[j], ffn_w3[j], ffn_w2[j], None, ctx_row)
        else:
            assert last, "the routed layer is fused with the final norm"
            out = _moe_layer(h_lat, g_ffn, mods, i, moe_router[j], moe_router_b[j],
                             moe_w1[j], moe_w3[j], moe_w2[j], final_g.reshape(1, d), seq)
    return out.reshape(batch, seq, d)
```

```python
import functools

import jax
import jax.numpy as jnp
from jax import lax
from jax.experimental import pallas as pl
from jax.experimental.pallas import tpu as pltpu

F32 = jnp.float32
BF16 = jnp.bfloat16
I32 = jnp.int32

GRID_W = 64
HEAD_DIM = 64
N_HEADS = 8
NA_KH = 8
NA_KW = 16
LRU_BLOCK = 64
LRU_C = 8.0
CONV_W = 4
N_EXPERTS = 8
EPS = 1e-6
NEG_INF = -1e30
LANES = 128
SUBLANES = 8
VMEM_BYTES = 64 * 1024 * 1024

FF_CHUNK = 512
ROW_TILE_FFN = 1024
ROW_TILE = 512
GATHER_TILE = 256


def _params(semantics, vmem_mb):
    return pltpu.CompilerParams(dimension_semantics=semantics,
                                vmem_limit_bytes=min(vmem_mb * 1024 * 1024, VMEM_BYTES - 4 * 1024 * 1024))


def _rms_mod(x, g, sh, sc):
    y = x * lax.rsqrt(jnp.mean(x * x, axis=-1, keepdims=True) + EPS)
    return (y * g) * (1.0 + sc) + sh


def _ada_kernel(c_ref, w_ref, b_ref, o_ref):
    c = c_ref[...]
    s = c * jax.nn.sigmoid(c)
    o_ref[...] = jnp.dot(s, w_ref[...], preferred_element_type=F32,
                         precision=lax.Precision.HIGHEST) + b_ref[...]


def _ada_call(c_all, ada_w, ada_b):
    depth, d, n = ada_w.shape
    rows = c_all.shape[0]
    tn = 1024
    return pl.pallas_call(
        _ada_kernel,
        grid=(depth, n // tn),
        in_specs=[pl.BlockSpec((rows, d), lambda l, j: (0, 0)),
                  pl.BlockSpec((None, d, tn), lambda l, j: (l, 0, j)),
                  pl.BlockSpec((None, 1, tn), lambda l, j: (l, 0, j))],
        out_specs=pl.BlockSpec((None, rows, tn), lambda l, j: (l, 0, j)),
        out_shape=jax.ShapeDtypeStruct((depth, rows, n), F32),
        compiler_params=_params(("arbitrary", "arbitrary"), 24),
        name="ada_mod",
    )(c_all, ada_w, ada_b.reshape(depth, 1, n))


TIME_TILE = 64


def _to_time_major(r, o_ref, nb, tt):
    for c in range(o_ref.shape[0]):
        for b in range(nb):
            o_ref[c, pl.ds(b, tt, stride=SUBLANES), :] = r[b * tt:(b + 1) * tt, c * LANES:(c + 1) * LANES]


def _from_time_major(x_ref, nb, tt):
    return jnp.concatenate(
        [jnp.concatenate([x_ref[c, pl.ds(b, tt, stride=SUBLANES), :] for c in range(x_ref.shape[0])], axis=1)
         for b in range(nb)], axis=0)


def _in_proj_kernel(x_ref, g_ref, sh_ref, sc_ref, w_ref, q_ref, k_ref, v_ref, xr_ref, gr_ref, *, aw, lw):
    nb, tt, d = x_ref.shape
    a = _rms_mod(x_ref[...], g_ref[...], sh_ref[...], sc_ref[...]).astype(BF16).reshape(nb * tt, d)
    r = jnp.dot(a, w_ref[...], preferred_element_type=F32)
    q_ref[...] = (r[:, 0:aw] * (HEAD_DIM ** -0.5)).astype(BF16).reshape(nb, tt, aw)
    k_ref[...] = r[:, aw:2 * aw].astype(BF16).reshape(nb, tt, aw)
    v_ref[...] = r[:, 2 * aw:3 * aw].astype(BF16).reshape(nb, tt, aw)
    _to_time_major(r[:, 3 * aw:3 * aw + lw], xr_ref, nb, tt)
    _to_time_major(r[:, 3 * aw + lw:3 * aw + 2 * lw], gr_ref, nb, tt)


def _mod_spec(d, col, layer, rows_per_mod, tm, fixed_row):
    if fixed_row is None:
        assert rows_per_mod % tm == 0, "a row tile must not straddle two batch elements"
        return pl.BlockSpec((None, None, 1, d), lambda i, *_: (layer, (i * tm) // rows_per_mod, 0, col))
    return pl.BlockSpec((None, None, 1, d), lambda i, *_: (layer, fixed_row, 0, col))


def _batch_mod_spec(nb, d, layer, col):
    return pl.BlockSpec((None, nb, 1, d), lambda j: (layer, 0, 0, col))


def _in_proj_call(x, g, bmods, layer, w_bf, aw, lw):
    nb, t, d = x.shape
    assert nb == SUBLANES, "time-major rows put the batch on the sublanes"
    tt = min(TIME_TILE, t)
    ncol = w_bf.shape[1]
    ncb = lw // LANES
    bt = lambda w: pl.BlockSpec((nb, tt, w), lambda j: (0, j, 0))
    tm = pl.BlockSpec((ncb, tt * SUBLANES, LANES), lambda j: (0, j, 0))
    return pl.pallas_call(
        functools.partial(_in_proj_kernel, aw=aw, lw=lw),
        grid=(t // tt,),
        in_specs=[bt(d), pl.BlockSpec((1, d), lambda j: (0, 0)),
                  _batch_mod_spec(nb, d, layer, 0), _batch_mod_spec(nb, d, layer, 1),
                  pl.BlockSpec((d, ncol), lambda j: (0, 0))],
        out_specs=[bt(aw), bt(aw), bt(aw), tm, tm],
        out_shape=[jax.ShapeDtypeStruct((nb, t, aw), BF16)] * 3
        + [jax.ShapeDtypeStruct((ncb, t * SUBLANES, LANES), F32)] * 2,
        compiler_params=_params(("arbitrary",), 40),
        name="in_proj",
    )(x, g, bmods, bmods, w_bf)


def _softmax_pv(parts):
    m = None
    for s, _ in parts:
        mi = jnp.max(s, axis=-1, keepdims=True)
        m = mi if m is None else jnp.maximum(m, mi)
    l = None
    acc = None
    for s, v in parts:
        p = jnp.exp(s - m)
        li = jnp.sum(p, axis=-1, keepdims=True)
        ai = jnp.dot(p.astype(BF16), v, preferred_element_type=F32)
        l = li if l is None else l + li
        acc = ai if acc is None else acc + ai
    return acc * (1.0 / l)


def _qk(q, k):
    return lax.dot_general(q, k, (((1,), (1,)), ((), ())), preferred_element_type=F32)


def _build_na_bias(rpb_ref, bias_ref):
    lane = lax.broadcasted_iota(I32, (GRID_W, LANES), 1)
    q = lax.broadcasted_iota(I32, (GRID_W, LANES), 0)
    kcol = lane & (GRID_W - 1)
    cs = jnp.clip(q - NA_KW // 2, 0, GRID_W - NA_KW)
    ok = (kcol >= cs) & (kcol < cs + NA_KW)
    low = lane < GRID_W
    for delta in range(NA_KH):
        for h in range(N_HEADS):
            for jp in range(NA_KH // 2):
                halves = []
                for j in (2 * jp, 2 * jp + 1):
                    dr = j - delta + NA_KH - 1
                    w = jnp.broadcast_to(rpb_ref[h, dr:dr + 1, :], (GRID_W, LANES))
                    base = (j % 2) * GRID_W - (NA_KW - 1)
                    halves.append(pltpu.roll(w, base % LANES, 1, stride=1, stride_axis=0))
                t = jnp.where(low, halves[0], halves[1])
                bias_ref[delta, h, :, jp * LANES:(jp + 1) * LANES] = jnp.where(ok, t, NEG_INF)


def _na_kernel(q_ref, kl_ref, vl_ref, kc_ref, vc_ref, rpb_ref, o_ref, bias_ref, s_ref, p_ref, *, rows):
    @pl.when((pl.program_id(0) == 0) & (pl.program_id(1) == 0))
    def _():
        _build_na_bias(rpb_ref, bias_ref)

    r = pl.program_id(1)
    rs = jnp.clip(r - NA_KH // 2, 0, rows - NA_KH)
    delta = r - rs
    k0 = pl.multiple_of(rs * GRID_W, GRID_W)
    nwin = NA_KH * GRID_W
    lane = lax.broadcasted_iota(I32, (GRID_W, 2 * HEAD_DIM), 1)
    first = lane < HEAD_DIM
    for pair in range(N_HEADS // 2):
        sl = slice(pair * 2 * HEAD_DIM, (pair + 1) * 2 * HEAD_DIM)
        q2 = q_ref[:, sl]
        keys = jnp.concatenate([kl_ref[pl.ds(k0, nwin), sl], kc_ref[:, sl]], axis=0)
        for sub in range(2):
            h = pair * 2 + sub
            qh = jnp.where(first if sub == 0 else ~first, q2, jnp.zeros_like(q2))
            s = _qk(qh, keys)
            s_ref[h, :, 0:nwin] = s[:, 0:nwin] + bias_ref[delta, h]
            s_ref[h, :, nwin:] = s[:, nwin:]
    s = s_ref[...]
    p = jnp.exp(s - jnp.max(s, axis=-1, keepdims=True))
    inv = 1.0 / jnp.sum(p, axis=-1, keepdims=True)
    p_ref[...] = p.astype(BF16)
    for pair in range(N_HEADS // 2):
        sl = slice(pair * 2 * HEAD_DIM, (pair + 1) * 2 * HEAD_DIM)
        vals = jnp.concatenate([vl_ref[pl.ds(k0, nwin), sl], vc_ref[:, sl]], axis=0)
        outs = [jnp.dot(p_ref[pair * 2 + sub], vals, preferred_element_type=F32) * inv[pair * 2 + sub]
                for sub in range(2)]
        o_ref[:, sl] = jnp.where(first, outs[0], outs[1]).astype(o_ref.dtype)


def _na_call(q, kl, vl, kc, vc, rpb, batch):
    s = q.shape[0] // batch
    l = kc.shape[0] // batch
    w = q.shape[1]
    rows = s // GRID_W
    assert rows >= NA_KH
    nkeys = NA_KH * GRID_W + l
    return pl.pallas_call(
        functools.partial(_na_kernel, rows=rows),
        grid=(batch, rows),
        in_specs=[pl.BlockSpec((GRID_W, w), lambda b, r: (b * rows + r, 0)),
                  pl.BlockSpec((s, w), lambda b, r: (b, 0)),
                  pl.BlockSpec((s, w), lambda b, r: (b, 0)),
                  pl.BlockSpec((l, w), lambda b, r: (b, 0)),
                  pl.BlockSpec((l, w), lambda b, r: (b, 0)),
                  pl.BlockSpec(rpb.shape, lambda b, r: (0, 0, 0))],
        out_specs=pl.BlockSpec((GRID_W, w), lambda b, r: (b * rows + r, 0)),
        out_shape=jax.ShapeDtypeStruct(q.shape, BF16),
        scratch_shapes=[pltpu.VMEM((NA_KH, N_HEADS, GRID_W, NA_KH * GRID_W), F32),
                        pltpu.VMEM((N_HEADS, GRID_W, nkeys), F32),
                        pltpu.VMEM((N_HEADS, GRID_W, nkeys), BF16)],
        compiler_params=_params(("arbitrary", "arbitrary"), 48),
        name="na_attention",
    )(q, kl, vl, kc, vc, rpb)


def _pad_rpb(rpb):
    h, nr, nc = rpb.shape
    return jnp.zeros((h, 2 * NA_KH, LANES), F32).at[:, :nr, :nc].set(rpb.astype(F32))


def _ctx_attn_kernel(q_ref, k_ref, v_ref, o_ref):
    rows = q_ref.shape[0]
    lane = lax.broadcasted_iota(I32, (rows, 2 * HEAD_DIM), 1)
    first = lane < HEAD_DIM
    for pair in range(N_HEADS // 2):
        sl = slice(pair * 2 * HEAD_DIM, (pair + 1) * 2 * HEAD_DIM)
        q2 = q_ref[:, sl]
        k = k_ref[:, sl]
        v = v_ref[:, sl]
        outs = []
        for sub in range(2):
            qh = jnp.where(first if sub == 0 else ~first, q2, jnp.zeros_like(q2))
            outs.append(_softmax_pv([(_qk(qh, k), v)]))
        o_ref[:, sl] = jnp.where(first, outs[0], outs[1]).astype(o_ref.dtype)


def _ctx_attn_call(q, k, v, batch):
    l = q.shape[0] // batch
    w = q.shape[1]
    spec = pl.BlockSpec((l, w), lambda b: (b, 0))
    return pl.pallas_call(
        _ctx_attn_kernel,
        grid=(batch,),
        in_specs=[spec, spec, spec],
        out_specs=spec,
        out_shape=jax.ShapeDtypeStruct(q.shape, BF16),
        compiler_params=_params(("arbitrary",), 24),
        name="ctx_attention",
    )(q, k, v)


def _softplus(x):
    return jnp.maximum(x, 0.0) + jnp.log1p(jnp.exp(-jnp.abs(x)))


LRU_TIME_CHUNK = 64


def _lru_kernel(xl_hbm, xc_hbm, gl_ref, gc_ref, cw_ref, cb_ref, wg_ref, bg_ref, lam_ref, ol_ref, oc_ref,
                padl_ref, padc_ref, sem, *, tchunk):
    c = pl.program_id(0)
    nb = SUBLANES
    halo_lo, halo_hi = nb, 2 * nb
    zero = jnp.zeros((nb, LANES), F32)

    def fetch(x_hbm, pad_ref, k):
        n = x_hbm.shape[1]
        return pltpu.make_async_copy(x_hbm.at[c], pad_ref.at[pl.ds(halo_lo, n)], sem.at[k])

    copies = [fetch(xc_hbm, padc_ref, 0), fetch(xl_hbm, padl_ref, 1)]
    for cp in copies:
        cp.start()
    for x_hbm, pad_ref in ((xc_hbm, padc_ref), (xl_hbm, padl_ref)):
        n = x_hbm.shape[1]
        pad_ref[0:halo_lo, :] = zero
        pad_ref[halo_lo + n:halo_lo + n + halo_hi, :] = jnp.zeros((halo_hi, LANES), F32)
    for cp in copies:
        cp.wait()

    sp = [_softplus(-lam_ref[d:d + 1, :]) for d in range(2)]
    cb = cb_ref[...]
    cw = [cw_ref[j:j + 1, :] for j in range(CONV_W)]
    rows = tchunk * nb

    def segment(pad_ref, g_ref, o_ref, d, h):
        n = o_ref.shape[0] // rows

        def body(i, h):
            r0 = pl.multiple_of((i if d == 0 else n - 1 - i) * rows, rows)
            u = cb
            for j in range(CONV_W):
                u = u + pad_ref[pl.ds(r0 + j * nb, rows), :] * cw[j]
            ub = u.astype(BF16)
            ga = jnp.dot(ub, wg_ref[d, 0], preferred_element_type=F32) + bg_ref[d, 0]
            gi = jnp.dot(ub, wg_ref[d, 1], preferred_element_type=F32) + bg_ref[d, 1]
            a = jnp.exp((-LRU_C) * jax.nn.sigmoid(ga) * sp[d])
            b = jnp.sqrt(1.0 - a * a) * jax.nn.sigmoid(gi) * u
            hs = [None] * tchunk
            for t in (range(tchunk) if d == 0 else reversed(range(tchunk))):
                h = a[t * nb:(t + 1) * nb] * h + b[t * nb:(t + 1) * nb]
                hs[t] = h
            hcat = jnp.concatenate(hs, axis=0)
            if d == 0:
                o_ref[pl.ds(r0, rows), :] = hcat
            else:
                y = o_ref[pl.ds(r0, rows), :] + hcat
                o_ref[pl.ds(r0, rows), :] = jax.nn.gelu(g_ref[pl.ds(r0, rows), :]) * y
            return h

        return lax.fori_loop(0, n, body, h)

    for d in range(2):
        h = segment(padc_ref, gc_ref, oc_ref, d, zero)
        segment(padl_ref, gl_ref, ol_ref, d, h)


def _lru_call(xr_l, gr_l, xr_c, gr_c, conv_w, conv_b, wg, bg, lam):
    ncb, rl, _ = xr_l.shape
    rc = xr_c.shape[1]
    tchunk = min(LRU_TIME_CHUNK, rl // SUBLANES, rc // SUBLANES)
    halo = 3 * SUBLANES
    lat = pl.BlockSpec((None, rl, LANES), lambda c: (c, 0, 0))
    ctx = pl.BlockSpec((None, rc, LANES), lambda c: (c, 0, 0))
    per_block = lambda a: jnp.moveaxis(a.reshape(a.shape[:-1] + (ncb, LANES)), -2, 0)
    return pl.pallas_call(
        functools.partial(_lru_kernel, tchunk=tchunk),
        grid=(ncb,),
        in_specs=[pl.BlockSpec(memory_space=pl.ANY), pl.BlockSpec(memory_space=pl.ANY),
                  pl.BlockSpec((None, rl, LANES), lambda c: (c, 0, 0), pipeline_mode=pl.Buffered(1)), ctx,
                  pl.BlockSpec((None, CONV_W, LANES), lambda c: (c, 0, 0)),
                  pl.BlockSpec((None, 1, LANES), lambda c: (c, 0, 0)),
                  pl.BlockSpec((2, 2, None, LANES, LANES), lambda c: (0, 0, c, 0, 0)),
                  pl.BlockSpec((None, 2, 2, 1, LANES), lambda c: (c, 0, 0, 0, 0)),
                  pl.BlockSpec((None, 2, LANES), lambda c: (c, 0, 0))],
        out_specs=[lat, ctx],
        out_shape=[jax.ShapeDtypeStruct(xr_l.shape, F32), jax.ShapeDtypeStruct(xr_c.shape, F32)],
        scratch_shapes=[pltpu.VMEM((rl + halo, LANES), F32), pltpu.VMEM((rc + halo, LANES), F32),
                        pltpu.SemaphoreType.DMA((2,))],
        compiler_params=_params(("arbitrary",), 48),
        name="rglru",
    )(xr_l, xr_c, gr_l, gr_c, per_block(conv_w), per_block(conv_b.reshape(1, -1)), wg,
      per_block(bg[:, :, None, :]), per_block(lam))


def _lru_gate_weights(wa, ba, wx, bx, half):
    nblk = wa.shape[1]
    lw = nblk * LRU_BLOCK
    per = half // LRU_BLOCK

    def dense(w):
        w = w.reshape(2, nblk // per, per, LRU_BLOCK, LRU_BLOCK)
        eye = jnp.eye(per, dtype=w.dtype)
        full = w[:, :, :, :, None, :] * eye[None, None, :, None, :, None]
        return full.reshape(2, nblk // per, half, half)

    wg = jnp.stack([dense(wa), dense(wx)], axis=1).astype(BF16)
    bg = jnp.stack([ba, bx], axis=1).astype(F32)
    return wg, bg


def _out_proj_kernel(oa_ref, ol_ref, h_ref, g1_ref, w_ref, o_ref, *, aw):
    nb, tt, d = h_ref.shape
    o = jnp.dot(oa_ref[...].reshape(nb * tt, aw), w_ref[0:aw, :], preferred_element_type=F32)
    ol = _from_time_major(ol_ref, nb, tt).astype(BF16)
    o = o + jnp.dot(ol, w_ref[aw:, :], preferred_element_type=F32)
    o_ref[...] = h_ref[...] + g1_ref[...] * o.reshape(nb, tt, d)


def _out_proj_call(oa, ol, h, bmods, layer, w_bf):
    nb, t, d = h.shape
    aw = oa.shape[2]
    ncb = ol.shape[0]
    tt = min(TIME_TILE, t)
    bt = lambda w: pl.BlockSpec((nb, tt, w), lambda j: (0, j, 0))
    return pl.pallas_call(
        functools.partial(_out_proj_kernel, aw=aw),
        grid=(t // tt,),
        in_specs=[bt(aw), pl.BlockSpec((ncb, tt * SUBLANES, LANES), lambda j: (0, j, 0)), bt(d),
                  _batch_mod_spec(nb, d, layer, 2), pl.BlockSpec(w_bf.shape, lambda j: (0, 0))],
        out_specs=bt(d),
        out_shape=jax.ShapeDtypeStruct((nb, t, d), F32),
        compiler_params=_params(("arbitrary",), 32),
        name="out_proj",
    )(oa, ol, h, bmods, w_bf)


def _swiglu_step(a_bf, w1_ref, w3_ref, w2_ref, acc_ref):
    g = jnp.dot(a_bf, w1_ref[...].astype(BF16), preferred_element_type=F32)
    u = jnp.dot(a_bf, w3_ref[...].astype(BF16), preferred_element_type=F32)
    hmid = (g * jax.nn.sigmoid(g) * u).astype(BF16)
    acc_ref[...] += jnp.dot(hmid, w2_ref[...].astype(BF16), preferred_element_type=F32)


def _ffn_kernel(x_ref, g_ref, sh_ref, sc_ref, g2_ref, w1_ref, w3_ref, w2_ref, o_ref, a_ref, acc_ref):
    c = pl.program_id(1)

    @pl.when(c == 0)
    def _():
        a_ref[...] = _rms_mod(x_ref[...], g_ref[...], sh_ref[...], sc_ref[...]).astype(BF16)
        acc_ref[...] = jnp.zeros_like(acc_ref)

    _swiglu_step(a_ref[...], w1_ref, w3_ref, w2_ref, acc_ref)

    @pl.when(c == pl.num_programs(1) - 1)
    def _():
        o_ref[...] = x_ref[...] + g2_ref[...] * acc_ref[...]


def _ffn_call(h, g, mods, layer, w1, w3, w2, rows_per_mod, fixed_row):
    r, d = h.shape
    dff = w1.shape[1]
    tm = min(ROW_TILE_FFN, r)
    ms = functools.partial(_mod_spec, d, layer=layer, rows_per_mod=rows_per_mod, tm=tm, fixed_row=fixed_row)
    return pl.pallas_call(
        _ffn_kernel,
        grid=(r // tm, dff // FF_CHUNK),
        in_specs=[pl.BlockSpec((tm, d), lambda i, c: (i, 0)),
                  pl.BlockSpec((1, d), lambda i, c: (0, 0)),
                  ms(col=3), ms(col=4), ms(col=5),
                  pl.BlockSpec((d, FF_CHUNK), lambda i, c: (0, c)),
                  pl.BlockSpec((d, FF_CHUNK), lambda i, c: (0, c)),
                  pl.BlockSpec((FF_CHUNK, d), lambda i, c: (c, 0))],
        out_specs=pl.BlockSpec((tm, d), lambda i, c: (i, 0)),
        out_shape=jax.ShapeDtypeStruct((r, d), F32),
        scratch_shapes=[pltpu.VMEM((tm, d), BF16), pltpu.VMEM((tm, d), F32)],
        compiler_params=_params(("arbitrary", "arbitrary"), 56),
        name="ffn_dense",
    )(h, g, mods, mods, mods, w1, w3, w2)


def _to_row_tiles(x, o_ref):
    t, d = x.shape
    ns = d // LANES
    for s in range(ns):
        o_ref[pl.ds(s, t, stride=ns), :] = x[:, s * LANES:(s + 1) * LANES]


def _from_row_tiles(x_ref, idx, t, ns):
    return jnp.concatenate([x_ref[idx + (pl.ds(s, t, stride=ns), slice(None))] for s in range(ns)], axis=1)


def _gffn_kernel(te_ref, nu_ref, x_ref, w1_ref, w3_ref, w2_ref, o_ref, a_ref, acc_ref):
    i = pl.program_id(0)
    c = pl.program_id(1)
    tm, d = a_ref.shape

    @pl.when(i < nu_ref[0])
    def _():
        @pl.when(c == 0)
        def _():
            a_ref[...] = _from_row_tiles(x_ref, (), tm, d // LANES).astype(BF16)
            acc_ref[...] = jnp.zeros_like(acc_ref)

        _swiglu_step(a_ref[...], w1_ref, w3_ref, w2_ref, acc_ref)

        @pl.when(c == pl.num_programs(1) - 1)
        def _():
            _to_row_tiles(acc_ref[...], o_ref)

    @pl.when((i >= nu_ref[0]) & (c == pl.num_programs(1) - 1))
    def _():
        o_ref[...] = jnp.zeros_like(o_ref)


def _gffn_call(tile_expert, n_used, xg, w1, w3, w2):
    d = w1.shape[1]
    ns = d // LANES
    rg = xg.shape[0] // ns
    dff = w1.shape[2]
    tm = ROW_TILE_FFN
    nc = dff // FF_CHUNK

    def tile(i, nu):
        return jnp.maximum(jnp.minimum(i, nu[0] - 1), 0)

    def chunk(i, c, nu):
        return jnp.where(i < nu[0], c, nc - 1)

    grid_spec = pltpu.PrefetchScalarGridSpec(
        num_scalar_prefetch=2,
        grid=(rg // tm, nc),
        in_specs=[pl.BlockSpec((tm * ns, LANES), lambda i, c, te, nu: (tile(i, nu), 0)),
                  pl.BlockSpec((None, d, FF_CHUNK), lambda i, c, te, nu: (te[tile(i, nu)], 0, chunk(i, c, nu))),
                  pl.BlockSpec((None, d, FF_CHUNK), lambda i, c, te, nu: (te[tile(i, nu)], 0, chunk(i, c, nu))),
                  pl.BlockSpec((None, FF_CHUNK, d), lambda i, c, te, nu: (te[tile(i, nu)], chunk(i, c, nu), 0))],
        out_specs=pl.BlockSpec((tm * ns, LANES), lambda i, c, te, nu: (i, 0)),
        scratch_shapes=[pltpu.VMEM((tm, d), BF16), pltpu.VMEM((tm, d), F32)])
    return pl.pallas_call(
        _gffn_kernel,
        grid_spec=grid_spec,
        out_shape=jax.ShapeDtypeStruct((rg * ns, LANES), F32),
        compiler_params=_params(("arbitrary", "arbitrary"), 56),
        name="ffn_grouped",
    )(tile_expert, n_used, xg, w1, w3, w2)


def _route_kernel(x_ref, g_ref, sh_ref, sc_ref, wr_ref, br_ref, a_ref, tab_ref, gcol_ref, cnt_ref, carry_ref):
    i = pl.program_id(0)
    tm = x_ref.shape[0]

    @pl.when(i == 0)
    def _():
        carry_ref[...] = jnp.zeros_like(carry_ref)

    a = _rms_mod(x_ref[...], g_ref[...], sh_ref[...], sc_ref[...])
    _to_row_tiles(a, a_ref)
    lane = lax.broadcasted_iota(I32, (tm, LANES), 1).astype(F32)
    logits = jnp.dot(a, wr_ref[...], preferred_element_type=F32, precision=lax.Precision.HIGHEST) + br_ref[...]
    logits = jnp.where(lane < N_EXPERTS, logits, NEG_INF)
    m1 = jnp.max(logits, axis=-1, keepdims=True)
    i1 = jnp.min(jnp.where(logits == m1, lane, float(LANES)), axis=-1, keepdims=True)
    rest = jnp.where(lane == i1, 2.0 * NEG_INF, logits)
    m2 = jnp.max(rest, axis=-1, keepdims=True)
    i2 = jnp.min(jnp.where(rest == m2, lane, float(LANES)), axis=-1, keepdims=True)
    e = jnp.exp(m2 - m1)
    w1 = 1.0 / (1.0 + e)
    w2 = e * w1
    sel1 = lane == i1
    sel2 = lane == i2
    onehot = jnp.where(sel1 | sel2, 1.0, 0.0)
    row = lax.broadcasted_iota(I32, (tm, tm), 0)
    col = lax.broadcasted_iota(I32, (tm, tm), 1)
    before = jnp.where(col < row, 1.0, 0.0).astype(BF16)
    cum = jnp.dot(before, onehot.astype(BF16), preferred_element_type=F32) + carry_ref[...]
    r1 = jnp.sum(jnp.where(sel1, cum, 0.0), axis=-1, keepdims=True)
    r2 = jnp.sum(jnp.where(sel2, cum, 0.0), axis=-1, keepdims=True)
    carry_ref[...] += jnp.sum(onehot, axis=0, keepdims=True)
    cnt_ref[...] = carry_ref[...]
    cols = (jnp.where(lane == 0, i1, 0.0) + jnp.where(lane == 1, i2, 0.0)
            + jnp.where(lane == 2, r1, 0.0) + jnp.where(lane == 3, r2, 0.0))
    tab_ref[...] = cols.T[0:SUBLANES, :]
    gcol_ref[...] = (jnp.where(lane == 0, w1, 0.0) + jnp.where(lane == 1, w2, 0.0))[:, 0:SUBLANES]


def _route_call(h, g, mods, layer, wr, br, rows_per_mod):
    n, d = h.shape
    tm = min(ROW_TILE, n)
    ms = functools.partial(_mod_spec, d, layer=layer, rows_per_mod=rows_per_mod, tm=tm, fixed_row=None)
    return pl.pallas_call(
        _route_kernel,
        grid=(n // tm,),
        in_specs=[pl.BlockSpec((tm, d), lambda i: (i, 0)),
                  pl.BlockSpec((1, d), lambda i: (0, 0)), ms(col=3), ms(col=4),
                  pl.BlockSpec((d, LANES), lambda i: (0, 0)),
                  pl.BlockSpec((1, LANES), lambda i: (0, 0))],
        out_specs=[pl.BlockSpec((tm * (d // LANES), LANES), lambda i: (i, 0)),
                   pl.BlockSpec((SUBLANES, tm), lambda i: (0, i)),
                   pl.BlockSpec((tm, SUBLANES), lambda i: (i, 0)),
                   pl.BlockSpec((1, LANES), lambda i: (0, 0))],
        out_shape=[jax.ShapeDtypeStruct((n * (d // LANES), LANES), F32),
                   jax.ShapeDtypeStruct((SUBLANES, n), F32),
                   jax.ShapeDtypeStruct((n, SUBLANES), F32),
                   jax.ShapeDtypeStruct((1, LANES), F32)],
        scratch_shapes=[pltpu.VMEM((1, LANES), F32)],
        compiler_params=_params(("arbitrary",), 40),
        name="moe_route",
    )(h, g, mods, mods, wr, br)


DMA_UNROLL = 4


def _dispatch_kernel(off_ref, tab_ref, a_ref, xg_in, xg_out, pos_ref, sem, *, ns):
    del xg_in
    tm = tab_ref.shape[1]

    def row_copy(src, dst):
        return pltpu.make_async_copy(a_ref.at[pl.ds(pl.multiple_of(src * ns, ns), ns)],
                                     xg_out.at[pl.ds(pl.multiple_of(dst * ns, ns), ns)], sem)

    def issue(j, carry):
        for u in range(DMA_UNROLL):
            t = j * DMA_UNROLL + u
            for k in range(2):
                p = off_ref[tab_ref[k, t]] + tab_ref[2 + k, t]
                pos_ref[k, t] = p
                row_copy(t, p).start()
        return carry

    lax.fori_loop(0, tm // DMA_UNROLL, issue, 0)

    def drain(j, carry):
        for _ in range(2 * DMA_UNROLL):
            row_copy(0, 0).wait()
        return carry

    lax.fori_loop(0, tm // DMA_UNROLL, drain, 0)


def _dispatch_call(off, tab_i, a, rows_g, ns):
    n = a.shape[0] // ns
    tm = min(ROW_TILE, n)
    grid_spec = pltpu.PrefetchScalarGridSpec(
        num_scalar_prefetch=1,
        grid=(n // tm,),
        in_specs=[pl.BlockSpec((4, tm), lambda i, off: (0, i), memory_space=pltpu.SMEM),
                  pl.BlockSpec((tm * ns, LANES), lambda i, off: (i, 0)),
                  pl.BlockSpec(memory_space=pl.ANY)],
        out_specs=[pl.BlockSpec(memory_space=pl.ANY),
                   pl.BlockSpec((2, tm), lambda i, off: (0, i), memory_space=pltpu.SMEM)],
        scratch_shapes=[pltpu.SemaphoreType.DMA(())])
    xg0 = jnp.zeros((rows_g * ns, LANES), a.dtype)
    return pl.pallas_call(
        functools.partial(_dispatch_kernel, ns=ns),
        grid_spec=grid_spec,
        out_shape=[jax.ShapeDtypeStruct((rows_g * ns, LANES), a.dtype), jax.ShapeDtypeStruct((2, n), I32)],
        input_output_aliases={3: 0},
        compiler_params=_params(("arbitrary",), 16),
        name="moe_dispatch",
    )(off, tab_i, a, xg0)


def _combine_kernel(pos_ref, posn_ref, y_hbm, h_ref, gcol_ref, g2_ref, fg_ref, o_ref, buf_ref, sem):
    i = pl.program_id(0)
    n = pl.num_programs(0)
    tm, d = h_ref.shape
    ns = d // LANES

    def row_copy(src, slot, k, t):
        return pltpu.make_async_copy(y_hbm.at[pl.ds(pl.multiple_of(src * ns, ns), ns)],
                                     buf_ref.at[slot, k, pl.ds(pl.multiple_of(t * ns, ns), ns)], sem.at[slot])

    def issue(p_ref, slot):
        def body(j, carry):
            for u in range(DMA_UNROLL):
                t = j * DMA_UNROLL + u
                for k in range(2):
                    row_copy(p_ref[k, t], slot, k, t).start()
            return carry
        lax.fori_loop(0, tm // DMA_UNROLL, body, 0)

    @pl.when(i == 0)
    def _():
        issue(pos_ref, 0)

    @pl.when(i + 1 < n)
    def _():
        issue(posn_ref, (i + 1) % 2)

    slot = i % 2

    def drain(j, carry):
        for _ in range(2 * DMA_UNROLL):
            row_copy(0, slot, 0, 0).wait()
        return carry

    lax.fori_loop(0, tm // DMA_UNROLL, drain, 0)
    y = (gcol_ref[:, 0:1] * _from_row_tiles(buf_ref, (slot, 0), tm, ns)
         + gcol_ref[:, 1:2] * _from_row_tiles(buf_ref, (slot, 1), tm, ns))
    hn = h_ref[...] + g2_ref[...] * y
    o_ref[...] = hn * lax.rsqrt(jnp.mean(hn * hn, axis=-1, keepdims=True) + EPS) * fg_ref[...]


def _combine_call(pos, y, h, gcol, mods, layer, final_g, rows_per_mod):
    n, d = h.shape
    tm = min(GATHER_TILE, n)
    nt = n // tm
    ms = functools.partial(_mod_spec, d, layer=layer, rows_per_mod=rows_per_mod, tm=tm, fixed_row=None)
    return pl.pallas_call(
        _combine_kernel,
        grid=(nt,),
        in_specs=[pl.BlockSpec((2, tm), lambda i: (0, i), memory_space=pltpu.SMEM),
                  pl.BlockSpec((2, tm), lambda i: (0, jnp.minimum(i + 1, nt - 1)), memory_space=pltpu.SMEM),
                  pl.BlockSpec(memory_space=pl.ANY),
                  pl.BlockSpec((tm, d), lambda i: (i, 0)),
                  pl.BlockSpec((tm, SUBLANES), lambda i: (i, 0)),
                  ms(col=5),
                  pl.BlockSpec((1, d), lambda i: (0, 0))],
        out_specs=pl.BlockSpec((tm, d), lambda i: (i, 0)),
        out_shape=jax.ShapeDtypeStruct((n, d), F32),
        scratch_shapes=[pltpu.VMEM((2, 2, tm * (d // LANES), LANES), F32), pltpu.SemaphoreType.DMA((2,))],
        compiler_params=_params(("arbitrary",), 24),
        name="moe_combine",
    )(pos, pos, y, h, gcol, mods, final_g)


def _moe_layer(h, g, mods, layer, router, router_b, w1, w3, w2, final_g, rows_per_mod):
    n, d = h.shape
    wr = jnp.zeros((d, LANES), F32).at[:, :N_EXPERTS].set(router)
    br = jnp.zeros((1, LANES), F32).at[0, :N_EXPERTS].set(router_b)
    a, tab, gcol, cnt = _route_call(h, g, mods, layer, wr, br, rows_per_mod)
    tm = ROW_TILE_FFN
    n_tiles = (2 * n) // tm + N_EXPERTS
    counts = cnt[0, :N_EXPERTS].astype(I32)
    tiles = (counts + tm - 1) // tm
    ends = jnp.cumsum(tiles)
    off = ((ends - tiles) * tm).astype(I32)
    tile_expert = jnp.minimum(jnp.sum(jnp.arange(n_tiles, dtype=I32)[:, None] >= ends[None, :], axis=1),
                              N_EXPERTS - 1).astype(I32)
    n_used = ends[-1:].astype(I32)
    assert d // LANES == SUBLANES, "one token must fill one (8, 128) tile of the row-tile layout"
    xg, pos = _dispatch_call(off, tab[0:4].astype(I32), a, n_tiles * tm, d // LANES)
    y = _gffn_call(tile_expert, n_used, xg, w1, w3, w2)
    return _combine_call(pos, y, h, gcol, mods, layer, final_g, rows_per_mod)


def kernel(x, c, ctx, c_ctx, ada_w, ada_b, mix_norm_g, ffn_norm_g, w_in, w_out, na_rpb, conv_w, conv_b, lru_wa, lru_ba, lru_wx, lru_bx, lru_lam, ffn_w1, ffn_w3, ffn_w2, moe_router, moe_router_b, moe_w1, moe_w3, moe_w2, final_g):
    batch, seq, d = x.shape
    ctx_len = ctx.shape[1]
    depth = ada_w.shape[0]
    lw = conv_w.shape[-1]
    aw = (w_in.shape[-1] - 2 * lw) // 3
    rows = seq // GRID_W
    ctx_row = batch

    mod_rows = -(-(batch + 1) // SUBLANES) * SUBLANES
    c_all = jnp.zeros((mod_rows, d), F32).at[:batch].set(c).at[ctx_row].set(c_ctx)
    mods = _ada_call(c_all, ada_w, ada_b).reshape(depth, mod_rows, 1, 6 * d)

    bmods_lat = mods[:, :batch]
    bmods_ctx = jnp.broadcast_to(mods[:, ctx_row:ctx_row + 1], bmods_lat.shape)
    flat = lambda a: a.reshape(a.shape[0] * a.shape[1], a.shape[2])
    h_lat, h_ctx = x, ctx
    out = None
    for i in range(depth):
        last = i == depth - 1
        w_in_bf = w_in[i].astype(BF16)
        w_out_bf = w_out[i].astype(BF16)
        g_mix = mix_norm_g[i].reshape(1, d)
        g_ffn = ffn_norm_g[i].reshape(1, d)
        q_l, k_l, v_l, xr_l, gr_l = _in_proj_call(h_lat, g_mix, bmods_lat, i, w_in_bf, aw, lw)
        q_c, k_c, v_c, xr_c, gr_c = _in_proj_call(h_ctx, g_mix, bmods_ctx, i, w_in_bf, aw, lw)
        oa_l = _na_call(flat(q_l), flat(k_l), flat(v_l), flat(k_c), flat(v_c), _pad_rpb(na_rpb[i]), batch)
        wg, bg = _lru_gate_weights(lru_wa[i], lru_ba[i], lru_wx[i], lru_bx[i], LANES)
        ol_l, ol_c = _lru_call(xr_l, gr_l, xr_c, gr_c, conv_w[i], conv_b[i], wg, bg, lru_lam[i])
        h_lat = _out_proj_call(oa_l.reshape(batch, seq, aw), ol_l, h_lat, bmods_lat, i, w_out_bf)
        if not last:
            oa_c = _ctx_attn_call(flat(q_c), flat(k_c), flat(v_c), batch)
            h_ctx = _out_proj_call(oa_c.reshape(batch, ctx_len, aw), ol_c, h_ctx, bmods_ctx, i, w_out_bf)
        j = i // 2
        if i % 2 == 0:
            h_lat = _ffn_call(flat(h_lat), g_ffn, mods, i, ffn_w1[j], ffn_w3[j], ffn_w2[j],
                              seq, None).reshape(batch, seq, d)
            if not last:
                h_ctx = _ffn_call(flat(h_ctx), g_ffn, mods, i, ffn_w1[j], ffn_w3[j], ffn_w2[j],
                                  None, ctx_row).reshape(batch, ctx_len, d)
        else:
            assert last, "the routed layer is fused with the final norm"
            out = _moe_layer(flat(h_lat), g_ffn, mods, i, moe_router[j], moe_router_b[j],
                             moe_w1[j], moe_w3[j], moe_w2[j], final_g.reshape(1, d), seq)
    return out.reshape(batch, seq, d)
```

```python
import functools

import jax
import jax.numpy as jnp
from jax import lax
from jax.experimental import pallas as pl
from jax.experimental.pallas import tpu as pltpu

F32 = jnp.float32
BF16 = jnp.bfloat16
I32 = jnp.int32

GRID_W = 64
HEAD_DIM = 64
N_HEADS = 8
NA_KH = 8
NA_KW = 16
LRU_BLOCK = 64
LRU_C = 8.0
CONV_W = 4
N_EXPERTS = 8
EPS = 1e-6
NEG_INF = -1e30
LOG2E = 1.4426950408889634
LANES = 128
SUBLANES = 8
VMEM_BYTES = 64 * 1024 * 1024

FF_CHUNK = 512
ROW_TILE_FFN = 1024
ROW_TILE = 512
GATHER_TILE = 256


def _params(semantics, vmem_mb):
    return pltpu.CompilerParams(dimension_semantics=semantics,
                                vmem_limit_bytes=min(vmem_mb * 1024 * 1024, VMEM_BYTES - 4 * 1024 * 1024))


def _rms_mod(x, g, sh, sc):
    y = x * lax.rsqrt(jnp.mean(x * x, axis=-1, keepdims=True) + EPS)
    return (y * g) * (1.0 + sc) + sh


def _ada_kernel(c_ref, w_ref, b_ref, o_ref):
    c = c_ref[...]
    s = c * jax.nn.sigmoid(c)
    o_ref[...] = jnp.dot(s, w_ref[...], preferred_element_type=F32,
                         precision=lax.Precision.HIGHEST) + b_ref[...]


def _ada_call(c_all, ada_w, ada_b):
    depth, d, n = ada_w.shape
    rows = c_all.shape[0]
    tn = 1024
    return pl.pallas_call(
        _ada_kernel,
        grid=(depth, n // tn),
        in_specs=[pl.BlockSpec((rows, d), lambda l, j: (0, 0)),
                  pl.BlockSpec((None, d, tn), lambda l, j: (l, 0, j)),
                  pl.BlockSpec((None, 1, tn), lambda l, j: (l, 0, j))],
        out_specs=pl.BlockSpec((None, rows, tn), lambda l, j: (l, 0, j)),
        out_shape=jax.ShapeDtypeStruct((depth, rows, n), F32),
        compiler_params=_params(("arbitrary", "arbitrary"), 24),
        name="ada_mod",
    )(c_all, ada_w, ada_b.reshape(depth, 1, n))


TIME_TILE = 64


def _to_time_major(r, o_ref, nb, tt):
    for c in range(o_ref.shape[0]):
        for b in range(nb):
            o_ref[c, pl.ds(b, tt, stride=SUBLANES), :] = r[b * tt:(b + 1) * tt, c * LANES:(c + 1) * LANES]


def _from_time_major(x_ref, nb, tt):
    return jnp.concatenate(
        [jnp.concatenate([x_ref[c, pl.ds(b, tt, stride=SUBLANES), :] for c in range(x_ref.shape[0])], axis=1)
         for b in range(nb)], axis=0)


def _in_proj_kernel(x_ref, g_ref, sh_ref, sc_ref, w_ref, q_ref, k_ref, v_ref, xr_ref, gr_ref, *, aw, lw):
    nb, tt, d = x_ref.shape
    a = _rms_mod(x_ref[...], g_ref[...], sh_ref[...], sc_ref[...]).astype(BF16).reshape(nb * tt, d)
    r = jnp.dot(a, w_ref[...], preferred_element_type=F32)
    q_ref[...] = (r[:, 0:aw] * (LOG2E * HEAD_DIM ** -0.5)).astype(BF16).reshape(nb, tt, aw)
    k_ref[...] = r[:, aw:2 * aw].astype(BF16).reshape(nb, tt, aw)
    v_ref[...] = r[:, 2 * aw:3 * aw].astype(BF16).reshape(nb, tt, aw)
    _to_time_major(r[:, 3 * aw:3 * aw + lw], xr_ref, nb, tt)
    _to_time_major(r[:, 3 * aw + lw:3 * aw + 2 * lw], gr_ref, nb, tt)


def _mod_spec(d, col, layer, rows_per_mod, tm, fixed_row):
    if fixed_row is None:
        assert rows_per_mod % tm == 0, "a row tile must not straddle two batch elements"
        return pl.BlockSpec((None, None, 1, d), lambda i, *_: (layer, (i * tm) // rows_per_mod, 0, col))
    return pl.BlockSpec((None, None, 1, d), lambda i, *_: (layer, fixed_row, 0, col))


def _batch_mod_spec(nb, d, layer, col):
    return pl.BlockSpec((None, nb, 1, d), lambda j: (layer, 0, 0, col))


def _in_proj_call(x, g, bmods, layer, w_bf, aw, lw):
    nb, t, d = x.shape
    assert nb == SUBLANES, "time-major rows put the batch on the sublanes"
    tt = min(TIME_TILE, t)
    ncol = w_bf.shape[1]
    ncb = lw // LANES
    bt = lambda w: pl.BlockSpec((nb, tt, w), lambda j: (0, j, 0))
    tm = pl.BlockSpec((ncb, tt * SUBLANES, LANES), lambda j: (0, j, 0))
    return pl.pallas_call(
        functools.partial(_in_proj_kernel, aw=aw, lw=lw),
        grid=(t // tt,),
        in_specs=[bt(d), pl.BlockSpec((1, d), lambda j: (0, 0)),
                  _batch_mod_spec(nb, d, layer, 0), _batch_mod_spec(nb, d, layer, 1),
                  pl.BlockSpec((d, ncol), lambda j: (0, 0))],
        out_specs=[bt(aw), bt(aw), bt(aw), tm, tm],
        out_shape=[jax.ShapeDtypeStruct((nb, t, aw), BF16)] * 3
        + [jax.ShapeDtypeStruct((ncb, t * SUBLANES, LANES), F32)] * 2,
        compiler_params=_params(("arbitrary",), 40),
        name="in_proj",
    )(x, g, bmods, bmods, w_bf)


def _softmax_pv(parts):
    m = None
    for s, _ in parts:
        mi = jnp.max(s, axis=-1, keepdims=True)
        m = mi if m is None else jnp.maximum(m, mi)
    l = None
    acc = None
    for s, v in parts:
        p = jnp.exp2(s - m)
        li = jnp.sum(p, axis=-1, keepdims=True)
        ai = jnp.dot(p.astype(BF16), v, preferred_element_type=F32)
        l = li if l is None else l + li
        acc = ai if acc is None else acc + ai
    return acc * (1.0 / l)


def _qk(q, k):
    return lax.dot_general(q, k, (((1,), (1,)), ((), ())), preferred_element_type=F32)


def _build_na_bias(rpb_ref, bias_ref):
    lane = lax.broadcasted_iota(I32, (GRID_W, LANES), 1)
    q = lax.broadcasted_iota(I32, (GRID_W, LANES), 0)
    kcol = lane & (GRID_W - 1)
    cs = jnp.clip(q - NA_KW // 2, 0, GRID_W - NA_KW)
    ok = (kcol >= cs) & (kcol < cs + NA_KW)
    low = lane < GRID_W
    for delta in range(NA_KH):
        for h in range(N_HEADS):
            for jp in range(NA_KH // 2):
                halves = []
                for j in (2 * jp, 2 * jp + 1):
                    dr = j - delta + NA_KH - 1
                    w = jnp.broadcast_to(rpb_ref[h, dr:dr + 1, :], (GRID_W, LANES))
                    base = (j % 2) * GRID_W - (NA_KW - 1)
                    halves.append(pltpu.roll(w, base % LANES, 1, stride=1, stride_axis=0))
                t = jnp.where(low, halves[0], halves[1])
                bias_ref[delta, h, :, jp * LANES:(jp + 1) * LANES] = jnp.where(ok, t * LOG2E, NEG_INF)


NA_ROWS_PER_STEP = 4


def _stack_pair(q2):
    first = lax.broadcasted_iota(I32, q2.shape, 1) < HEAD_DIM
    zero = jnp.zeros_like(q2)
    return jnp.concatenate([jnp.where(first, q2, zero), jnp.where(first, zero, q2)], axis=0)


def _unstack_pair(o2):
    nq = o2.shape[0] // 2
    first = lax.broadcasted_iota(I32, (nq, o2.shape[1]), 1) < HEAD_DIM
    return jnp.where(first, o2[:nq], o2[nq:])


def _na_kernel(q_ref, kl_ref, vl_ref, kc_ref, vc_ref, rpb_ref, o_ref, bias_ref, s_ref, p_ref, *, rows, rb):
    @pl.when((pl.program_id(0) == 0) & (pl.program_id(1) == 0))
    def _():
        _build_na_bias(rpb_ref, bias_ref)

    nwin = NA_KH * GRID_W
    npair = N_HEADS // 2
    for rho in range(rb):
        r = pl.program_id(1) * rb + rho
        rs = jnp.clip(r - NA_KH // 2, 0, rows - NA_KH)
        delta = r - rs
        k0 = pl.multiple_of(rs * GRID_W, GRID_W)
        qrows = slice(rho * GRID_W, (rho + 1) * GRID_W)
        for pair in range(npair):
            sl = slice(pair * 2 * HEAD_DIM, (pair + 1) * 2 * HEAD_DIM)
            keys = jnp.concatenate([kl_ref[pl.ds(k0, nwin), sl], kc_ref[:, sl]], axis=0)
            s = _qk(_stack_pair(q_ref[qrows, sl]), keys)
            for sub in range(2):
                h = pair * 2 + sub
                rr = slice(sub * GRID_W, (sub + 1) * GRID_W)
                s_ref[rho, h, :, 0:nwin] = s[rr, 0:nwin] + bias_ref[delta, h]
                s_ref[rho, h, :, nwin:] = s[rr, nwin:]
        s = s_ref[rho]
        p = jnp.exp2(s - jnp.max(s, axis=-1, keepdims=True))
        inv = 1.0 / jnp.sum(p, axis=-1, keepdims=True)
        p_ref[rho] = p.astype(BF16)
        for pair in range(npair):
            sl = slice(pair * 2 * HEAD_DIM, (pair + 1) * 2 * HEAD_DIM)
            vals = jnp.concatenate([vl_ref[pl.ds(k0, nwin), sl], vc_ref[:, sl]], axis=0)
            p2 = p_ref[rho, 2 * pair:2 * pair + 2].reshape(2 * GRID_W, p_ref.shape[-1])
            o2 = jnp.dot(p2, vals, preferred_element_type=F32) * inv[2 * pair:2 * pair + 2].reshape(2 * GRID_W, 1)
            o_ref[qrows, sl] = _unstack_pair(o2).astype(o_ref.dtype)


def _na_call(q, kl, vl, kc, vc, rpb, batch):
    s = q.shape[0] // batch
    l = kc.shape[0] // batch
    w = q.shape[1]
    rows = s // GRID_W
    assert rows >= NA_KH
    rb = NA_ROWS_PER_STEP if rows % NA_ROWS_PER_STEP == 0 else 1
    nkeys = NA_KH * GRID_W + l
    nrb = rows // rb
    return pl.pallas_call(
        functools.partial(_na_kernel, rows=rows, rb=rb),
        grid=(batch, nrb),
        in_specs=[pl.BlockSpec((rb * GRID_W, w), lambda b, r: (b * nrb + r, 0)),
                  pl.BlockSpec((s, w), lambda b, r: (b, 0)),
                  pl.BlockSpec((s, w), lambda b, r: (b, 0)),
                  pl.BlockSpec((l, w), lambda b, r: (b, 0)),
                  pl.BlockSpec((l, w), lambda b, r: (b, 0)),
                  pl.BlockSpec(rpb.shape, lambda b, r: (0, 0, 0))],
        out_specs=pl.BlockSpec((rb * GRID_W, w), lambda b, r: (b * nrb + r, 0)),
        out_shape=jax.ShapeDtypeStruct(q.shape, BF16),
        scratch_shapes=[pltpu.VMEM((NA_KH, N_HEADS, GRID_W, NA_KH * GRID_W), F32),
                        pltpu.VMEM((rb, N_HEADS, GRID_W, nkeys), F32),
                        pltpu.VMEM((rb, N_HEADS, GRID_W, nkeys), BF16)],
        compiler_params=_params(("arbitrary", "arbitrary"), 48),
        name="na_attention",
    )(q, kl, vl, kc, vc, rpb)


def _pad_rpb(rpb):
    h, nr, nc = rpb.shape
    return jnp.zeros((h, 2 * NA_KH, LANES), F32).at[:, :nr, :nc].set(rpb.astype(F32))


def _ctx_attn_kernel(q_ref, k_ref, v_ref, o_ref):
    rows = q_ref.shape[0]
    lane = lax.broadcasted_iota(I32, (rows, 2 * HEAD_DIM), 1)
    first = lane < HEAD_DIM
    for pair in range(N_HEADS // 2):
        sl = slice(pair * 2 * HEAD_DIM, (pair + 1) * 2 * HEAD_DIM)
        q2 = q_ref[:, sl]
        k = k_ref[:, sl]
        v = v_ref[:, sl]
        outs = []
        for sub in range(2):
            qh = jnp.where(first if sub == 0 else ~first, q2, jnp.zeros_like(q2))
            outs.append(_softmax_pv([(_qk(qh, k), v)]))
        o_ref[:, sl] = jnp.where(first, outs[0], outs[1]).astype(o_ref.dtype)


def _ctx_attn_call(q, k, v, batch):
    l = q.shape[0] // batch
    w = q.shape[1]
    spec = pl.BlockSpec((l, w), lambda b: (b, 0))
    return pl.pallas_call(
        _ctx_attn_kernel,
        grid=(batch,),
        in_specs=[spec, spec, spec],
        out_specs=spec,
        out_shape=jax.ShapeDtypeStruct(q.shape, BF16),
        compiler_params=_params(("arbitrary",), 24),
        name="ctx_attention",
    )(q, k, v)


def _softplus(x):
    return jnp.maximum(x, 0.0) + jnp.log1p(jnp.exp(-jnp.abs(x)))


LRU_TIME_CHUNK = 64


def _lru_kernel(xl_hbm, xc_hbm, gl_ref, gc_ref, cw_ref, cb_ref, wg_ref, bg_ref, lam_ref, ol_ref, oc_ref,
                padl_ref, padc_ref, ul_ref, uc_ref, sem, *, tchunk):
    c = pl.program_id(0)
    nb = SUBLANES
    halo_lo, halo_hi = nb, 2 * nb
    zero = jnp.zeros((nb, LANES), F32)

    def fetch(x_hbm, pad_ref, k):
        n = x_hbm.shape[1]
        return pltpu.make_async_copy(x_hbm.at[c], pad_ref.at[pl.ds(halo_lo, n)], sem.at[k])

    copies = [fetch(xc_hbm, padc_ref, 0), fetch(xl_hbm, padl_ref, 1)]
    for cp in copies:
        cp.start()
    for x_hbm, pad_ref in ((xc_hbm, padc_ref), (xl_hbm, padl_ref)):
        n = x_hbm.shape[1]
        pad_ref[0:halo_lo, :] = zero
        pad_ref[halo_lo + n:halo_lo + n + halo_hi, :] = jnp.zeros((halo_hi, LANES), F32)
    for cp in copies:
        cp.wait()

    la = [(-0.5 * LRU_C * LOG2E) * _softplus(-lam_ref[d:d + 1, :]) for d in range(2)]
    cb = cb_ref[...]
    cw = [cw_ref[j:j + 1, :] for j in range(CONV_W)]
    rows = tchunk * nb

    def segment(pad_ref, u_ref, g_ref, o_ref, d, h):
        n = o_ref.shape[0] // rows

        def body(i, h):
            r0 = pl.multiple_of((i if d == 0 else n - 1 - i) * rows, rows)
            if d == 0:
                u = cb
                for j in range(CONV_W):
                    u = u + pad_ref[pl.ds(r0 + j * nb, rows), :] * cw[j]
                u_ref[pl.ds(r0, rows), :] = u
            else:
                u = u_ref[pl.ds(r0, rows), :]
            ub = u.astype(BF16)
            ta = jnp.tanh(jnp.dot(ub, wg_ref[d, 0], preferred_element_type=F32) + bg_ref[d, 0])
            ti = jnp.tanh(jnp.dot(ub, wg_ref[d, 1], preferred_element_type=F32) + bg_ref[d, 1])
            a = jnp.exp2(la[d] * ta + la[d])
            z = 1.0 - a * a
            b = jnp.where(z > 0.0, z * lax.rsqrt(z), 0.0) * (0.5 * ti + 0.5) * u
            hs = [None] * tchunk
            for t in (range(tchunk) if d == 0 else reversed(range(tchunk))):
                h = a[t * nb:(t + 1) * nb] * h + b[t * nb:(t + 1) * nb]
                hs[t] = h
            hcat = jnp.concatenate(hs, axis=0)
            if d == 0:
                o_ref[pl.ds(r0, rows), :] = hcat
            else:
                y = o_ref[pl.ds(r0, rows), :] + hcat
                o_ref[pl.ds(r0, rows), :] = jax.nn.gelu(g_ref[pl.ds(r0, rows), :]) * y
            return h

        return lax.fori_loop(0, n, body, h)

    for d in range(2):
        h = segment(padc_ref, uc_ref, gc_ref, oc_ref, d, zero)
        segment(padl_ref, ul_ref, gl_ref, ol_ref, d, h)


def _lru_call(xr_l, gr_l, xr_c, gr_c, conv_w, conv_b, wg, bg, lam):
    ncb, rl, _ = xr_l.shape
    rc = xr_c.shape[1]
    tchunk = min(LRU_TIME_CHUNK, rl // SUBLANES, rc // SUBLANES)
    halo = 3 * SUBLANES
    lat = pl.BlockSpec((None, rl, LANES), lambda c: (c, 0, 0))
    ctx = pl.BlockSpec((None, rc, LANES), lambda c: (c, 0, 0))
    per_block = lambda a: jnp.moveaxis(a.reshape(a.shape[:-1] + (ncb, LANES)), -2, 0)
    return pl.pallas_call(
        functools.partial(_lru_kernel, tchunk=tchunk),
        grid=(ncb,),
        in_specs=[pl.BlockSpec(memory_space=pl.ANY), pl.BlockSpec(memory_space=pl.ANY),
                  pl.BlockSpec((None, rl, LANES), lambda c: (c, 0, 0), pipeline_mode=pl.Buffered(1)), ctx,
                  pl.BlockSpec((None, CONV_W, LANES), lambda c: (c, 0, 0)),
                  pl.BlockSpec((None, 1, LANES), lambda c: (c, 0, 0)),
                  pl.BlockSpec((2, 2, None, LANES, LANES), lambda c: (0, 0, c, 0, 0)),
                  pl.BlockSpec((None, 2, 2, 1, LANES), lambda c: (c, 0, 0, 0, 0)),
                  pl.BlockSpec((None, 2, LANES), lambda c: (c, 0, 0))],
        out_specs=[lat, ctx],
        out_shape=[jax.ShapeDtypeStruct(xr_l.shape, F32), jax.ShapeDtypeStruct(xr_c.shape, F32)],
        scratch_shapes=[pltpu.VMEM((rl + halo, LANES), F32), pltpu.VMEM((rc + halo, LANES), F32),
                        pltpu.VMEM((rl, LANES), F32), pltpu.VMEM((rc, LANES), F32),
                        pltpu.SemaphoreType.DMA((2,))],
        compiler_params=_params(("arbitrary",), 56),
        name="rglru",
    )(xr_l, xr_c, gr_l, gr_c, per_block(conv_w), per_block(conv_b.reshape(1, -1)), wg,
      per_block(bg[:, :, None, :]), per_block(lam))


def _lru_gate_weights(wa, ba, wx, bx, half):
    nblk = wa.shape[1]
    lw = nblk * LRU_BLOCK
    per = half // LRU_BLOCK

    def dense(w):
        w = w.reshape(2, nblk // per, per, LRU_BLOCK, LRU_BLOCK)
        eye = jnp.eye(per, dtype=w.dtype)
        full = w[:, :, :, :, None, :] * eye[None, None, :, None, :, None]
        return full.reshape(2, nblk // per, half, half)

    wg = (0.5 * jnp.stack([dense(wa), dense(wx)], axis=1)).astype(BF16)
    bg = 0.5 * jnp.stack([ba, bx], axis=1).astype(F32)
    return wg, bg


def _out_proj_kernel(oa_ref, ol_ref, h_ref, g1_ref, w_ref, o_ref, *, aw):
    nb, tt, d = h_ref.shape
    o = jnp.dot(oa_ref[...].reshape(nb * tt, aw), w_ref[0:aw, :], preferred_element_type=F32)
    ol = _from_time_major(ol_ref, nb, tt).astype(BF16)
    o = o + jnp.dot(ol, w_ref[aw:, :], preferred_element_type=F32)
    o_ref[...] = h_ref[...] + g1_ref[...] * o.reshape(nb, tt, d)


def _out_proj_call(oa, ol, h, bmods, layer, w_bf):
    nb, t, d = h.shape
    aw = oa.shape[2]
    ncb = ol.shape[0]
    tt = min(TIME_TILE, t)
    bt = lambda w: pl.BlockSpec((nb, tt, w), lambda j: (0, j, 0))
    return pl.pallas_call(
        functools.partial(_out_proj_kernel, aw=aw),
        grid=(t // tt,),
        in_specs=[bt(aw), pl.BlockSpec((ncb, tt * SUBLANES, LANES), lambda j: (0, j, 0)), bt(d),
                  _batch_mod_spec(nb, d, layer, 2), pl.BlockSpec(w_bf.shape, lambda j: (0, 0))],
        out_specs=bt(d),
        out_shape=jax.ShapeDtypeStruct((nb, t, d), F32),
        compiler_params=_params(("arbitrary",), 32),
        name="out_proj",
    )(oa, ol, h, bmods, w_bf)


def _swiglu_step(a_bf, w1_ref, w3_ref, w2_ref, acc_ref):
    g = jnp.dot(a_bf, w1_ref[...].astype(BF16), preferred_element_type=F32)
    u = jnp.dot(a_bf, w3_ref[...].astype(BF16), preferred_element_type=F32)
    hmid = (g * jax.nn.sigmoid(g) * u).astype(BF16)
    acc_ref[...] += jnp.dot(hmid, w2_ref[...].astype(BF16), preferred_element_type=F32)


def _ffn_kernel(x_ref, g_ref, sh_ref, sc_ref, g2_ref, w1_ref, w3_ref, w2_ref, o_ref, a_ref, acc_ref):
    c = pl.program_id(1)

    @pl.when(c == 0)
    def _():
        a_ref[...] = _rms_mod(x_ref[...], g_ref[...], sh_ref[...], sc_ref[...]).astype(BF16)
        acc_ref[...] = jnp.zeros_like(acc_ref)

    _swiglu_step(a_ref[...], w1_ref, w3_ref, w2_ref, acc_ref)

    @pl.when(c == pl.num_programs(1) - 1)
    def _():
        o_ref[...] = x_ref[...] + g2_ref[...] * acc_ref[...]


def _ffn_call(h, g, mods, layer, w1, w3, w2, rows_per_mod, fixed_row):
    r, d = h.shape
    dff = w1.shape[1]
    tm = min(ROW_TILE_FFN, r)
    ms = functools.partial(_mod_spec, d, layer=layer, rows_per_mod=rows_per_mod, tm=tm, fixed_row=fixed_row)
    return pl.pallas_call(
        _ffn_kernel,
        grid=(r // tm, dff // FF_CHUNK),
        in_specs=[pl.BlockSpec((tm, d), lambda i, c: (i, 0)),
                  pl.BlockSpec((1, d), lambda i, c: (0, 0)),
                  ms(col=3), ms(col=4), ms(col=5),
                  pl.BlockSpec((d, FF_CHUNK), lambda i, c: (0, c)),
                  pl.BlockSpec((d, FF_CHUNK), lambda i, c: (0, c)),
                  pl.BlockSpec((FF_CHUNK, d), lambda i, c: (c, 0))],
        out_specs=pl.BlockSpec((tm, d), lambda i, c: (i, 0)),
        out_shape=jax.ShapeDtypeStruct((r, d), F32),
        scratch_shapes=[pltpu.VMEM((tm, d), BF16), pltpu.VMEM((tm, d), F32)],
        compiler_params=_params(("arbitrary", "arbitrary"), 56),
        name="ffn_dense",
    )(h, g, mods, mods, mods, w1, w3, w2)


def _to_row_tiles(x, o_ref):
    t, d = x.shape
    ns = d // LANES
    for s in range(ns):
        o_ref[pl.ds(s, t, stride=ns), :] = x[:, s * LANES:(s + 1) * LANES]


def _from_row_tiles(x_ref, idx, t, ns):
    return jnp.concatenate([x_ref[idx + (pl.ds(s, t, stride=ns), slice(None))] for s in range(ns)], axis=1)


def _gffn_kernel(te_ref, nu_ref, x_ref, w1_ref, w3_ref, w2_ref, o_ref, a_ref, acc_ref):
    i = pl.program_id(0)
    c = pl.program_id(1)
    tm, d = a_ref.shape

    @pl.when(i < nu_ref[0])
    def _():
        @pl.when(c == 0)
        def _():
            a_ref[...] = _from_row_tiles(x_ref, (), tm, d // LANES).astype(BF16)
            acc_ref[...] = jnp.zeros_like(acc_ref)

        _swiglu_step(a_ref[...], w1_ref, w3_ref, w2_ref, acc_ref)

        @pl.when(c == pl.num_programs(1) - 1)
        def _():
            _to_row_tiles(acc_ref[...], o_ref)

    @pl.when((i >= nu_ref[0]) & (c == pl.num_programs(1) - 1))
    def _():
        o_ref[...] = jnp.zeros_like(o_ref)


def _gffn_call(tile_expert, n_used, xg, w1, w3, w2):
    d = w1.shape[1]
    ns = d // LANES
    rg = xg.shape[0] // ns
    dff = w1.shape[2]
    tm = ROW_TILE_FFN
    nc = dff // FF_CHUNK

    def tile(i, nu):
        return jnp.maximum(jnp.minimum(i, nu[0] - 1), 0)

    def chunk(i, c, nu):
        return jnp.where(i < nu[0], c, nc - 1)

    grid_spec = pltpu.PrefetchScalarGridSpec(
        num_scalar_prefetch=2,
        grid=(rg // tm, nc),
        in_specs=[pl.BlockSpec((tm * ns, LANES), lambda i, c, te, nu: (tile(i, nu), 0)),
                  pl.BlockSpec((None, d, FF_CHUNK), lambda i, c, te, nu: (te[tile(i, nu)], 0, chunk(i, c, nu))),
                  pl.BlockSpec((None, d, FF_CHUNK), lambda i, c, te, nu: (te[tile(i, nu)], 0, chunk(i, c, nu))),
                  pl.BlockSpec((None, FF_CHUNK, d), lambda i, c, te, nu: (te[tile(i, nu)], chunk(i, c, nu), 0))],
        out_specs=pl.BlockSpec((tm * ns, LANES), lambda i, c, te, nu: (i, 0)),
        scratch_shapes=[pltpu.VMEM((tm, d), BF16), pltpu.VMEM((tm, d), F32)])
    return pl.pallas_call(
        _gffn_kernel,
        grid_spec=grid_spec,
        out_shape=jax.ShapeDtypeStruct((rg * ns, LANES), F32),
        compiler_params=_params(("arbitrary", "arbitrary"), 56),
        name="ffn_grouped",
    )(tile_expert, n_used, xg, w1, w3, w2)


def _route_kernel(x_ref, g_ref, sh_ref, sc_ref, wr_ref, br_ref, a_ref, tab_ref, gcol_ref, cnt_ref, carry_ref):
    i = pl.program_id(0)
    tm = x_ref.shape[0]

    @pl.when(i == 0)
    def _():
        carry_ref[...] = jnp.zeros_like(carry_ref)

    a = _rms_mod(x_ref[...], g_ref[...], sh_ref[...], sc_ref[...])
    _to_row_tiles(a, a_ref)
    lane = lax.broadcasted_iota(I32, (tm, LANES), 1).astype(F32)
    logits = jnp.dot(a, wr_ref[...], preferred_element_type=F32, precision=lax.Precision.HIGHEST) + br_ref[...]
    logits = jnp.where(lane < N_EXPERTS, logits, NEG_INF)
    m1 = jnp.max(logits, axis=-1, keepdims=True)
    i1 = jnp.min(jnp.where(logits == m1, lane, float(LANES)), axis=-1, keepdims=True)
    rest = jnp.where(lane == i1, 2.0 * NEG_INF, logits)
    m2 = jnp.max(rest, axis=-1, keepdims=True)
    i2 = jnp.min(jnp.where(rest == m2, lane, float(LANES)), axis=-1, keepdims=True)
    e = jnp.exp(m2 - m1)
    w1 = 1.0 / (1.0 + e)
    w2 = e * w1
    sel1 = lane == i1
    sel2 = lane == i2
    onehot = jnp.where(sel1 | sel2, 1.0, 0.0)
    row = lax.broadcasted_iota(I32, (tm, tm), 0)
    col = lax.broadcasted_iota(I32, (tm, tm), 1)
    before = jnp.where(col < row, 1.0, 0.0).astype(BF16)
    cum = jnp.dot(before, onehot.astype(BF16), preferred_element_type=F32) + carry_ref[...]
    r1 = jnp.sum(jnp.where(sel1, cum, 0.0), axis=-1, keepdims=True)
    r2 = jnp.sum(jnp.where(sel2, cum, 0.0), axis=-1, keepdims=True)
    carry_ref[...] += jnp.sum(onehot, axis=0, keepdims=True)
    cnt_ref[...] = carry_ref[...]
    cols = (jnp.where(lane == 0, i1, 0.0) + jnp.where(lane == 1, i2, 0.0)
            + jnp.where(lane == 2, r1, 0.0) + jnp.where(lane == 3, r2, 0.0))
    tab_ref[...] = cols.T[0:SUBLANES, :]
    gcol_ref[...] = (jnp.where(lane == 0, w1, 0.0) + jnp.where(lane == 1, w2, 0.0))[:, 0:SUBLANES]


def _route_call(h, g, mods, layer, wr, br, rows_per_mod):
    n, d = h.shape
    tm = min(ROW_TILE, n)
    ms = functools.partial(_mod_spec, d, layer=layer, rows_per_mod=rows_per_mod, tm=tm, fixed_row=None)
    return pl.pallas_call(
        _route_kernel,
        grid=(n // tm,),
        in_specs=[pl.BlockSpec((tm, d), lambda i: (i, 0)),
                  pl.BlockSpec((1, d), lambda i: (0, 0)), ms(col=3), ms(col=4),
                  pl.BlockSpec((d, LANES), lambda i: (0, 0)),
                  pl.BlockSpec((1, LANES), lambda i: (0, 0))],
        out_specs=[pl.BlockSpec((tm * (d // LANES), LANES), lambda i: (i, 0)),
                   pl.BlockSpec((SUBLANES, tm), lambda i: (0, i)),
                   pl.BlockSpec((tm, SUBLANES), lambda i: (i, 0)),
                   pl.BlockSpec((1, LANES), lambda i: (0, 0))],
        out_shape=[jax.ShapeDtypeStruct((n * (d // LANES), LANES), F32),
                   jax.ShapeDtypeStruct((SUBLANES, n), F32),
                   jax.ShapeDtypeStruct((n, SUBLANES), F32),
                   jax.ShapeDtypeStruct((1, LANES), F32)],
        scratch_shapes=[pltpu.VMEM((1, LANES), F32)],
        compiler_params=_params(("arbitrary",), 40),
        name="moe_route",
    )(h, g, mods, mods, wr, br)


DMA_UNROLL = 4


def _dispatch_kernel(off_ref, tab_ref, a_ref, xg_in, xg_out, pos_ref, sem, *, ns):
    del xg_in
    tm = tab_ref.shape[1]

    def row_copy(src, dst):
        return pltpu.make_async_copy(a_ref.at[pl.ds(pl.multiple_of(src * ns, ns), ns)],
                                     xg_out.at[pl.ds(pl.multiple_of(dst * ns, ns), ns)], sem)

    def issue(j, carry):
        for u in range(DMA_UNROLL):
            t = j * DMA_UNROLL + u
            for k in range(2):
                p = off_ref[tab_ref[k, t]] + tab_ref[2 + k, t]
                pos_ref[k, t] = p
                row_copy(t, p).start()
        return carry

    lax.fori_loop(0, tm // DMA_UNROLL, issue, 0)

    def drain(j, carry):
        for _ in range(2 * DMA_UNROLL):
            row_copy(0, 0).wait()
        return carry

    lax.fori_loop(0, tm // DMA_UNROLL, drain, 0)


def _dispatch_call(off, tab_i, a, rows_g, ns):
    n = a.shape[0] // ns
    tm = min(ROW_TILE, n)
    grid_spec = pltpu.PrefetchScalarGridSpec(
        num_scalar_prefetch=1,
        grid=(n // tm,),
        in_specs=[pl.BlockSpec((4, tm), lambda i, off: (0, i), memory_space=pltpu.SMEM),
                  pl.BlockSpec((tm * ns, LANES), lambda i, off: (i, 0)),
                  pl.BlockSpec(memory_space=pl.ANY)],
        out_specs=[pl.BlockSpec(memory_space=pl.ANY),
                   pl.BlockSpec((2, tm), lambda i, off: (0, i), memory_space=pltpu.SMEM)],
        scratch_shapes=[pltpu.SemaphoreType.DMA(())])
    xg0 = jnp.zeros((rows_g * ns, LANES), a.dtype)
    return pl.pallas_call(
        functools.partial(_dispatch_kernel, ns=ns),
        grid_spec=grid_spec,
        out_shape=[jax.ShapeDtypeStruct((rows_g * ns, LANES), a.dtype), jax.ShapeDtypeStruct((2, n), I32)],
        input_output_aliases={3: 0},
        compiler_params=_params(("arbitrary",), 16),
        name="moe_dispatch",
    )(off, tab_i, a, xg0)


def _combine_kernel(pos_ref, posn_ref, y_hbm, h_ref, gcol_ref, g2_ref, fg_ref, o_ref, buf_ref, sem):
    i = pl.program_id(0)
    n = pl.num_programs(0)
    tm, d = h_ref.shape
    ns = d // LANES

    def row_copy(src, slot, k, t):
        return pltpu.make_async_copy(y_hbm.at[pl.ds(pl.multiple_of(src * ns, ns), ns)],
                                     buf_ref.at[slot, k, pl.ds(pl.multiple_of(t * ns, ns), ns)], sem.at[slot])

    def issue(p_ref, slot):
        def body(j, carry):
            for u in range(DMA_UNROLL):
                t = j * DMA_UNROLL + u
                for k in range(2):
                    row_copy(p_ref[k, t], slot, k, t).start()
            return carry
        lax.fori_loop(0, tm // DMA_UNROLL, body, 0)

    @pl.when(i == 0)
    def _():
        issue(pos_ref, 0)

    @pl.when(i + 1 < n)
    def _():
        issue(posn_ref, (i + 1) % 2)

    slot = i % 2

    def drain(j, carry):
        for _ in range(2 * DMA_UNROLL):
            row_copy(0, slot, 0, 0).wait()
        return carry

    lax.fori_loop(0, tm // DMA_UNROLL, drain, 0)
    y = (gcol_ref[:, 0:1] * _from_row_tiles(buf_ref, (slot, 0), tm, ns)
         + gcol_ref[:, 1:2] * _from_row_tiles(buf_ref, (slot, 1), tm, ns))
    hn = h_ref[...] + g2_ref[...] * y
    o_ref[...] = hn * lax.rsqrt(jnp.mean(hn * hn, axis=-1, keepdims=True) + EPS) * fg_ref[...]


def _combine_call(pos, y, h, gcol, mods, layer, final_g, rows_per_mod):
    n, d = h.shape
    tm = min(GATHER_TILE, n)
    nt = n // tm
    ms = functools.partial(_mod_spec, d, layer=layer, rows_per_mod=rows_per_mod, tm=tm, fixed_row=None)
    return pl.pallas_call(
        _combine_kernel,
        grid=(nt,),
        in_specs=[pl.BlockSpec((2, tm), lambda i: (0, i), memory_space=pltpu.SMEM),
                  pl.BlockSpec((2, tm), lambda i: (0, jnp.minimum(i + 1, nt - 1)), memory_space=pltpu.SMEM),
                  pl.BlockSpec(memory_space=pl.ANY),
                  pl.BlockSpec((tm, d), lambda i: (i, 0)),
                  pl.BlockSpec((tm, SUBLANES), lambda i: (i, 0)),
                  ms(col=5),
                  pl.BlockSpec((1, d), lambda i: (0, 0))],
        out_specs=pl.BlockSpec((tm, d), lambda i: (i, 0)),
        out_shape=jax.ShapeDtypeStruct((n, d), F32),
        scratch_shapes=[pltpu.VMEM((2, 2, tm * (d // LANES), LANES), F32), pltpu.SemaphoreType.DMA((2,))],
        compiler_params=_params(("arbitrary",), 24),
        name="moe_combine",
    )(pos, pos, y, h, gcol, mods, final_g)


def _moe_layer(h, g, mods, layer, router, router_b, w1, w3, w2, final_g, rows_per_mod):
    n, d = h.shape
    wr = jnp.zeros((d, LANES), F32).at[:, :N_EXPERTS].set(router)
    br = jnp.zeros((1, LANES), F32).at[0, :N_EXPERTS].set(router_b)
    a, tab, gcol, cnt = _route_call(h, g, mods, layer, wr, br, rows_per_mod)
    tm = ROW_TILE_FFN
    n_tiles = (2 * n) // tm + N_EXPERTS
    counts = cnt[0, :N_EXPERTS].astype(I32)
    tiles = (counts + tm - 1) // tm
    ends = jnp.cumsum(tiles)
    off = ((ends - tiles) * tm).astype(I32)
    tile_expert = jnp.minimum(jnp.sum(jnp.arange(n_tiles, dtype=I32)[:, None] >= ends[None, :], axis=1),
                              N_EXPERTS - 1).astype(I32)
    n_used = ends[-1:].astype(I32)
    assert d // LANES == SUBLANES, "one token must fill one (8, 128) tile of the row-tile layout"
    xg, pos = _dispatch_call(off, tab[0:4].astype(I32), a, n_tiles * tm, d // LANES)
    y = _gffn_call(tile_expert, n_used, xg, w1, w3, w2)
    return _combine_call(pos, y, h, gcol, mods, layer, final_g, rows_per_mod)


def kernel(x, c, ctx, c_ctx, ada_w, ada_b, mix_norm_g, ffn_norm_g, w_in, w_out, na_rpb, conv_w, conv_b, lru_wa, lru_ba, lru_wx, lru_bx, lru_lam, ffn_w1, ffn_w3, ffn_w2, moe_router, moe_router_b, moe_w1, moe_w3, moe_w2, final_g):
    batch, seq, d = x.shape
    ctx_len = ctx.shape[1]
    depth = ada_w.shape[0]
    lw = conv_w.shape[-1]
    aw = (w_in.shape[-1] - 2 * lw) // 3
    rows = seq // GRID_W
    ctx_row = batch

    mod_rows = -(-(batch + 1) // SUBLANES) * SUBLANES
    c_all = jnp.zeros((mod_rows, d), F32).at[:batch].set(c).at[ctx_row].set(c_ctx)
    mods = _ada_call(c_all, ada_w, ada_b).reshape(depth, mod_rows, 1, 6 * d)

    bmods_lat = mods[:, :batch]
    bmods_ctx = jnp.broadcast_to(mods[:, ctx_row:ctx_row + 1], bmods_lat.shape)
    flat = lambda a: a.reshape(a.shape[0] * a.shape[1], a.shape[2])
    h_lat, h_ctx = x, ctx
    out = None
    for i in range(depth):
        last = i == depth - 1
        w_in_bf = w_in[i].astype(BF16)
        w_out_bf = w_out[i].astype(BF16)
        g_mix = mix_norm_g[i].reshape(1, d)
        g_ffn = ffn_norm_g[i].reshape(1, d)
        q_l, k_l, v_l, xr_l, gr_l = _in_proj_call(h_lat, g_mix, bmods_lat, i, w_in_bf, aw, lw)
        q_c, k_c, v_c, xr_c, gr_c = _in_proj_call(h_ctx, g_mix, bmods_ctx, i, w_in_bf, aw, lw)
        oa_l = _na_call(flat(q_l), flat(k_l), flat(v_l), flat(k_c), flat(v_c), _pad_rpb(na_rpb[i]), batch)
        wg, bg = _lru_gate_weights(lru_wa[i], lru_ba[i], lru_wx[i], lru_bx[i], LANES)
        ol_l, ol_c = _lru_call(xr_l, gr_l, xr_c, gr_c, conv_w[i], conv_b[i], wg, bg, lru_lam[i])
        h_lat = _out_proj_call(oa_l.reshape(batch, seq, aw), ol_l, h_lat, bmods_lat, i, w_out_bf)
        if not last:
            oa_c = _ctx_attn_call(flat(q_c), flat(k_c), flat(v_c), batch)
            h_ctx = _out_proj_call(oa_c.reshape(batch, ctx_len, aw), ol_c, h_ctx, bmods_ctx, i, w_out_bf)
        j = i // 2
        if i % 2 == 0:
            h_lat = _ffn_call(flat(h_lat), g_ffn, mods, i, ffn_w1[j], ffn_w3[j], ffn_w2[j],
                              seq, None).reshape(batch, seq, d)
            if not last:
                h_ctx = _ffn_call(flat(h_ctx), g_ffn, mods, i, ffn_w1[j], ffn_w3[j], ffn_w2[j],
                                  None, ctx_row).reshape(batch, ctx_len, d)
        else:
            assert last, "the routed layer is fused with the final norm"
            out = _moe_layer(flat(h_lat), g_ffn, mods, i, moe_router[j], moe_router_b[j],
                             moe_w1[j], moe_w3[j], moe_w2[j], final_g.reshape(1, d), seq)
    return out.reshape(batch, seq, d)
```

```python
import functools

import jax
import jax.numpy as jnp
from jax import lax
from jax.experimental import pallas as pl
from jax.experimental.pallas import tpu as pltpu

F32 = jnp.float32
BF16 = jnp.bfloat16
I32 = jnp.int32

GRID_W = 64
HEAD_DIM = 64
N_HEADS = 8
NA_KH = 8
NA_KW = 16
LRU_BLOCK = 64
LRU_C = 8.0
CONV_W = 4
N_EXPERTS = 8
EPS = 1e-6
NEG_INF = -1e30
LOG2E = 1.4426950408889634
LANES = 128
SUBLANES = 8
VMEM_BYTES = 64 * 1024 * 1024

FF_CHUNK = 512
ROW_TILE_FFN = 1024
ROW_TILE = 512
GATHER_TILE = 256


def _params(semantics, vmem_mb):
    return pltpu.CompilerParams(dimension_semantics=semantics,
                                vmem_limit_bytes=min(vmem_mb * 1024 * 1024, VMEM_BYTES - 4 * 1024 * 1024))


def _rms_mod(x, g, sh, sc):
    y = x * lax.rsqrt(jnp.mean(x * x, axis=-1, keepdims=True) + EPS)
    return (y * g) * (1.0 + sc) + sh


def _ada_kernel(c_ref, w_ref, b_ref, o_ref):
    c = c_ref[...]
    s = c * jax.nn.sigmoid(c)
    o_ref[...] = jnp.dot(s, w_ref[...], preferred_element_type=F32,
                         precision=lax.Precision.HIGHEST) + b_ref[...]


def _ada_call(c_all, ada_w, ada_b):
    depth, d, n = ada_w.shape
    rows = c_all.shape[0]
    tn = 1024
    return pl.pallas_call(
        _ada_kernel,
        grid=(depth, n // tn),
        in_specs=[pl.BlockSpec((rows, d), lambda l, j: (0, 0)),
                  pl.BlockSpec((None, d, tn), lambda l, j: (l, 0, j)),
                  pl.BlockSpec((None, 1, tn), lambda l, j: (l, 0, j))],
        out_specs=pl.BlockSpec((None, rows, tn), lambda l, j: (l, 0, j)),
        out_shape=jax.ShapeDtypeStruct((depth, rows, n), F32),
        compiler_params=_params(("arbitrary", "arbitrary"), 24),
        name="ada_mod",
    )(c_all, ada_w, ada_b.reshape(depth, 1, n))


TIME_TILE = 64


def _to_time_major(r, o_ref, nb, tt):
    for c in range(o_ref.shape[0]):
        for b in range(nb):
            o_ref[c, pl.ds(b, tt, stride=SUBLANES), :] = r[b * tt:(b + 1) * tt, c * LANES:(c + 1) * LANES]


def _from_time_major(x_ref, nb, tt):
    return jnp.concatenate(
        [jnp.concatenate([x_ref[c, pl.ds(b, tt, stride=SUBLANES), :] for c in range(x_ref.shape[0])], axis=1)
         for b in range(nb)], axis=0)


def _in_proj_kernel(x_ref, g_ref, sh_ref, sc_ref, w_ref, q_ref, k_ref, v_ref, xr_ref, gr_ref, *, aw, lw):
    nb, tt, d = x_ref.shape
    a = _rms_mod(x_ref[...], g_ref[...], sh_ref[...], sc_ref[...]).astype(BF16).reshape(nb * tt, d)
    r = jnp.dot(a, w_ref[...], preferred_element_type=F32)
    q_ref[...] = (r[:, 0:aw] * (LOG2E * HEAD_DIM ** -0.5)).astype(BF16).reshape(nb, tt, aw)
    k_ref[...] = r[:, aw:2 * aw].astype(BF16).reshape(nb, tt, aw)
    v_ref[...] = r[:, 2 * aw:3 * aw].astype(BF16).reshape(nb, tt, aw)
    _to_time_major(r[:, 3 * aw:3 * aw + lw], xr_ref, nb, tt)
    _to_time_major(r[:, 3 * aw + lw:3 * aw + 2 * lw], gr_ref, nb, tt)


def _mod_spec(d, col, layer, rows_per_mod, tm, fixed_row):
    if fixed_row is None:
        assert rows_per_mod % tm == 0, "a row tile must not straddle two batch elements"
        return pl.BlockSpec((None, None, 1, d), lambda i, *_: (layer, (i * tm) // rows_per_mod, 0, col))
    return pl.BlockSpec((None, None, 1, d), lambda i, *_: (layer, fixed_row, 0, col))


def _batch_mod_spec(nb, d, layer, col):
    return pl.BlockSpec((None, nb, 1, d), lambda j: (layer, 0, 0, col))


def _in_proj_call(x, g, bmods, layer, w_bf, aw, lw):
    nb, t, d = x.shape
    assert nb == SUBLANES, "time-major rows put the batch on the sublanes"
    tt = min(TIME_TILE, t)
    ncol = w_bf.shape[1]
    ncb = lw // LANES
    bt = lambda w: pl.BlockSpec((nb, tt, w), lambda j: (0, j, 0))
    tm = pl.BlockSpec((ncb, tt * SUBLANES, LANES), lambda j: (0, j, 0))
    return pl.pallas_call(
        functools.partial(_in_proj_kernel, aw=aw, lw=lw),
        grid=(t // tt,),
        in_specs=[bt(d), pl.BlockSpec((1, d), lambda j: (0, 0)),
                  _batch_mod_spec(nb, d, layer, 0), _batch_mod_spec(nb, d, layer, 1),
                  pl.BlockSpec((d, ncol), lambda j: (0, 0))],
        out_specs=[bt(aw), bt(aw), bt(aw), tm, tm],
        out_shape=[jax.ShapeDtypeStruct((nb, t, aw), BF16)] * 3
        + [jax.ShapeDtypeStruct((ncb, t * SUBLANES, LANES), F32)] * 2,
        compiler_params=_params(("arbitrary",), 40),
        name="in_proj",
    )(x, g, bmods, bmods, w_bf)


def _softmax_pv(parts):
    m = None
    for s, _ in parts:
        mi = jnp.max(s, axis=-1, keepdims=True)
        m = mi if m is None else jnp.maximum(m, mi)
    l = None
    acc = None
    for s, v in parts:
        p = jnp.exp2(s - m)
        li = jnp.sum(p, axis=-1, keepdims=True)
        ai = jnp.dot(p.astype(BF16), v, preferred_element_type=F32)
        l = li if l is None else l + li
        acc = ai if acc is None else acc + ai
    return acc * (1.0 / l)


def _qk(q, k):
    return lax.dot_general(q, k, (((1,), (1,)), ((), ())), preferred_element_type=F32)


def _build_na_bias(rpb_ref, bias_ref):
    lane = lax.broadcasted_iota(I32, (GRID_W, LANES), 1)
    q = lax.broadcasted_iota(I32, (GRID_W, LANES), 0)
    kcol = lane & (GRID_W - 1)
    cs = jnp.clip(q - NA_KW // 2, 0, GRID_W - NA_KW)
    ok = (kcol >= cs) & (kcol < cs + NA_KW)
    low = lane < GRID_W
    for delta in range(NA_KH):
        for h in range(N_HEADS):
            for jp in range(NA_KH // 2):
                halves = []
                for j in (2 * jp, 2 * jp + 1):
                    dr = j - delta + NA_KH - 1
                    w = jnp.broadcast_to(rpb_ref[h, dr:dr + 1, :], (GRID_W, LANES))
                    base = (j % 2) * GRID_W - (NA_KW - 1)
                    halves.append(pltpu.roll(w, base % LANES, 1, stride=1, stride_axis=0))
                t = jnp.where(low, halves[0], halves[1])
                bias_ref[delta, h, :, jp * LANES:(jp + 1) * LANES] = jnp.where(ok, t * LOG2E, NEG_INF)


NA_ROWS_PER_STEP = 4


def _stack_pair(q2):
    first = lax.broadcasted_iota(I32, q2.shape, 1) < HEAD_DIM
    zero = jnp.zeros_like(q2)
    return jnp.concatenate([jnp.where(first, q2, zero), jnp.where(first, zero, q2)], axis=0)


def _unstack_pair(o2):
    nq = o2.shape[0] // 2
    first = lax.broadcasted_iota(I32, (nq, o2.shape[1]), 1) < HEAD_DIM
    return jnp.where(first, o2[:nq], o2[nq:])


def _na_kernel(q_ref, kl_ref, vl_ref, kc_ref, vc_ref, rpb_ref, o_ref, bias_ref, s_ref, p_ref, *, rows, rb):
    @pl.when((pl.program_id(0) == 0) & (pl.program_id(1) == 0))
    def _():
        _build_na_bias(rpb_ref, bias_ref)

    nwin = NA_KH * GRID_W
    npair = N_HEADS // 2
    for rho in range(rb):
        r = pl.program_id(1) * rb + rho
        rs = jnp.clip(r - NA_KH // 2, 0, rows - NA_KH)
        delta = r - rs
        k0 = pl.multiple_of(rs * GRID_W, GRID_W)
        qrows = slice(rho * GRID_W, (rho + 1) * GRID_W)
        for pair in range(npair):
            sl = slice(pair * 2 * HEAD_DIM, (pair + 1) * 2 * HEAD_DIM)
            keys = jnp.concatenate([kl_ref[pl.ds(k0, nwin), sl], kc_ref[:, sl]], axis=0)
            s = _qk(_stack_pair(q_ref[qrows, sl]), keys)
            for sub in range(2):
                h = pair * 2 + sub
                rr = slice(sub * GRID_W, (sub + 1) * GRID_W)
                s_ref[rho, h, :, 0:nwin] = s[rr, 0:nwin] + bias_ref[delta, h]
                s_ref[rho, h, :, nwin:] = s[rr, nwin:]
        s = s_ref[rho]
        p = jnp.exp2(s - jnp.max(s, axis=-1, keepdims=True))
        inv = 1.0 / jnp.sum(p, axis=-1, keepdims=True)
        p_ref[rho] = p.astype(BF16)
        for pair in range(npair):
            sl = slice(pair * 2 * HEAD_DIM, (pair + 1) * 2 * HEAD_DIM)
            vals = jnp.concatenate([vl_ref[pl.ds(k0, nwin), sl], vc_ref[:, sl]], axis=0)
            p2 = p_ref[rho, 2 * pair:2 * pair + 2].reshape(2 * GRID_W, p_ref.shape[-1])
            o2 = jnp.dot(p2, vals, preferred_element_type=F32) * inv[2 * pair:2 * pair + 2].reshape(2 * GRID_W, 1)
            o_ref[qrows, sl] = _unstack_pair(o2).astype(o_ref.dtype)


def _na_call(q, kl, vl, kc, vc, rpb, batch):
    s = q.shape[0] // batch
    l = kc.shape[0] // batch
    w = q.shape[1]
    rows = s // GRID_W
    assert rows >= NA_KH
    rb = NA_ROWS_PER_STEP if rows % NA_ROWS_PER_STEP == 0 else 1
    nkeys = NA_KH * GRID_W + l
    nrb = rows // rb
    return pl.pallas_call(
        functools.partial(_na_kernel, rows=rows, rb=rb),
        grid=(batch, nrb),
        in_specs=[pl.BlockSpec((rb * GRID_W, w), lambda b, r: (b * nrb + r, 0)),
                  pl.BlockSpec((s, w), lambda b, r: (b, 0)),
                  pl.BlockSpec((s, w), lambda b, r: (b, 0)),
                  pl.BlockSpec((l, w), lambda b, r: (b, 0)),
                  pl.BlockSpec((l, w), lambda b, r: (b, 0)),
                  pl.BlockSpec(rpb.shape, lambda b, r: (0, 0, 0))],
        out_specs=pl.BlockSpec((rb * GRID_W, w), lambda b, r: (b * nrb + r, 0)),
        out_shape=jax.ShapeDtypeStruct(q.shape, BF16),
        scratch_shapes=[pltpu.VMEM((NA_KH, N_HEADS, GRID_W, NA_KH * GRID_W), F32),
                        pltpu.VMEM((rb, N_HEADS, GRID_W, nkeys), F32),
                        pltpu.VMEM((rb, N_HEADS, GRID_W, nkeys), BF16)],
        compiler_params=_params(("arbitrary", "arbitrary"), 48),
        name="na_attention",
    )(q, kl, vl, kc, vc, rpb)


def _pad_rpb(rpb):
    h, nr, nc = rpb.shape
    return jnp.zeros((h, 2 * NA_KH, LANES), F32).at[:, :nr, :nc].set(rpb.astype(F32))


def _ctx_attn_kernel(q_ref, k_ref, v_ref, o_ref):
    rows = q_ref.shape[0]
    lane = lax.broadcasted_iota(I32, (rows, 2 * HEAD_DIM), 1)
    first = lane < HEAD_DIM
    for pair in range(N_HEADS // 2):
        sl = slice(pair * 2 * HEAD_DIM, (pair + 1) * 2 * HEAD_DIM)
        q2 = q_ref[:, sl]
        k = k_ref[:, sl]
        v = v_ref[:, sl]
        outs = []
        for sub in range(2):
            qh = jnp.where(first if sub == 0 else ~first, q2, jnp.zeros_like(q2))
            outs.append(_softmax_pv([(_qk(qh, k), v)]))
        o_ref[:, sl] = jnp.where(first, outs[0], outs[1]).astype(o_ref.dtype)


def _ctx_attn_call(q, k, v, batch):
    l = q.shape[0] // batch
    w = q.shape[1]
    spec = pl.BlockSpec((l, w), lambda b: (b, 0))
    return pl.pallas_call(
        _ctx_attn_kernel,
        grid=(batch,),
        in_specs=[spec, spec, spec],
        out_specs=spec,
        out_shape=jax.ShapeDtypeStruct(q.shape, BF16),
        compiler_params=_params(("arbitrary",), 24),
        name="ctx_attention",
    )(q, k, v)


def _softplus(x):
    return jnp.maximum(x, 0.0) + jnp.log1p(jnp.exp(-jnp.abs(x)))


LRU_TIME_CHUNK = 64


def _lru_kernel(xl_hbm, xc_hbm, gl_ref, gc_ref, cw_ref, cb_ref, wg_ref, bg_ref, lam_ref, ol_ref, oc_ref,
                padl_ref, padc_ref, ul_ref, uc_ref, sem, *, tchunk):
    c = pl.program_id(0)
    nb = SUBLANES
    halo_lo, halo_hi = nb, 2 * nb
    zero = jnp.zeros((nb, LANES), F32)

    def fetch(x_hbm, pad_ref, k):
        n = x_hbm.shape[1]
        return pltpu.make_async_copy(x_hbm.at[c], pad_ref.at[pl.ds(halo_lo, n)], sem.at[k])

    copies = [fetch(xc_hbm, padc_ref, 0), fetch(xl_hbm, padl_ref, 1)]
    for cp in copies:
        cp.start()
    for x_hbm, pad_ref in ((xc_hbm, padc_ref), (xl_hbm, padl_ref)):
        n = x_hbm.shape[1]
        pad_ref[0:halo_lo, :] = zero
        pad_ref[halo_lo + n:halo_lo + n + halo_hi, :] = jnp.zeros((halo_hi, LANES), F32)
    for cp in copies:
        cp.wait()

    la = [(-0.5 * LRU_C * LOG2E) * _softplus(-lam_ref[d:d + 1, :]) for d in range(2)]
    cb = cb_ref[...]
    cw = [cw_ref[j:j + 1, :] for j in range(CONV_W)]
    rows = tchunk * nb

    def segment(pad_ref, u_ref, g_ref, o_ref, d, h):
        n = o_ref.shape[0] // rows

        def body(i, h):
            r0 = pl.multiple_of((i if d == 0 else n - 1 - i) * rows, rows)
            if d == 0:
                u = cb
                for j in range(CONV_W):
                    u = u + pad_ref[pl.ds(r0 + j * nb, rows), :] * cw[j]
                u_ref[pl.ds(r0, rows), :] = u
            else:
                u = u_ref[pl.ds(r0, rows), :]
            ub = u.astype(BF16)
            ta = jnp.tanh(jnp.dot(ub, wg_ref[d, 0], preferred_element_type=F32) + bg_ref[d, 0])
            ti = jnp.tanh(jnp.dot(ub, wg_ref[d, 1], preferred_element_type=F32) + bg_ref[d, 1])
            a = jnp.exp2(la[d] * ta + la[d])
            z = 1.0 - a * a
            b = jnp.where(z > 0.0, z * lax.rsqrt(z), 0.0) * (0.5 * ti + 0.5) * u
            hs = [None] * tchunk
            for t in (range(tchunk) if d == 0 else reversed(range(tchunk))):
                h = a[t * nb:(t + 1) * nb] * h + b[t * nb:(t + 1) * nb]
                hs[t] = h
            hcat = jnp.concatenate(hs, axis=0)
            if d == 0:
                o_ref[pl.ds(r0, rows), :] = hcat
            else:
                y = o_ref[pl.ds(r0, rows), :] + hcat
                o_ref[pl.ds(r0, rows), :] = jax.nn.gelu(g_ref[pl.ds(r0, rows), :]) * y
            return h

        return lax.fori_loop(0, n, body, h)

    for d in range(2):
        h = segment(padc_ref, uc_ref, gc_ref, oc_ref, d, zero)
        segment(padl_ref, ul_ref, gl_ref, ol_ref, d, h)


def _lru_call(xr_l, gr_l, xr_c, gr_c, conv_w, conv_b, wg, bg, lam):
    ncb, rl, _ = xr_l.shape
    rc = xr_c.shape[1]
    tchunk = min(LRU_TIME_CHUNK, rl // SUBLANES, rc // SUBLANES)
    halo = 3 * SUBLANES
    lat = pl.BlockSpec((None, rl, LANES), lambda c: (c, 0, 0))
    ctx = pl.BlockSpec((None, rc, LANES), lambda c: (c, 0, 0))
    per_block = lambda a: jnp.moveaxis(a.reshape(a.shape[:-1] + (ncb, LANES)), -2, 0)
    return pl.pallas_call(
        functools.partial(_lru_kernel, tchunk=tchunk),
        grid=(ncb,),
        in_specs=[pl.BlockSpec(memory_space=pl.ANY), pl.BlockSpec(memory_space=pl.ANY),
                  pl.BlockSpec((None, rl, LANES), lambda c: (c, 0, 0), pipeline_mode=pl.Buffered(1)), ctx,
                  pl.BlockSpec((None, CONV_W, LANES), lambda c: (c, 0, 0)),
                  pl.BlockSpec((None, 1, LANES), lambda c: (c, 0, 0)),
                  pl.BlockSpec((2, 2, None, LANES, LANES), lambda c: (0, 0, c, 0, 0)),
                  pl.BlockSpec((None, 2, 2, 1, LANES), lambda c: (c, 0, 0, 0, 0)),
                  pl.BlockSpec((None, 2, LANES), lambda c: (c, 0, 0))],
        out_specs=[lat, ctx],
        out_shape=[jax.ShapeDtypeStruct(xr_l.shape, F32), jax.ShapeDtypeStruct(xr_c.shape, F32)],
        scratch_shapes=[pltpu.VMEM((rl + halo, LANES), F32), pltpu.VMEM((rc + halo, LANES), F32),
                        pltpu.VMEM((rl, LANES), F32), pltpu.VMEM((rc, LANES), F32),
                        pltpu.SemaphoreType.DMA((2,))],
        compiler_params=_params(("arbitrary",), 56),
        name="rglru",
    )(xr_l, xr_c, gr_l, gr_c, per_block(conv_w), per_block(conv_b.reshape(1, -1)), wg,
      per_block(bg[:, :, None, :]), per_block(lam))


def _lru_gate_weights(wa, ba, wx, bx, half):
    nblk = wa.shape[1]
    lw = nblk * LRU_BLOCK
    per = half // LRU_BLOCK

    def dense(w):
        w = w.reshape(2, nblk // per, per, LRU_BLOCK, LRU_BLOCK)
        eye = jnp.eye(per, dtype=w.dtype)
        full = w[:, :, :, :, None, :] * eye[None, None, :, None, :, None]
        return full.reshape(2, nblk // per, half, half)

    wg = (0.5 * jnp.stack([dense(wa), dense(wx)], axis=1)).astype(BF16)
    bg = 0.5 * jnp.stack([ba, bx], axis=1).astype(F32)
    return wg, bg


def _out_proj_kernel(oa_ref, ol_ref, h_ref, g1_ref, w_ref, o_ref, *, aw):
    o_ref[...] = _mix_proj(oa_ref, ol_ref, h_ref, g1_ref, w_ref, aw)


def _out_proj_call(oa, ol, h, bmods, layer, w_bf):
    nb, t, d = h.shape
    aw = oa.shape[2]
    ncb = ol.shape[0]
    tt = min(TIME_TILE, t)
    bt = lambda w: pl.BlockSpec((nb, tt, w), lambda j: (0, j, 0))
    return pl.pallas_call(
        functools.partial(_out_proj_kernel, aw=aw),
        grid=(t // tt,),
        in_specs=[bt(aw), pl.BlockSpec((ncb, tt * SUBLANES, LANES), lambda j: (0, j, 0)), bt(d),
                  _batch_mod_spec(nb, d, layer, 2), pl.BlockSpec(w_bf.shape, lambda j: (0, 0))],
        out_specs=bt(d),
        out_shape=jax.ShapeDtypeStruct((nb, t, d), F32),
        compiler_params=_params(("arbitrary",), 32),
        name="out_proj",
    )(oa, ol, h, bmods, w_bf)


def _swiglu_step(a_bf, w1_ref, w3_ref, w2_ref, acc_ref):
    g = jnp.dot(a_bf, w1_ref[...].astype(BF16), preferred_element_type=F32)
    u = jnp.dot(a_bf, w3_ref[...].astype(BF16), preferred_element_type=F32)
    hmid = (g * jax.nn.sigmoid(g) * u).astype(BF16)
    acc_ref[...] += jnp.dot(hmid, w2_ref[...].astype(BF16), preferred_element_type=F32)


def _mix_proj(oa_ref, ol_ref, h_ref, g1_ref, wo_ref, aw):
    nb, tt, d = h_ref.shape
    o = jnp.dot(oa_ref[...].reshape(nb * tt, aw), wo_ref[0:aw, :], preferred_element_type=F32)
    ol = _from_time_major(ol_ref, nb, tt).astype(BF16)
    o = o + jnp.dot(ol, wo_ref[aw:, :], preferred_element_type=F32)
    return h_ref[...] + g1_ref[...] * o.reshape(nb, tt, d)


def _proj_ffn_kernel(oa_ref, ol_ref, h_ref, g1_ref, wo_ref, g_ref, sh_ref, sc_ref, g2_ref,
                     w1_ref, w3_ref, w2_ref, o_ref, a_ref, acc_ref, *, aw):
    c = pl.program_id(1)
    nb, tt, d = h_ref.shape

    @pl.when(c == 0)
    def _():
        h1 = _mix_proj(oa_ref, ol_ref, h_ref, g1_ref, wo_ref, aw)
        o_ref[...] = h1
        a_ref[...] = _rms_mod(h1, g_ref[...], sh_ref[...], sc_ref[...]).astype(BF16).reshape(nb * tt, d)
        acc_ref[...] = jnp.zeros_like(acc_ref)

    _swiglu_step(a_ref[...], w1_ref, w3_ref, w2_ref, acc_ref)

    @pl.when(c == pl.num_programs(1) - 1)
    def _():
        o_ref[...] += g2_ref[...] * acc_ref[...].reshape(nb, tt, d)


def _proj_ffn_call(oa, ol, h, g, bmods, layer, wo_bf, w1, w3, w2):
    nb, t, d = h.shape
    aw = oa.shape[2]
    ncb = ol.shape[0]
    dff = w1.shape[1]
    tt = min(ROW_TILE_FFN // nb, t)
    bt = lambda w: pl.BlockSpec((nb, tt, w), lambda j, c: (0, j, 0))
    bm = lambda col: pl.BlockSpec((None, nb, 1, d), lambda j, c: (layer, 0, 0, col))
    return pl.pallas_call(
        functools.partial(_proj_ffn_kernel, aw=aw),
        grid=(t // tt, dff // FF_CHUNK),
        in_specs=[bt(aw), pl.BlockSpec((ncb, tt * SUBLANES, LANES), lambda j, c: (0, j, 0)), bt(d), bm(2),
                  pl.BlockSpec(wo_bf.shape, lambda j, c: (0, 0), pipeline_mode=pl.Buffered(1)),
                  pl.BlockSpec((1, d), lambda j, c: (0, 0)), bm(3), bm(4), bm(5),
                  pl.BlockSpec((d, FF_CHUNK), lambda j, c: (0, c)),
                  pl.BlockSpec((d, FF_CHUNK), lambda j, c: (0, c)),
                  pl.BlockSpec((FF_CHUNK, d), lambda j, c: (c, 0))],
        out_specs=bt(d),
        out_shape=jax.ShapeDtypeStruct((nb, t, d), F32),
        scratch_shapes=[pltpu.VMEM((nb * tt, d), BF16), pltpu.VMEM((nb * tt, d), F32)],
        compiler_params=_params(("arbitrary", "arbitrary"), 58),
        name="proj_ffn",
    )(oa, ol, h, bmods, wo_bf, g, bmods, bmods, bmods, w1, w3, w2)


def _to_row_tiles(x, o_ref):
    t, d = x.shape
    ns = d // LANES
    for s in range(ns):
        o_ref[pl.ds(s, t, stride=ns), :] = x[:, s * LANES:(s + 1) * LANES]


def _from_row_tiles(x_ref, idx, t, ns):
    return jnp.concatenate([x_ref[idx + (pl.ds(s, t, stride=ns), slice(None))] for s in range(ns)], axis=1)


def _gffn_kernel(te_ref, nu_ref, x_ref, w1_ref, w3_ref, w2_ref, o_ref, a_ref, acc_ref):
    i = pl.program_id(0)
    c = pl.program_id(1)
    tm, d = a_ref.shape

    @pl.when(i < nu_ref[0])
    def _():
        @pl.when(c == 0)
        def _():
            a_ref[...] = _from_row_tiles(x_ref, (), tm, d // LANES).astype(BF16)
            acc_ref[...] = jnp.zeros_like(acc_ref)

        _swiglu_step(a_ref[...], w1_ref, w3_ref, w2_ref, acc_ref)

        @pl.when(c == pl.num_programs(1) - 1)
        def _():
            _to_row_tiles(acc_ref[...], o_ref)

    @pl.when((i >= nu_ref[0]) & (c == pl.num_programs(1) - 1))
    def _():
        o_ref[...] = jnp.zeros_like(o_ref)


def _gffn_call(tile_expert, n_used, xg, w1, w3, w2):
    d = w1.shape[1]
    ns = d // LANES
    rg = xg.shape[0] // ns
    dff = w1.shape[2]
    tm = ROW_TILE_FFN
    nc = dff // FF_CHUNK

    def tile(i, nu):
        return jnp.maximum(jnp.minimum(i, nu[0] - 1), 0)

    def chunk(i, c, nu):
        return jnp.where(i < nu[0], c, nc - 1)

    grid_spec = pltpu.PrefetchScalarGridSpec(
        num_scalar_prefetch=2,
        grid=(rg // tm, nc),
        in_specs=[pl.BlockSpec((tm * ns, LANES), lambda i, c, te, nu: (tile(i, nu), 0)),
                  pl.BlockSpec((None, d, FF_CHUNK), lambda i, c, te, nu: (te[tile(i, nu)], 0, chunk(i, c, nu))),
                  pl.BlockSpec((None, d, FF_CHUNK), lambda i, c, te, nu: (te[tile(i, nu)], 0, chunk(i, c, nu))),
                  pl.BlockSpec((None, FF_CHUNK, d), lambda i, c, te, nu: (te[tile(i, nu)], chunk(i, c, nu), 0))],
        out_specs=pl.BlockSpec((tm * ns, LANES), lambda i, c, te, nu: (i, 0)),
        scratch_shapes=[pltpu.VMEM((tm, d), BF16), pltpu.VMEM((tm, d), F32)])
    return pl.pallas_call(
        _gffn_kernel,
        grid_spec=grid_spec,
        out_shape=jax.ShapeDtypeStruct((rg * ns, LANES), F32),
        compiler_params=_params(("arbitrary", "arbitrary"), 56),
        name="ffn_grouped",
    )(tile_expert, n_used, xg, w1, w3, w2)


def _route_kernel(x_ref, g_ref, sh_ref, sc_ref, wr_ref, br_ref, a_ref, tab_ref, gcol_ref, cnt_ref, carry_ref):
    i = pl.program_id(0)
    tm = x_ref.shape[0]

    @pl.when(i == 0)
    def _():
        carry_ref[...] = jnp.zeros_like(carry_ref)

    a = _rms_mod(x_ref[...], g_ref[...], sh_ref[...], sc_ref[...])
    _to_row_tiles(a, a_ref)
    lane = lax.broadcasted_iota(I32, (tm, LANES), 1).astype(F32)
    logits = jnp.dot(a, wr_ref[...], preferred_element_type=F32, precision=lax.Precision.HIGHEST) + br_ref[...]
    logits = jnp.where(lane < N_EXPERTS, logits, NEG_INF)
    m1 = jnp.max(logits, axis=-1, keepdims=True)
    i1 = jnp.min(jnp.where(logits == m1, lane, float(LANES)), axis=-1, keepdims=True)
    rest = jnp.where(lane == i1, 2.0 * NEG_INF, logits)
    m2 = jnp.max(rest, axis=-1, keepdims=True)
    i2 = jnp.min(jnp.where(rest == m2, lane, float(LANES)), axis=-1, keepdims=True)
    e = jnp.exp(m2 - m1)
    w1 = 1.0 / (1.0 + e)
    w2 = e * w1
    sel1 = lane == i1
    sel2 = lane == i2
    onehot = jnp.where(sel1 | sel2, 1.0, 0.0)
    row = lax.broadcasted_iota(I32, (tm, tm), 0)
    col = lax.broadcasted_iota(I32, (tm, tm), 1)
    before = jnp.where(col < row, 1.0, 0.0).astype(BF16)
    cum = jnp.dot(before, onehot.astype(BF16), preferred_element_type=F32) + carry_ref[...]
    r1 = jnp.sum(jnp.where(sel1, cum, 0.0), axis=-1, keepdims=True)
    r2 = jnp.sum(jnp.where(sel2, cum, 0.0), axis=-1, keepdims=True)
    carry_ref[...] += jnp.sum(onehot, axis=0, keepdims=True)
    cnt_ref[...] = carry_ref[...]
    cols = (jnp.where(lane == 0, i1, 0.0) + jnp.where(lane == 1, i2, 0.0)
            + jnp.where(lane == 2, r1, 0.0) + jnp.where(lane == 3, r2, 0.0))
    tab_ref[...] = cols.T[0:SUBLANES, :]
    gcol_ref[...] = (jnp.where(lane == 0, w1, 0.0) + jnp.where(lane == 1, w2, 0.0))[:, 0:SUBLANES]


def _route_call(h, g, mods, layer, wr, br, rows_per_mod):
    n, d = h.shape
    tm = min(ROW_TILE, n)
    ms = functools.partial(_mod_spec, d, layer=layer, rows_per_mod=rows_per_mod, tm=tm, fixed_row=None)
    return pl.pallas_call(
        _route_kernel,
        grid=(n // tm,),
        in_specs=[pl.BlockSpec((tm, d), lambda i: (i, 0)),
                  pl.BlockSpec((1, d), lambda i: (0, 0)), ms(col=3), ms(col=4),
                  pl.BlockSpec((d, LANES), lambda i: (0, 0)),
                  pl.BlockSpec((1, LANES), lambda i: (0, 0))],
        out_specs=[pl.BlockSpec((tm * (d // LANES), LANES), lambda i: (i, 0)),
                   pl.BlockSpec((SUBLANES, tm), lambda i: (0, i)),
                   pl.BlockSpec((tm, SUBLANES), lambda i: (i, 0)),
                   pl.BlockSpec((1, LANES), lambda i: (0, 0))],
        out_shape=[jax.ShapeDtypeStruct((n * (d // LANES), LANES), F32),
                   jax.ShapeDtypeStruct((SUBLANES, n), F32),
                   jax.ShapeDtypeStruct((n, SUBLANES), F32),
                   jax.ShapeDtypeStruct((1, LANES), F32)],
        scratch_shapes=[pltpu.VMEM((1, LANES), F32)],
        compiler_params=_params(("arbitrary",), 40),
        name="moe_route",
    )(h, g, mods, mods, wr, br)


DMA_UNROLL = 4


FILL_CHUNK = 64


def _dispatch_kernel(off_ref, fill_ref, tab_ref, a_ref, xg_out, pos_ref, sem, *, ns):
    tm = tab_ref.shape[1]

    def rows_copy(src, dst, nrows):
        return pltpu.make_async_copy(a_ref.at[pl.ds(pl.multiple_of(src * ns, ns), nrows * ns)],
                                     xg_out.at[pl.ds(pl.multiple_of(dst * ns, ns), nrows * ns)], sem)

    def row_copy(src, dst):
        return rows_copy(src, dst, 1)

    @pl.when(pl.program_id(0) == pl.num_programs(0) - 1)
    def _():
        nfill = fill_ref.shape[0] // 2
        for e in range(nfill):
            start = fill_ref[e]
            length = fill_ref[nfill + e]
            nbig = length // FILL_CHUNK
            nsmall = length - nbig * FILL_CHUNK
            small0 = start + nbig * FILL_CHUNK

            def big(j, carry, start=start):
                rows_copy(0, start + j * FILL_CHUNK, FILL_CHUNK).start()
                return carry

            def small(j, carry, small0=small0):
                row_copy(0, small0 + j).start()
                return carry

            lax.fori_loop(0, nbig, big, 0)
            lax.fori_loop(0, nsmall, small, 0)
            lax.fori_loop(0, nbig, lambda j, c: (rows_copy(0, 0, FILL_CHUNK).wait(), c)[1], 0)
            lax.fori_loop(0, nsmall, lambda j, c: (row_copy(0, 0).wait(), c)[1], 0)

    def issue(j, carry):
        for u in range(DMA_UNROLL):
            t = j * DMA_UNROLL + u
            for k in range(2):
                p = off_ref[tab_ref[k, t]] + tab_ref[2 + k, t]
                pos_ref[k, t] = p
                row_copy(t, p).start()
        return carry

    lax.fori_loop(0, tm // DMA_UNROLL, issue, 0)

    def drain(j, carry):
        for _ in range(2 * DMA_UNROLL):
            row_copy(0, 0).wait()
        return carry

    lax.fori_loop(0, tm // DMA_UNROLL, drain, 0)


def _dispatch_call(off, fill, tab_i, a, rows_g, ns):
    n = a.shape[0] // ns
    tm = min(ROW_TILE, n)
    assert tm >= FILL_CHUNK
    grid_spec = pltpu.PrefetchScalarGridSpec(
        num_scalar_prefetch=2,
        grid=(n // tm,),
        in_specs=[pl.BlockSpec((4, tm), lambda i, off, fill: (0, i), memory_space=pltpu.SMEM),
                  pl.BlockSpec((tm * ns, LANES), lambda i, off, fill: (i, 0))],
        out_specs=[pl.BlockSpec(memory_space=pl.ANY),
                   pl.BlockSpec((2, tm), lambda i, off, fill: (0, i), memory_space=pltpu.SMEM)],
        scratch_shapes=[pltpu.SemaphoreType.DMA(())])
    return pl.pallas_call(
        functools.partial(_dispatch_kernel, ns=ns),
        grid_spec=grid_spec,
        out_shape=[jax.ShapeDtypeStruct((rows_g * ns, LANES), a.dtype), jax.ShapeDtypeStruct((2, n), I32)],
        compiler_params=_params(("arbitrary",), 16),
        name="moe_dispatch",
    )(off, fill, tab_i, a)


def _combine_kernel(pos_ref, posn_ref, y_hbm, h_ref, gcol_ref, g2_ref, fg_ref, o_ref, buf_ref, sem):
    i = pl.program_id(0)
    n = pl.num_programs(0)
    tm, d = h_ref.shape
    ns = d // LANES

    def row_copy(src, slot, k, t):
        return pltpu.make_async_copy(y_hbm.at[pl.ds(pl.multiple_of(src * ns, ns), ns)],
                                     buf_ref.at[slot, k, pl.ds(pl.multiple_of(t * ns, ns), ns)], sem.at[slot])

    def issue(p_ref, slot):
        def body(j, carry):
            for u in range(DMA_UNROLL):
                t = j * DMA_UNROLL + u
                for k in range(2):
                    row_copy(p_ref[k, t], slot, k, t).start()
            return carry
        lax.fori_loop(0, tm // DMA_UNROLL, body, 0)

    @pl.when(i == 0)
    def _():
        issue(pos_ref, 0)

    @pl.when(i + 1 < n)
    def _():
        issue(posn_ref, (i + 1) % 2)

    slot = i % 2

    def drain(j, carry):
        for _ in range(2 * DMA_UNROLL):
            row_copy(0, slot, 0, 0).wait()
        return carry

    lax.fori_loop(0, tm // DMA_UNROLL, drain, 0)
    y = (gcol_ref[:, 0:1] * _from_row_tiles(buf_ref, (slot, 0), tm, ns)
         + gcol_ref[:, 1:2] * _from_row_tiles(buf_ref, (slot, 1), tm, ns))
    hn = h_ref[...] + g2_ref[...] * y
    o_ref[...] = hn * lax.rsqrt(jnp.mean(hn * hn, axis=-1, keepdims=True) + EPS) * fg_ref[...]


def _combine_call(pos, y, h, gcol, mods, layer, final_g, rows_per_mod):
    n, d = h.shape
    tm = min(GATHER_TILE, n)
    nt = n // tm
    ms = functools.partial(_mod_spec, d, layer=layer, rows_per_mod=rows_per_mod, tm=tm, fixed_row=None)
    return pl.pallas_call(
        _combine_kernel,
        grid=(nt,),
        in_specs=[pl.BlockSpec((2, tm), lambda i: (0, i), memory_space=pltpu.SMEM),
                  pl.BlockSpec((2, tm), lambda i: (0, jnp.minimum(i + 1, nt - 1)), memory_space=pltpu.SMEM),
                  pl.BlockSpec(memory_space=pl.ANY),
                  pl.BlockSpec((tm, d), lambda i: (i, 0)),
                  pl.BlockSpec((tm, SUBLANES), lambda i: (i, 0)),
                  ms(col=5),
                  pl.BlockSpec((1, d), lambda i: (0, 0))],
        out_specs=pl.BlockSpec((tm, d), lambda i: (i, 0)),
        out_shape=jax.ShapeDtypeStruct((n, d), F32),
        scratch_shapes=[pltpu.VMEM((2, 2, tm * (d // LANES), LANES), F32), pltpu.SemaphoreType.DMA((2,))],
        compiler_params=_params(("arbitrary",), 24),
        name="moe_combine",
    )(pos, pos, y, h, gcol, mods, final_g)


def _moe_layer(h, g, mods, layer, router, router_b, w1, w3, w2, final_g, rows_per_mod):
    n, d = h.shape
    wr = jnp.zeros((d, LANES), F32).at[:, :N_EXPERTS].set(router)
    br = jnp.zeros((1, LANES), F32).at[0, :N_EXPERTS].set(router_b)
    a, tab, gcol, cnt = _route_call(h, g, mods, layer, wr, br, rows_per_mod)
    tm = ROW_TILE_FFN
    n_tiles = (2 * n) // tm + N_EXPERTS
    counts = cnt[0, :N_EXPERTS].astype(I32)
    tiles = (counts + tm - 1) // tm
    ends = jnp.cumsum(tiles)
    off = ((ends - tiles) * tm).astype(I32)
    tile_expert = jnp.minimum(jnp.sum(jnp.arange(n_tiles, dtype=I32)[:, None] >= ends[None, :], axis=1),
                              N_EXPERTS - 1).astype(I32)
    n_used = ends[-1:].astype(I32)
    assert d // LANES == SUBLANES, "one token must fill one (8, 128) tile of the row-tile layout"
    used = ends[-1:] * tm
    fill = jnp.concatenate([off + counts, used, tiles * tm - counts, n_tiles * tm - used]).astype(I32)
    xg, pos = _dispatch_call(off, fill, tab[0:4].astype(I32), a, n_tiles * tm, d // LANES)
    y = _gffn_call(tile_expert, n_used, xg, w1, w3, w2)
    return _combine_call(pos, y, h, gcol, mods, layer, final_g, rows_per_mod)


def kernel(x, c, ctx, c_ctx, ada_w, ada_b, mix_norm_g, ffn_norm_g, w_in, w_out, na_rpb, conv_w, conv_b, lru_wa, lru_ba, lru_wx, lru_bx, lru_lam, ffn_w1, ffn_w3, ffn_w2, moe_router, moe_router_b, moe_w1, moe_w3, moe_w2, final_g):
    batch, seq, d = x.shape
    ctx_len = ctx.shape[1]
    depth = ada_w.shape[0]
    lw = conv_w.shape[-1]
    aw = (w_in.shape[-1] - 2 * lw) // 3
    rows = seq // GRID_W
    ctx_row = batch

    mod_rows = -(-(batch + 1) // SUBLANES) * SUBLANES
    c_all = jnp.zeros((mod_rows, d), F32).at[:batch].set(c).at[ctx_row].set(c_ctx)
    mods = _ada_call(c_all, ada_w, ada_b).reshape(depth, mod_rows, 1, 6 * d)

    bmods_lat = mods[:, :batch]
    bmods_ctx = jnp.broadcast_to(mods[:, ctx_row:ctx_row + 1], bmods_lat.shape)
    flat = lambda a: a.reshape(a.shape[0] * a.shape[1], a.shape[2])
    h_lat, h_ctx = x, ctx
    out = None
    for i in range(depth):
        last = i == depth - 1
        w_in_bf = w_in[i].astype(BF16)
        w_out_bf = w_out[i].astype(BF16)
        g_mix = mix_norm_g[i].reshape(1, d)
        g_ffn = ffn_norm_g[i].reshape(1, d)
        q_l, k_l, v_l, xr_l, gr_l = _in_proj_call(h_lat, g_mix, bmods_lat, i, w_in_bf, aw, lw)
        q_c, k_c, v_c, xr_c, gr_c = _in_proj_call(h_ctx, g_mix, bmods_ctx, i, w_in_bf, aw, lw)
        oa_l = _na_call(flat(q_l), flat(k_l), flat(v_l), flat(k_c), flat(v_c), _pad_rpb(na_rpb[i]), batch)
        wg, bg = _lru_gate_weights(lru_wa[i], lru_ba[i], lru_wx[i], lru_bx[i], LANES)
        ol_l, ol_c = _lru_call(xr_l, gr_l, xr_c, gr_c, conv_w[i], conv_b[i], wg, bg, lru_lam[i])
        oa_l = oa_l.reshape(batch, seq, aw)
        j = i // 2
        if i % 2 == 0:
            ffn_w = (ffn_w1[j], ffn_w3[j], ffn_w2[j])
            h_lat = _proj_ffn_call(oa_l, ol_l, h_lat, g_ffn, bmods_lat, i, w_out_bf, *ffn_w)
            if not last:
                oa_c = _ctx_attn_call(flat(q_c), flat(k_c), flat(v_c), batch).reshape(batch, ctx_len, aw)
                h_ctx = _proj_ffn_call(oa_c, ol_c, h_ctx, g_ffn, bmods_ctx, i, w_out_bf, *ffn_w)
        else:
            assert last, "the routed layer is fused with the final norm"
            h_lat = _out_proj_call(oa_l, ol_l, h_lat, bmods_lat, i, w_out_bf)
            out = _moe_layer(flat(h_lat), g_ffn, mods, i, moe_router[j], moe_router_b[j],
                             moe_w1[j], moe_w3[j], moe_w2[j], final_g.reshape(1, d), seq)
    return out.reshape(batch, seq, d)
```

```python
import functools

import jax
import jax.numpy as jnp
from jax import lax
from jax.experimental import pallas as pl
from jax.experimental.pallas import tpu as pltpu

F32 = jnp.float32
BF16 = jnp.bfloat16
I32 = jnp.int32

GRID_W = 64
HEAD_DIM = 64
N_HEADS = 8
NA_KH = 8
NA_KW = 16
LRU_BLOCK = 64
LRU_C = 8.0
CONV_W = 4
N_EXPERTS = 8
EPS = 1e-6
NEG_INF = -1e30
LOG2E = 1.4426950408889634
LANES = 128
SUBLANES = 8
VMEM_BYTES = 64 * 1024 * 1024

FF_CHUNK = 512
ROW_TILE_FFN = 1024
ROW_TILE = 512
GATHER_TILE = 256


def _params(semantics, vmem_mb):
    return pltpu.CompilerParams(dimension_semantics=semantics,
                                vmem_limit_bytes=min(vmem_mb * 1024 * 1024, VMEM_BYTES - 4 * 1024 * 1024))


def _rms_mod(x, g, sh, sc):
    y = x * lax.rsqrt(jnp.mean(x * x, axis=-1, keepdims=True) + EPS)
    return (y * g) * (1.0 + sc) + sh


def _ada_kernel(c_ref, w_ref, b_ref, o_ref):
    c = c_ref[...]
    s = c * jax.nn.sigmoid(c)
    o_ref[...] = jnp.dot(s, w_ref[...], preferred_element_type=F32,
                         precision=lax.Precision.HIGHEST) + b_ref[...]


def _ada_call(c_all, ada_w, ada_b):
    depth, d, n = ada_w.shape
    rows = c_all.shape[0]
    tn = 1024
    return pl.pallas_call(
        _ada_kernel,
        grid=(depth, n // tn),
        in_specs=[pl.BlockSpec((rows, d), lambda l, j: (0, 0)),
                  pl.BlockSpec((None, d, tn), lambda l, j: (l, 0, j)),
                  pl.BlockSpec((None, 1, tn), lambda l, j: (l, 0, j))],
        out_specs=pl.BlockSpec((None, rows, tn), lambda l, j: (l, 0, j)),
        out_shape=jax.ShapeDtypeStruct((depth, rows, n), F32),
        compiler_params=_params(("arbitrary", "arbitrary"), 24),
        name="ada_mod",
    )(c_all, ada_w, ada_b.reshape(depth, 1, n))


TIME_TILE = 64


def _to_time_major(r, o_ref, nb, tt):
    for c in range(o_ref.shape[0]):
        for b in range(nb):
            o_ref[c, pl.ds(b, tt, stride=SUBLANES), :] = r[b * tt:(b + 1) * tt, c * LANES:(c + 1) * LANES]


def _from_time_major(x_ref, nb, tt):
    return jnp.concatenate(
        [jnp.concatenate([x_ref[c, pl.ds(b, tt, stride=SUBLANES), :] for c in range(x_ref.shape[0])], axis=1)
         for b in range(nb)], axis=0)


def _in_proj_kernel(x_ref, g_ref, sh_ref, sc_ref, w_ref, q_ref, k_ref, v_ref, xr_ref, gr_ref, *, aw, lw):
    nb, tt, d = x_ref.shape
    a = _rms_mod(x_ref[...], g_ref[...], sh_ref[...], sc_ref[...]).astype(BF16).reshape(nb * tt, d)
    r = jnp.dot(a, w_ref[...], preferred_element_type=F32)
    q_ref[...] = (r[:, 0:aw] * (LOG2E * HEAD_DIM ** -0.5)).astype(BF16).reshape(nb, tt, aw)
    k_ref[...] = r[:, aw:2 * aw].astype(BF16).reshape(nb, tt, aw)
    v_ref[...] = r[:, 2 * aw:3 * aw].astype(BF16).reshape(nb, tt, aw)
    _to_time_major(r[:, 3 * aw:3 * aw + lw], xr_ref, nb, tt)
    _to_time_major(r[:, 3 * aw + lw:3 * aw + 2 * lw], gr_ref, nb, tt)


def _mod_spec(d, col, layer, rows_per_mod, tm, fixed_row):
    if fixed_row is None:
        assert rows_per_mod % tm == 0, "a row tile must not straddle two batch elements"
        return pl.BlockSpec((None, None, 1, d), lambda i, *_: (layer, (i * tm) // rows_per_mod, 0, col))
    return pl.BlockSpec((None, None, 1, d), lambda i, *_: (layer, fixed_row, 0, col))


def _batch_mod_spec(nb, d, layer, col):
    return pl.BlockSpec((None, nb, 1, d), lambda j: (layer, 0, 0, col))


def _in_proj_call(x, g, bmods, layer, w_bf, aw, lw):
    nb, t, d = x.shape
    assert nb == SUBLANES, "time-major rows put the batch on the sublanes"
    tt = min(TIME_TILE, t)
    ncol = w_bf.shape[1]
    ncb = lw // LANES
    bt = lambda w: pl.BlockSpec((nb, tt, w), lambda j: (0, j, 0))
    tm = pl.BlockSpec((ncb, tt * SUBLANES, LANES), lambda j: (0, j, 0))
    return pl.pallas_call(
        functools.partial(_in_proj_kernel, aw=aw, lw=lw),
        grid=(t // tt,),
        in_specs=[bt(d), pl.BlockSpec((1, d), lambda j: (0, 0)),
                  _batch_mod_spec(nb, d, layer, 0), _batch_mod_spec(nb, d, layer, 1),
                  pl.BlockSpec((d, ncol), lambda j: (0, 0))],
        out_specs=[bt(aw), bt(aw), bt(aw), tm, tm],
        out_shape=[jax.ShapeDtypeStruct((nb, t, aw), BF16)] * 3
        + [jax.ShapeDtypeStruct((ncb, t * SUBLANES, LANES), F32)] * 2,
        compiler_params=_params(("arbitrary",), 40),
        name="in_proj",
    )(x, g, bmods, bmods, w_bf)


def _softmax_pv(parts):
    m = None
    for s, _ in parts:
        mi = jnp.max(s, axis=-1, keepdims=True)
        m = mi if m is None else jnp.maximum(m, mi)
    l = None
    acc = None
    for s, v in parts:
        p = jnp.exp2(s - m)
        li = jnp.sum(p, axis=-1, keepdims=True)
        ai = jnp.dot(p.astype(BF16), v, preferred_element_type=F32)
        l = li if l is None else l + li
        acc = ai if acc is None else acc + ai
    return acc * (1.0 / l)


def _qk(q, k):
    return lax.dot_general(q, k, (((1,), (1,)), ((), ())), preferred_element_type=F32)


def _build_na_bias(rpb_ref, bias_ref):
    lane = lax.broadcasted_iota(I32, (GRID_W, LANES), 1)
    q = lax.broadcasted_iota(I32, (GRID_W, LANES), 0)
    kcol = lane & (GRID_W - 1)
    cs = jnp.clip(q - NA_KW // 2, 0, GRID_W - NA_KW)
    ok = (kcol >= cs) & (kcol < cs + NA_KW)
    low = lane < GRID_W
    for delta in range(NA_KH):
        for h in range(N_HEADS):
            for jp in range(NA_KH // 2):
                halves = []
                for j in (2 * jp, 2 * jp + 1):
                    dr = j - delta + NA_KH - 1
                    w = jnp.broadcast_to(rpb_ref[h, dr:dr + 1, :], (GRID_W, LANES))
                    base = (j % 2) * GRID_W - (NA_KW - 1)
                    halves.append(pltpu.roll(w, base % LANES, 1, stride=1, stride_axis=0))
                t = jnp.where(low, halves[0], halves[1])
                bias_ref[delta, h, :, jp * LANES:(jp + 1) * LANES] = jnp.where(ok, t * LOG2E, NEG_INF)


NA_ROWS_PER_STEP = 4


def _stack_pair(q2):
    first = lax.broadcasted_iota(I32, q2.shape, 1) < HEAD_DIM
    zero = jnp.zeros_like(q2)
    return jnp.concatenate([jnp.where(first, q2, zero), jnp.where(first, zero, q2)], axis=0)


def _unstack_pair(o2):
    nq = o2.shape[0] // 2
    first = lax.broadcasted_iota(I32, (nq, o2.shape[1]), 1) < HEAD_DIM
    return jnp.where(first, o2[:nq], o2[nq:])


def _na_kernel(q_ref, kl_ref, vl_ref, kc_ref, vc_ref, rpb_ref, o_ref, bias_ref, s_ref, p_ref, *, rows, rb):
    @pl.when((pl.program_id(0) == 0) & (pl.program_id(1) == 0))
    def _():
        _build_na_bias(rpb_ref, bias_ref)

    nwin = NA_KH * GRID_W
    npair = N_HEADS // 2
    for rho in range(rb):
        r = pl.program_id(1) * rb + rho
        rs = jnp.clip(r - NA_KH // 2, 0, rows - NA_KH)
        delta = r - rs
        k0 = pl.multiple_of(rs * GRID_W, GRID_W)
        qrows = slice(rho * GRID_W, (rho + 1) * GRID_W)
        for pair in range(npair):
            sl = slice(pair * 2 * HEAD_DIM, (pair + 1) * 2 * HEAD_DIM)
            keys = jnp.concatenate([kl_ref[pl.ds(k0, nwin), sl], kc_ref[:, sl]], axis=0)
            s = _qk(_stack_pair(q_ref[qrows, sl]), keys)
            for sub in range(2):
                h = pair * 2 + sub
                rr = slice(sub * GRID_W, (sub + 1) * GRID_W)
                s_ref[rho, h, :, 0:nwin] = s[rr, 0:nwin] + bias_ref[delta, h]
                s_ref[rho, h, :, nwin:] = s[rr, nwin:]
        s = s_ref[rho]
        p = jnp.exp2(s - jnp.max(s, axis=-1, keepdims=True))
        inv = 1.0 / jnp.sum(p, axis=-1, keepdims=True)
        p_ref[rho] = p.astype(BF16)
        for pair in range(npair):
            sl = slice(pair * 2 * HEAD_DIM, (pair + 1) * 2 * HEAD_DIM)
            vals = jnp.concatenate([vl_ref[pl.ds(k0, nwin), sl], vc_ref[:, sl]], axis=0)
            p2 = p_ref[rho, 2 * pair:2 * pair + 2].reshape(2 * GRID_W, p_ref.shape[-1])
            o2 = jnp.dot(p2, vals, preferred_element_type=F32) * inv[2 * pair:2 * pair + 2].reshape(2 * GRID_W, 1)
            o_ref[qrows, sl] = _unstack_pair(o2).astype(o_ref.dtype)


def _na_call(q, kl, vl, kc, vc, rpb, batch):
    s = q.shape[0] // batch
    l = kc.shape[0] // batch
    w = q.shape[1]
    rows = s // GRID_W
    assert rows >= NA_KH
    rb = NA_ROWS_PER_STEP if rows % NA_ROWS_PER_STEP == 0 else 1
    nkeys = NA_KH * GRID_W + l
    nrb = rows // rb
    return pl.pallas_call(
        functools.partial(_na_kernel, rows=rows, rb=rb),
        grid=(batch, nrb),
        in_specs=[pl.BlockSpec((rb * GRID_W, w), lambda b, r: (b * nrb + r, 0)),
                  pl.BlockSpec((s, w), lambda b, r: (b, 0)),
                  pl.BlockSpec((s, w), lambda b, r: (b, 0)),
                  pl.BlockSpec((l, w), lambda b, r: (b, 0)),
                  pl.BlockSpec((l, w), lambda b, r: (b, 0)),
                  pl.BlockSpec(rpb.shape, lambda b, r: (0, 0, 0))],
        out_specs=pl.BlockSpec((rb * GRID_W, w), lambda b, r: (b * nrb + r, 0)),
        out_shape=jax.ShapeDtypeStruct(q.shape, BF16),
        scratch_shapes=[pltpu.VMEM((NA_KH, N_HEADS, GRID_W, NA_KH * GRID_W), F32),
                        pltpu.VMEM((rb, N_HEADS, GRID_W, nkeys), F32),
                        pltpu.VMEM((rb, N_HEADS, GRID_W, nkeys), BF16)],
        compiler_params=_params(("arbitrary", "arbitrary"), 48),
        name="na_attention",
    )(q, kl, vl, kc, vc, rpb)


def _pad_rpb(rpb):
    h, nr, nc = rpb.shape
    return jnp.zeros((h, 2 * NA_KH, LANES), F32).at[:, :nr, :nc].set(rpb.astype(F32))


def _ctx_attn_kernel(q_ref, k_ref, v_ref, o_ref):
    rows = q_ref.shape[0]
    lane = lax.broadcasted_iota(I32, (rows, 2 * HEAD_DIM), 1)
    first = lane < HEAD_DIM
    for pair in range(N_HEADS // 2):
        sl = slice(pair * 2 * HEAD_DIM, (pair + 1) * 2 * HEAD_DIM)
        q2 = q_ref[:, sl]
        k = k_ref[:, sl]
        v = v_ref[:, sl]
        outs = []
        for sub in range(2):
            qh = jnp.where(first if sub == 0 else ~first, q2, jnp.zeros_like(q2))
            outs.append(_softmax_pv([(_qk(qh, k), v)]))
        o_ref[:, sl] = jnp.where(first, outs[0], outs[1]).astype(o_ref.dtype)


def _ctx_attn_call(q, k, v, batch):
    l = q.shape[0] // batch
    w = q.shape[1]
    spec = pl.BlockSpec((l, w), lambda b: (b, 0))
    return pl.pallas_call(
        _ctx_attn_kernel,
        grid=(batch,),
        in_specs=[spec, spec, spec],
        out_specs=spec,
        out_shape=jax.ShapeDtypeStruct(q.shape, BF16),
        compiler_params=_params(("arbitrary",), 24),
        name="ctx_attention",
    )(q, k, v)


def _softplus(x):
    return jnp.maximum(x, 0.0) + jnp.log1p(jnp.exp(-jnp.abs(x)))


LRU_TIME_CHUNK = 64


def _lru_kernel(xl_hbm, xc_hbm, gl_ref, gc_ref, cw_ref, cb_ref, wg_ref, bg_ref, lam_ref, ol_ref, oc_ref,
                padl_ref, padc_ref, sem, *, tchunk):
    c = pl.program_id(0)
    slot = c % 2
    nb = SUBLANES
    halo_lo, halo_hi = nb, 2 * nb
    zero = jnp.zeros((nb, LANES), F32)

    def fetches(block, slot):
        return [pltpu.make_async_copy(x_hbm.at[block], pad_ref.at[slot, pl.ds(halo_lo, x_hbm.shape[1])], sem.at[slot, k])
                for k, (x_hbm, pad_ref) in enumerate(((xc_hbm, padc_ref), (xl_hbm, padl_ref)))]

    @pl.when(c == 0)
    def _():
        for x_hbm, pad_ref in ((xc_hbm, padc_ref), (xl_hbm, padl_ref)):
            n = x_hbm.shape[1]
            for s in range(2):
                pad_ref[s, 0:halo_lo, :] = zero
                pad_ref[s, halo_lo + n:halo_lo + n + halo_hi, :] = jnp.zeros((halo_hi, LANES), F32)
        for cp in fetches(0, 0):
            cp.start()

    @pl.when(c + 1 < pl.num_programs(0))
    def _():
        for cp in fetches(c + 1, 1 - slot):
            cp.start()

    for cp in fetches(c, slot):
        cp.wait()

    la = [(-0.5 * LRU_C * LOG2E) * _softplus(-lam_ref[d:d + 1, :]) for d in range(2)]
    cb = cb_ref[...]
    cw = [cw_ref[j:j + 1, :] for j in range(CONV_W)]
    rows = tchunk * nb

    def segment(pad_ref, g_ref, o_ref, d, h):
        n = o_ref.shape[0] // rows

        def body(i, h):
            r0 = pl.multiple_of((i if d == 0 else n - 1 - i) * rows, rows)
            u = cb
            for j in range(CONV_W):
                u = u + pad_ref[slot, pl.ds(r0 + j * nb, rows), :] * cw[j]
            ub = u.astype(BF16)
            ta = jnp.tanh(jnp.dot(ub, wg_ref[d, 0], preferred_element_type=F32) + bg_ref[d, 0])
            ti = jnp.tanh(jnp.dot(ub, wg_ref[d, 1], preferred_element_type=F32) + bg_ref[d, 1])
            a = jnp.exp2(la[d] * ta + la[d])
            z = 1.0 - a * a
            b = jnp.where(z > 0.0, z * lax.rsqrt(z), 0.0) * (0.5 * ti + 0.5) * u
            hs = [None] * tchunk
            for t in (range(tchunk) if d == 0 else reversed(range(tchunk))):
                h = a[t * nb:(t + 1) * nb] * h + b[t * nb:(t + 1) * nb]
                hs[t] = h
            hcat = jnp.concatenate(hs, axis=0)
            if d == 0:
                o_ref[pl.ds(r0, rows), :] = hcat
            else:
                y = o_ref[pl.ds(r0, rows), :] + hcat
                o_ref[pl.ds(r0, rows), :] = jax.nn.gelu(g_ref[pl.ds(r0, rows), :]) * y
            return h

        return lax.fori_loop(0, n, body, h)

    for d in range(2):
        h = segment(padc_ref, gc_ref, oc_ref, d, zero)
        segment(padl_ref, gl_ref, ol_ref, d, h)


def _lru_call(xr_l, gr_l, xr_c, gr_c, conv_w, conv_b, wg, bg, lam):
    ncb, rl, _ = xr_l.shape
    rc = xr_c.shape[1]
    tchunk = min(LRU_TIME_CHUNK, rl // SUBLANES, rc // SUBLANES)
    halo = 3 * SUBLANES
    lat = pl.BlockSpec((None, rl, LANES), lambda c: (c, 0, 0))
    ctx = pl.BlockSpec((None, rc, LANES), lambda c: (c, 0, 0))
    per_block = lambda a: jnp.moveaxis(a.reshape(a.shape[:-1] + (ncb, LANES)), -2, 0)
    return pl.pallas_call(
        functools.partial(_lru_kernel, tchunk=tchunk),
        grid=(ncb,),
        in_specs=[pl.BlockSpec(memory_space=pl.ANY), pl.BlockSpec(memory_space=pl.ANY),
                  lat, ctx,
                  pl.BlockSpec((None, CONV_W, LANES), lambda c: (c, 0, 0)),
                  pl.BlockSpec((None, 1, LANES), lambda c: (c, 0, 0)),
                  pl.BlockSpec((2, 2, None, LANES, LANES), lambda c: (0, 0, c, 0, 0)),
                  pl.BlockSpec((None, 2, 2, 1, LANES), lambda c: (c, 0, 0, 0, 0)),
                  pl.BlockSpec((None, 2, LANES), lambda c: (c, 0, 0))],
        out_specs=[lat, ctx],
        out_shape=[jax.ShapeDtypeStruct(xr_l.shape, F32), jax.ShapeDtypeStruct(xr_c.shape, F32)],
        scratch_shapes=[pltpu.VMEM((2, rl + halo, LANES), F32), pltpu.VMEM((2, rc + halo, LANES), F32),
                        pltpu.SemaphoreType.DMA((2, 2))],
        compiler_params=_params(("arbitrary",), 58),
        name="rglru",
    )(xr_l, xr_c, gr_l, gr_c, per_block(conv_w), per_block(conv_b.reshape(1, -1)), wg,
      per_block(bg[:, :, None, :]), per_block(lam))


def _lru_gate_weights(wa, ba, wx, bx, half):
    nblk = wa.shape[1]
    lw = nblk * LRU_BLOCK
    per = half // LRU_BLOCK

    def dense(w):
        w = w.reshape(2, nblk // per, per, LRU_BLOCK, LRU_BLOCK)
        eye = jnp.eye(per, dtype=w.dtype)
        full = w[:, :, :, :, None, :] * eye[None, None, :, None, :, None]
        return full.reshape(2, nblk // per, half, half)

    wg = (0.5 * jnp.stack([dense(wa), dense(wx)], axis=1)).astype(BF16)
    bg = 0.5 * jnp.stack([ba, bx], axis=1).astype(F32)
    return wg, bg


def _out_proj_kernel(oa_ref, ol_ref, h_ref, g1_ref, w_ref, o_ref, *, aw):
    o_ref[...] = _mix_proj(oa_ref, ol_ref, h_ref, g1_ref, w_ref, aw)


def _out_proj_call(oa, ol, h, bmods, layer, w_bf):
    nb, t, d = h.shape
    aw = oa.shape[2]
    ncb = ol.shape[0]
    tt = min(TIME_TILE, t)
    bt = lambda w: pl.BlockSpec((nb, tt, w), lambda j: (0, j, 0))
    return pl.pallas_call(
        functools.partial(_out_proj_kernel, aw=aw),
        grid=(t // tt,),
        in_specs=[bt(aw), pl.BlockSpec((ncb, tt * SUBLANES, LANES), lambda j: (0, j, 0)), bt(d),
                  _batch_mod_spec(nb, d, layer, 2), pl.BlockSpec(w_bf.shape, lambda j: (0, 0))],
        out_specs=bt(d),
        out_shape=jax.ShapeDtypeStruct((nb, t, d), F32),
        compiler_params=_params(("arbitrary",), 32),
        name="out_proj",
    )(oa, ol, h, bmods, w_bf)


def _swiglu_step(a_bf, w1_ref, w3_ref, w2_ref, acc_ref):
    g = jnp.dot(a_bf, w1_ref[...].astype(BF16), preferred_element_type=F32)
    u = jnp.dot(a_bf, w3_ref[...].astype(BF16), preferred_element_type=F32)
    hmid = (g * jax.nn.sigmoid(g) * u).astype(BF16)
    acc_ref[...] += jnp.dot(hmid, w2_ref[...].astype(BF16), preferred_element_type=F32)


def _mix_proj(oa_ref, ol_ref, h_ref, g1_ref, wo_ref, aw):
    nb, tt, d = h_ref.shape
    o = jnp.dot(oa_ref[...].reshape(nb * tt, aw), wo_ref[0:aw, :], preferred_element_type=F32)
    ol = _from_time_major(ol_ref, nb, tt).astype(BF16)
    o = o + jnp.dot(ol, wo_ref[aw:, :], preferred_element_type=F32)
    return h_ref[...] + g1_ref[...] * o.reshape(nb, tt, d)


def _proj_ffn_kernel(oa_ref, ol_ref, h_ref, g1_ref, wo_ref, g_ref, sh_ref, sc_ref, g2_ref,
                     w1_ref, w3_ref, w2_ref, o_ref, a_ref, acc_ref, *, aw):
    c = pl.program_id(1)
    nb, tt, d = h_ref.shape

    @pl.when(c == 0)
    def _():
        h1 = _mix_proj(oa_ref, ol_ref, h_ref, g1_ref, wo_ref, aw)
        o_ref[...] = h1
        a_ref[...] = _rms_mod(h1, g_ref[...], sh_ref[...], sc_ref[...]).astype(BF16).reshape(nb * tt, d)
        acc_ref[...] = jnp.zeros_like(acc_ref)

    _swiglu_step(a_ref[...], w1_ref, w3_ref, w2_ref, acc_ref)

    @pl.when(c == pl.num_programs(1) - 1)
    def _():
        o_ref[...] += g2_ref[...] * acc_ref[...].reshape(nb, tt, d)


def _proj_ffn_call(oa, ol, h, g, bmods, layer, wo_bf, w1, w3, w2):
    nb, t, d = h.shape
    aw = oa.shape[2]
    ncb = ol.shape[0]
    dff = w1.shape[1]
    tt = min(ROW_TILE_FFN // nb, t)
    bt = lambda w: pl.BlockSpec((nb, tt, w), lambda j, c: (0, j, 0))
    bm = lambda col: pl.BlockSpec((None, nb, 1, d), lambda j, c: (layer, 0, 0, col))
    return pl.pallas_call(
        functools.partial(_proj_ffn_kernel, aw=aw),
        grid=(t // tt, dff // FF_CHUNK),
        in_specs=[bt(aw), pl.BlockSpec((ncb, tt * SUBLANES, LANES), lambda j, c: (0, j, 0)), bt(d), bm(2),
                  pl.BlockSpec(wo_bf.shape, lambda j, c: (0, 0), pipeline_mode=pl.Buffered(1)),
                  pl.BlockSpec((1, d), lambda j, c: (0, 0)), bm(3), bm(4), bm(5),
                  pl.BlockSpec((d, FF_CHUNK), lambda j, c: (0, c)),
                  pl.BlockSpec((d, FF_CHUNK), lambda j, c: (0, c)),
                  pl.BlockSpec((FF_CHUNK, d), lambda j, c: (c, 0))],
        out_specs=bt(d),
        out_shape=jax.ShapeDtypeStruct((nb, t, d), F32),
        scratch_shapes=[pltpu.VMEM((nb * tt, d), BF16), pltpu.VMEM((nb * tt, d), F32)],
        compiler_params=_params(("arbitrary", "arbitrary"), 58),
        name="proj_ffn",
    )(oa, ol, h, bmods, wo_bf, g, bmods, bmods, bmods, w1, w3, w2)


def _to_row_tiles(x, o_ref):
    t, d = x.shape
    ns = d // LANES
    for s in range(ns):
        o_ref[pl.ds(s, t, stride=ns), :] = x[:, s * LANES:(s + 1) * LANES]


def _from_row_tiles(x_ref, idx, t, ns):
    return jnp.concatenate([x_ref[idx + (pl.ds(s, t, stride=ns), slice(None))] for s in range(ns)], axis=1)


def _gffn_kernel(te_ref, nu_ref, x_ref, w1_ref, w3_ref, w2_ref, o_ref, a_ref, acc_ref):
    i = pl.program_id(0)
    c = pl.program_id(1)
    tm, d = a_ref.shape

    @pl.when(i < nu_ref[0])
    def _():
        @pl.when(c == 0)
        def _():
            a_ref[...] = _from_row_tiles(x_ref, (), tm, d // LANES).astype(BF16)
            acc_ref[...] = jnp.zeros_like(acc_ref)

        _swiglu_step(a_ref[...], w1_ref, w3_ref, w2_ref, acc_ref)

        @pl.when(c == pl.num_programs(1) - 1)
        def _():
            _to_row_tiles(acc_ref[...], o_ref)

    @pl.when((i >= nu_ref[0]) & (c == pl.num_programs(1) - 1))
    def _():
        o_ref[...] = jnp.zeros_like(o_ref)


def _gffn_call(tile_expert, n_used, xg, w1, w3, w2):
    d = w1.shape[1]
    ns = d // LANES
    rg = xg.shape[0] // ns
    dff = w1.shape[2]
    tm = ROW_TILE_FFN
    nc = dff // FF_CHUNK

    def tile(i, nu):
        return jnp.maximum(jnp.minimum(i, nu[0] - 1), 0)

    def chunk(i, c, nu):
        return jnp.where(i < nu[0], c, nc - 1)

    grid_spec = pltpu.PrefetchScalarGridSpec(
        num_scalar_prefetch=2,
        grid=(rg // tm, nc),
        in_specs=[pl.BlockSpec((tm * ns, LANES), lambda i, c, te, nu: (tile(i, nu), 0)),
                  pl.BlockSpec((None, d, FF_CHUNK), lambda i, c, te, nu: (te[tile(i, nu)], 0, chunk(i, c, nu))),
                  pl.BlockSpec((None, d, FF_CHUNK), lambda i, c, te, nu: (te[tile(i, nu)], 0, chunk(i, c, nu))),
                  pl.BlockSpec((None, FF_CHUNK, d), lambda i, c, te, nu: (te[tile(i, nu)], chunk(i, c, nu), 0))],
        out_specs=pl.BlockSpec((tm * ns, LANES), lambda i, c, te, nu: (i, 0)),
        scratch_shapes=[pltpu.VMEM((tm, d), BF16), pltpu.VMEM((tm, d), F32)])
    return pl.pallas_call(
        _gffn_kernel,
        grid_spec=grid_spec,
        out_shape=jax.ShapeDtypeStruct((rg * ns, LANES), F32),
        compiler_params=_params(("arbitrary", "arbitrary"), 56),
        name="ffn_grouped",
    )(tile_expert, n_used, xg, w1, w3, w2)


def _route_kernel(x_ref, g_ref, sh_ref, sc_ref, wr_ref, br_ref, a_ref, tab_ref, gcol_ref, cnt_ref, carry_ref):
    i = pl.program_id(0)
    tm = x_ref.shape[0]

    @pl.when(i == 0)
    def _():
        carry_ref[...] = jnp.zeros_like(carry_ref)

    a = _rms_mod(x_ref[...], g_ref[...], sh_ref[...], sc_ref[...])
    _to_row_tiles(a, a_ref)
    lane = lax.broadcasted_iota(I32, (tm, LANES), 1).astype(F32)
    logits = jnp.dot(a, wr_ref[...], preferred_element_type=F32, precision=lax.Precision.HIGHEST) + br_ref[...]
    logits = jnp.where(lane < N_EXPERTS, logits, NEG_INF)
    m1 = jnp.max(logits, axis=-1, keepdims=True)
    i1 = jnp.min(jnp.where(logits == m1, lane, float(LANES)), axis=-1, keepdims=True)
    rest = jnp.where(lane == i1, 2.0 * NEG_INF, logits)
    m2 = jnp.max(rest, axis=-1, keepdims=True)
    i2 = jnp.min(jnp.where(rest == m2, lane, float(LANES)), axis=-1, keepdims=True)
    e = jnp.exp(m2 - m1)
    w1 = 1.0 / (1.0 + e)
    w2 = e * w1
    sel1 = lane == i1
    sel2 = lane == i2
    onehot = jnp.where(sel1 | sel2, 1.0, 0.0)
    row = lax.broadcasted_iota(I32, (tm, tm), 0)
    col = lax.broadcasted_iota(I32, (tm, tm), 1)
    before = jnp.where(col < row, 1.0, 0.0).astype(BF16)
    cum = jnp.dot(before, onehot.astype(BF16), preferred_element_type=F32) + carry_ref[...]
    r1 = jnp.sum(jnp.where(sel1, cum, 0.0), axis=-1, keepdims=True)
    r2 = jnp.sum(jnp.where(sel2, cum, 0.0), axis=-1, keepdims=True)
    carry_ref[...] += jnp.sum(onehot, axis=0, keepdims=True)
    cnt_ref[...] = carry_ref[...]
    cols = (jnp.where(lane == 0, i1, 0.0) + jnp.where(lane == 1, i2, 0.0)
            + jnp.where(lane == 2, r1, 0.0) + jnp.where(lane == 3, r2, 0.0))
    tab_ref[...] = cols.T[0:SUBLANES, :]
    gcol_ref[...] = (jnp.where(lane == 0, w1, 0.0) + jnp.where(lane == 1, w2, 0.0))[:, 0:SUBLANES]


def _route_call(h, g, mods, layer, wr, br, rows_per_mod):
    n, d = h.shape
    tm = min(ROW_TILE, n)
    ms = functools.partial(_mod_spec, d, layer=layer, rows_per_mod=rows_per_mod, tm=tm, fixed_row=None)
    return pl.pallas_call(
        _route_kernel,
        grid=(n // tm,),
        in_specs=[pl.BlockSpec((tm, d), lambda i: (i, 0)),
                  pl.BlockSpec((1, d), lambda i: (0, 0)), ms(col=3), ms(col=4),
                  pl.BlockSpec((d, LANES), lambda i: (0, 0)),
                  pl.BlockSpec((1, LANES), lambda i: (0, 0))],
        out_specs=[pl.BlockSpec((tm * (d // LANES), LANES), lambda i: (i, 0)),
                   pl.BlockSpec((SUBLANES, tm), lambda i: (0, i)),
                   pl.BlockSpec((tm, SUBLANES), lambda i: (i, 0)),
                   pl.BlockSpec((1, LANES), lambda i: (0, 0))],
        out_shape=[jax.ShapeDtypeStruct((n * (d // LANES), LANES), F32),
                   jax.ShapeDtypeStruct((SUBLANES, n), F32),
                   jax.ShapeDtypeStruct((n, SUBLANES), F32),
                   jax.ShapeDtypeStruct((1, LANES), F32)],
        scratch_shapes=[pltpu.VMEM((1, LANES), F32)],
        compiler_params=_params(("arbitrary",), 40),
        name="moe_route",
    )(h, g, mods, mods, wr, br)


DMA_UNROLL = 4


FILL_CHUNK = 64


def _dispatch_kernel(off_ref, fill_ref, tab_ref, a_ref, xg_out, pos_ref, sem, *, ns):
    tm = tab_ref.shape[1]

    def rows_copy(src, dst, nrows):
        return pltpu.make_async_copy(a_ref.at[pl.ds(pl.multiple_of(src * ns, ns), nrows * ns)],
                                     xg_out.at[pl.ds(pl.multiple_of(dst * ns, ns), nrows * ns)], sem)

    def row_copy(src, dst):
        return rows_copy(src, dst, 1)

    @pl.when(pl.program_id(0) == pl.num_programs(0) - 1)
    def _():
        nfill = fill_ref.shape[0] // 2
        for e in range(nfill):
            start = fill_ref[e]
            length = fill_ref[nfill + e]
            nbig = length // FILL_CHUNK
            nsmall = length - nbig * FILL_CHUNK
            small0 = start + nbig * FILL_CHUNK

            def big(j, carry, start=start):
                rows_copy(0, start + j * FILL_CHUNK, FILL_CHUNK).start()
                return carry

            def small(j, carry, small0=small0):
                row_copy(0, small0 + j).start()
                return carry

            lax.fori_loop(0, nbig, big, 0)
            lax.fori_loop(0, nsmall, small, 0)
            lax.fori_loop(0, nbig, lambda j, c: (rows_copy(0, 0, FILL_CHUNK).wait(), c)[1], 0)
            lax.fori_loop(0, nsmall, lambda j, c: (row_copy(0, 0).wait(), c)[1], 0)

    def issue(j, carry):
        for u in range(DMA_UNROLL):
            t = j * DMA_UNROLL + u
            for k in range(2):
                p = off_ref[tab_ref[k, t]] + tab_ref[2 + k, t]
                pos_ref[k, t] = p
                row_copy(t, p).start()
        return carry

    lax.fori_loop(0, tm // DMA_UNROLL, issue, 0)

    def drain(j, carry):
        for _ in range(2 * DMA_UNROLL):
            row_copy(0, 0).wait()
        return carry

    lax.fori_loop(0, tm // DMA_UNROLL, drain, 0)


def _dispatch_call(off, fill, tab_i, a, rows_g, ns):
    n = a.shape[0] // ns
    tm = min(ROW_TILE, n)
    assert tm >= FILL_CHUNK
    grid_spec = pltpu.PrefetchScalarGridSpec(
        num_scalar_prefetch=2,
        grid=(n // tm,),
        in_specs=[pl.BlockSpec((4, tm), lambda i, off, fill: (0, i), memory_space=pltpu.SMEM),
                  pl.BlockSpec((tm * ns, LANES), lambda i, off, fill: (i, 0))],
        out_specs=[pl.BlockSpec(memory_space=pl.ANY),
                   pl.BlockSpec((2, tm), lambda i, off, fill: (0, i), memory_space=pltpu.SMEM)],
        scratch_shapes=[pltpu.SemaphoreType.DMA(())])
    return pl.pallas_call(
        functools.partial(_dispatch_kernel, ns=ns),
        grid_spec=grid_spec,
        out_shape=[jax.ShapeDtypeStruct((rows_g * ns, LANES), a.dtype), jax.ShapeDtypeStruct((2, n), I32)],
        compiler_params=_params(("arbitrary",), 16),
        name="moe_dispatch",
    )(off, fill, tab_i, a)


def _combine_kernel(pos_ref, posn_ref, y_hbm, h_ref, gcol_ref, g2_ref, fg_ref, o_ref, buf_ref, sem):
    i = pl.program_id(0)
    n = pl.num_programs(0)
    tm, d = h_ref.shape
    ns = d // LANES

    def row_copy(src, slot, k, t):
        return pltpu.make_async_copy(y_hbm.at[pl.ds(pl.multiple_of(src * ns, ns), ns)],
                                     buf_ref.at[slot, k, pl.ds(pl.multiple_of(t * ns, ns), ns)], sem.at[slot])

    def issue(p_ref, slot):
        def body(j, carry):
            for u in range(DMA_UNROLL):
                t = j * DMA_UNROLL + u
                for k in range(2):
                    row_copy(p_ref[k, t], slot, k, t).start()
            return carry
        lax.fori_loop(0, tm // DMA_UNROLL, body, 0)

    @pl.when(i == 0)
    def _():
        issue(pos_ref, 0)

    @pl.when(i + 1 < n)
    def _():
        issue(posn_ref, (i + 1) % 2)

    slot = i % 2

    def drain(j, carry):
        for _ in range(2 * DMA_UNROLL):
            row_copy(0, slot, 0, 0).wait()
        return carry

    lax.fori_loop(0, tm // DMA_UNROLL, drain, 0)
    y = (gcol_ref[:, 0:1] * _from_row_tiles(buf_ref, (slot, 0), tm, ns)
         + gcol_ref[:, 1:2] * _from_row_tiles(buf_ref, (slot, 1), tm, ns))
    hn = h_ref[...] + g2_ref[...] * y
    o_ref[...] = hn * lax.rsqrt(jnp.mean(hn * hn, axis=-1, keepdims=True) + EPS) * fg_ref[...]


def _combine_call(pos, y, h, gcol, mods, layer, final_g, rows_per_mod):
    n, d = h.shape
    tm = min(GATHER_TILE, n)
    nt = n // tm
    ms = functools.partial(_mod_spec, d, layer=layer, rows_per_mod=rows_per_mod, tm=tm, fixed_row=None)
    return pl.pallas_call(
        _combine_kernel,
        grid=(nt,),
        in_specs=[pl.BlockSpec((2, tm), lambda i: (0, i), memory_space=pltpu.SMEM),
                  pl.BlockSpec((2, tm), lambda i: (0, jnp.minimum(i + 1, nt - 1)), memory_space=pltpu.SMEM),
                  pl.BlockSpec(memory_space=pl.ANY),
                  pl.BlockSpec((tm, d), lambda i: (i, 0)),
                  pl.BlockSpec((tm, SUBLANES), lambda i: (i, 0)),
                  ms(col=5),
                  pl.BlockSpec((1, d), lambda i: (0, 0))],
        out_specs=pl.BlockSpec((tm, d), lambda i: (i, 0)),
        out_shape=jax.ShapeDtypeStruct((n, d), F32),
        scratch_shapes=[pltpu.VMEM((2, 2, tm * (d // LANES), LANES), F32), pltpu.SemaphoreType.DMA((2,))],
        compiler_params=_params(("arbitrary",), 24),
        name="moe_combine",
    )(pos, pos, y, h, gcol, mods, final_g)


def _moe_layer(h, g, mods, layer, router, router_b, w1, w3, w2, final_g, rows_per_mod):
    n, d = h.shape
    wr = jnp.zeros((d, LANES), F32).at[:, :N_EXPERTS].set(router)
    br = jnp.zeros((1, LANES), F32).at[0, :N_EXPERTS].set(router_b)
    a, tab, gcol, cnt = _route_call(h, g, mods, layer, wr, br, rows_per_mod)
    tm = ROW_TILE_FFN
    n_tiles = (2 * n) // tm + N_EXPERTS
    counts = cnt[0, :N_EXPERTS].astype(I32)
    tiles = (counts + tm - 1) // tm
    ends = jnp.cumsum(tiles)
    off = ((ends - tiles) * tm).astype(I32)
    tile_expert = jnp.minimum(jnp.sum(jnp.arange(n_tiles, dtype=I32)[:, None] >= ends[None, :], axis=1),
                              N_EXPERTS - 1).astype(I32)
    n_used = ends[-1:].astype(I32)
    assert d // LANES == SUBLANES, "one token must fill one (8, 128) tile of the row-tile layout"
    used = ends[-1:] * tm
    fill = jnp.concatenate([off + counts, used, tiles * tm - counts, n_tiles * tm - used]).astype(I32)
    xg, pos = _dispatch_call(off, fill, tab[0:4].astype(I32), a, n_tiles * tm, d // LANES)
    y = _gffn_call(tile_expert, n_used, xg, w1, w3, w2)
    return _combine_call(pos, y, h, gcol, mods, layer, final_g, rows_per_mod)


def kernel(x, c, ctx, c_ctx, ada_w, ada_b, mix_norm_g, ffn_norm_g, w_in, w_out, na_rpb, conv_w, conv_b, lru_wa, lru_ba, lru_wx, lru_bx, lru_lam, ffn_w1, ffn_w3, ffn_w2, moe_router, moe_router_b, moe_w1, moe_w3, moe_w2, final_g):
    batch, seq, d = x.shape
    ctx_len = ctx.shape[1]
    depth = ada_w.shape[0]
    lw = conv_w.shape[-1]
    aw = (w_in.shape[-1] - 2 * lw) // 3
    rows = seq // GRID_W
    ctx_row = batch

    mod_rows = -(-(batch + 1) // SUBLANES) * SUBLANES
    c_all = jnp.zeros((mod_rows, d), F32).at[:batch].set(c).at[ctx_row].set(c_ctx)
    mods = _ada_call(c_all, ada_w, ada_b).reshape(depth, mod_rows, 1, 6 * d)

    bmods_lat = mods[:, :batch]
    bmods_ctx = jnp.broadcast_to(mods[:, ctx_row:ctx_row + 1], bmods_lat.shape)
    flat = lambda a: a.reshape(a.shape[0] * a.shape[1], a.shape[2])
    h_lat, h_ctx = x, ctx
    out = None
    for i in range(depth):
        last = i == depth - 1
        w_in_bf = w_in[i].astype(BF16)
        w_out_bf = w_out[i].astype(BF16)
        g_mix = mix_norm_g[i].reshape(1, d)
        g_ffn = ffn_norm_g[i].reshape(1, d)
        q_l, k_l, v_l, xr_l, gr_l = _in_proj_call(h_lat, g_mix, bmods_lat, i, w_in_bf, aw, lw)
        q_c, k_c, v_c, xr_c, gr_c = _in_proj_call(h_ctx, g_mix, bmods_ctx, i, w_in_bf, aw, lw)
        oa_l = _na_call(flat(q_l), flat(k_l), flat(v_l), flat(k_c), flat(v_c), _pad_rpb(na_rpb[i]), batch)
        wg, bg = _lru_gate_weights(lru_wa[i], lru_ba[i], lru_wx[i], lru_bx[i], LANES)
        ol_l, ol_c = _lru_call(xr_l, gr_l, xr_c, gr_c, conv_w[i], conv_b[i], wg, bg, lru_lam[i])
        oa_l = oa_l.reshape(batch, seq, aw)
        j = i // 2
        if i % 2 == 0:
            ffn_w = (ffn_w1[j], ffn_w3[j], ffn_w2[j])
            h_lat = _proj_ffn_call(oa_l, ol_l, h_lat, g_ffn, bmods_lat, i, w_out_bf, *ffn_w)
            if not last:
                oa_c = _ctx_attn_call(flat(q_c), flat(k_c), flat(v_c), batch).reshape(batch, ctx_len, aw)
                h_ctx = _proj_ffn_call(oa_c, ol_c, h_ctx, g_ffn, bmods_ctx, i, w_out_bf, *ffn_w)
        else:
            assert last, "the routed layer is fused with the final norm"
            h_lat = _out_proj_call(oa_l, ol_l, h_lat, bmods_lat, i, w_out_bf)
            out = _moe_layer(flat(h_lat), g_ffn, mods, i, moe_router[j], moe_router_b[j],
                             moe_w1[j], moe_w3[j], moe_w2[j], final_g.reshape(1, d), seq)
    return out.reshape(batch, seq, d)
```

```python
import functools

import jax
import jax.numpy as jnp
from jax import lax
from jax.experimental import pallas as pl
from jax.experimental.pallas import tpu as pltpu

F32 = jnp.float32
BF16 = jnp.bfloat16
I32 = jnp.int32

GRID_W = 64
HEAD_DIM = 64
N_HEADS = 8
NA_KH = 8
NA_KW = 16
LRU_BLOCK = 64
LRU_C = 8.0
CONV_W = 4
N_EXPERTS = 8
EPS = 1e-6
NEG_INF = -1e30
LOG2E = 1.4426950408889634
LANES = 128
SUBLANES = 8
VMEM_BYTES = 64 * 1024 * 1024

FF_CHUNK = 512
ROW_TILE_FFN = 1024
ROW_TILE = 512
GATHER_TILE = 256


def _params(semantics, vmem_mb):
    return pltpu.CompilerParams(dimension_semantics=semantics,
                                vmem_limit_bytes=min(vmem_mb * 1024 * 1024, VMEM_BYTES - 4 * 1024 * 1024))


def _rms_mod(x, g, sh, sc):
    y = x * lax.rsqrt(jnp.mean(x * x, axis=-1, keepdims=True) + EPS)
    return (y * g) * (1.0 + sc) + sh


def _ada_kernel(c_ref, w_ref, b_ref, o_ref):
    c = c_ref[...]
    s = c * jax.nn.sigmoid(c)
    o_ref[...] = jnp.dot(s, w_ref[...], preferred_element_type=F32,
                         precision=lax.Precision.HIGHEST) + b_ref[...]


def _ada_call(c_all, ada_w, ada_b):
    depth, d, n = ada_w.shape
    rows = c_all.shape[0]
    tn = 1024
    return pl.pallas_call(
        _ada_kernel,
        grid=(depth, n // tn),
        in_specs=[pl.BlockSpec((rows, d), lambda l, j: (0, 0)),
                  pl.BlockSpec((None, d, tn), lambda l, j: (l, 0, j)),
                  pl.BlockSpec((None, 1, tn), lambda l, j: (l, 0, j))],
        out_specs=pl.BlockSpec((None, rows, tn), lambda l, j: (l, 0, j)),
        out_shape=jax.ShapeDtypeStruct((depth, rows, n), F32),
        compiler_params=_params(("arbitrary", "arbitrary"), 24),
        name="ada_mod",
    )(c_all, ada_w, ada_b.reshape(depth, 1, n))


TIME_TILE = 64


def _to_time_major(r, o_ref, nb, tt):
    for c in range(o_ref.shape[0]):
        for b in range(nb):
            o_ref[c, pl.ds(b, tt, stride=SUBLANES), :] = r[b * tt:(b + 1) * tt, c * LANES:(c + 1) * LANES]


def _from_time_major(x_ref, nb, tt):
    return jnp.concatenate(
        [jnp.concatenate([x_ref[c, pl.ds(b, tt, stride=SUBLANES), :] for c in range(x_ref.shape[0])], axis=1)
         for b in range(nb)], axis=0)


def _in_proj_kernel(x_ref, g_ref, sh_ref, sc_ref, w_ref, q_ref, k_ref, v_ref, xr_ref, gr_ref, *, aw, lw):
    nb, tt, d = x_ref.shape
    a = _rms_mod(x_ref[...], g_ref[...], sh_ref[...], sc_ref[...]).astype(BF16).reshape(nb * tt, d)
    r = jnp.dot(a, w_ref[...], preferred_element_type=F32)
    q_ref[...] = (r[:, 0:aw] * (LOG2E * HEAD_DIM ** -0.5)).astype(BF16).reshape(nb, tt, aw)
    k_ref[...] = r[:, aw:2 * aw].astype(BF16).reshape(nb, tt, aw)
    v_ref[...] = r[:, 2 * aw:3 * aw].astype(BF16).reshape(nb, tt, aw)
    _to_time_major(r[:, 3 * aw:3 * aw + lw], xr_ref, nb, tt)
    _to_time_major(r[:, 3 * aw + lw:3 * aw + 2 * lw], gr_ref, nb, tt)


def _mod_spec(d, col, layer, rows_per_mod, tm, fixed_row):
    if fixed_row is None:
        assert rows_per_mod % tm == 0, "a row tile must not straddle two batch elements"
        return pl.BlockSpec((None, None, 1, d), lambda i, *_: (layer, (i * tm) // rows_per_mod, 0, col))
    return pl.BlockSpec((None, None, 1, d), lambda i, *_: (layer, fixed_row, 0, col))


def _batch_mod_spec(nb, d, layer, col):
    return pl.BlockSpec((None, nb, 1, d), lambda j: (layer, 0, 0, col))


def _in_proj_call(x, g, bmods, layer, w_bf, aw, lw):
    nb, t, d = x.shape
    assert nb == SUBLANES, "time-major rows put the batch on the sublanes"
    tt = min(TIME_TILE, t)
    ncol = w_bf.shape[1]
    ncb = lw // LANES
    bt = lambda w: pl.BlockSpec((nb, tt, w), lambda j: (0, j, 0))
    tm = pl.BlockSpec((ncb, tt * SUBLANES, LANES), lambda j: (0, j, 0))
    return pl.pallas_call(
        functools.partial(_in_proj_kernel, aw=aw, lw=lw),
        grid=(t // tt,),
        in_specs=[bt(d), pl.BlockSpec((1, d), lambda j: (0, 0)),
                  _batch_mod_spec(nb, d, layer, 0), _batch_mod_spec(nb, d, layer, 1),
                  pl.BlockSpec((d, ncol), lambda j: (0, 0))],
        out_specs=[bt(aw), bt(aw), bt(aw), tm, tm],
        out_shape=[jax.ShapeDtypeStruct((nb, t, aw), BF16)] * 3
        + [jax.ShapeDtypeStruct((ncb, t * SUBLANES, LANES), F32)] * 2,
        compiler_params=_params(("arbitrary",), 40),
        name="in_proj",
    )(x, g, bmods, bmods, w_bf)


def _softmax_pv(parts):
    m = None
    for s, _ in parts:
        mi = jnp.max(s, axis=-1, keepdims=True)
        m = mi if m is None else jnp.maximum(m, mi)
    l = None
    acc = None
    for s, v in parts:
        p = jnp.exp2(s - m)
        li = jnp.sum(p, axis=-1, keepdims=True)
        ai = jnp.dot(p.astype(BF16), v, preferred_element_type=F32)
        l = li if l is None else l + li
        acc = ai if acc is None else acc + ai
    return acc * (1.0 / l)


def _qk(q, k):
    return lax.dot_general(q, k, (((1,), (1,)), ((), ())), preferred_element_type=F32)


def _build_na_bias(rpb_ref, bias_ref):
    lane = lax.broadcasted_iota(I32, (GRID_W, LANES), 1)
    q = lax.broadcasted_iota(I32, (GRID_W, LANES), 0)
    kcol = lane & (GRID_W - 1)
    cs = jnp.clip(q - NA_KW // 2, 0, GRID_W - NA_KW)
    ok = (kcol >= cs) & (kcol < cs + NA_KW)
    low = lane < GRID_W
    for delta in range(NA_KH):
        for h in range(N_HEADS):
            for jp in range(NA_KH // 2):
                halves = []
                for j in (2 * jp, 2 * jp + 1):
                    dr = j - delta + NA_KH - 1
                    w = jnp.broadcast_to(rpb_ref[h, dr:dr + 1, :], (GRID_W, LANES))
                    base = (j % 2) * GRID_W - (NA_KW - 1)
                    halves.append(pltpu.roll(w, base % LANES, 1, stride=1, stride_axis=0))
                t = jnp.where(low, halves[0], halves[1])
                bias_ref[delta, h, :, jp * LANES:(jp + 1) * LANES] = jnp.where(ok, t * LOG2E, NEG_INF)


NA_ROWS_PER_STEP = 4


def _stack_pair(q2):
    first = lax.broadcasted_iota(I32, q2.shape, 1) < HEAD_DIM
    zero = jnp.zeros_like(q2)
    return jnp.concatenate([jnp.where(first, q2, zero), jnp.where(first, zero, q2)], axis=0)


def _unstack_pair(o2):
    nq = o2.shape[0] // 2
    first = lax.broadcasted_iota(I32, (nq, o2.shape[1]), 1) < HEAD_DIM
    return jnp.where(first, o2[:nq], o2[nq:])


def _na_kernel(q_ref, kl_ref, vl_ref, kc_ref, vc_ref, rpb_ref, o_ref, bias_ref, s_ref, p_ref, *, rows, rb):
    @pl.when((pl.program_id(0) == 0) & (pl.program_id(1) == 0))
    def _():
        _build_na_bias(rpb_ref, bias_ref)

    nwin = NA_KH * GRID_W
    npair = N_HEADS // 2
    for rho in range(rb):
        r = pl.program_id(1) * rb + rho
        rs = jnp.clip(r - NA_KH // 2, 0, rows - NA_KH)
        delta = r - rs
        k0 = pl.multiple_of(rs * GRID_W, GRID_W)
        qrows = slice(rho * GRID_W, (rho + 1) * GRID_W)
        for pair in range(npair):
            sl = slice(pair * 2 * HEAD_DIM, (pair + 1) * 2 * HEAD_DIM)
            keys = jnp.concatenate([kl_ref[pl.ds(k0, nwin), sl], kc_ref[:, sl]], axis=0)
            s = _qk(_stack_pair(q_ref[qrows, sl]), keys)
            for sub in range(2):
                h = pair * 2 + sub
                rr = slice(sub * GRID_W, (sub + 1) * GRID_W)
                s_ref[rho, h, :, 0:nwin] = s[rr, 0:nwin] + bias_ref[delta, h]
                s_ref[rho, h, :, nwin:] = s[rr, nwin:]
        s = s_ref[rho]
        p = jnp.exp2(s - jnp.max(s, axis=-1, keepdims=True))
        inv = 1.0 / jnp.sum(p, axis=-1, keepdims=True)
        p_ref[rho] = p.astype(BF16)
        for pair in range(npair):
            sl = slice(pair * 2 * HEAD_DIM, (pair + 1) * 2 * HEAD_DIM)
            vals = jnp.concatenate([vl_ref[pl.ds(k0, nwin), sl], vc_ref[:, sl]], axis=0)
            p2 = p_ref[rho, 2 * pair:2 * pair + 2].reshape(2 * GRID_W, p_ref.shape[-1])
            o2 = jnp.dot(p2, vals, preferred_element_type=F32) * inv[2 * pair:2 * pair + 2].reshape(2 * GRID_W, 1)
            o_ref[qrows, sl] = _unstack_pair(o2).astype(o_ref.dtype)


def _na_call(q, kl, vl, kc, vc, rpb, batch):
    s = q.shape[0] // batch
    l = kc.shape[0] // batch
    w = q.shape[1]
    rows = s // GRID_W
    assert rows >= NA_KH
    rb = NA_ROWS_PER_STEP if rows % NA_ROWS_PER_STEP == 0 else 1
    nkeys = NA_KH * GRID_W + l
    nrb = rows // rb
    return pl.pallas_call(
        functools.partial(_na_kernel, rows=rows, rb=rb),
        grid=(batch, nrb),
        in_specs=[pl.BlockSpec((rb * GRID_W, w), lambda b, r: (b * nrb + r, 0)),
                  pl.BlockSpec((s, w), lambda b, r: (b, 0)),
                  pl.BlockSpec((s, w), lambda b, r: (b, 0)),
                  pl.BlockSpec((l, w), lambda b, r: (b, 0)),
                  pl.BlockSpec((l, w), lambda b, r: (b, 0)),
                  pl.BlockSpec(rpb.shape, lambda b, r: (0, 0, 0))],
        out_specs=pl.BlockSpec((rb * GRID_W, w), lambda b, r: (b * nrb + r, 0)),
        out_shape=jax.ShapeDtypeStruct(q.shape, BF16),
        scratch_shapes=[pltpu.VMEM((NA_KH, N_HEADS, GRID_W, NA_KH * GRID_W), F32),
                        pltpu.VMEM((rb, N_HEADS, GRID_W, nkeys), F32),
                        pltpu.VMEM((rb, N_HEADS, GRID_W, nkeys), BF16)],
        compiler_params=_params(("arbitrary", "arbitrary"), 48),
        name="na_attention",
    )(q, kl, vl, kc, vc, rpb)


def _pad_rpb(rpb):
    h, nr, nc = rpb.shape
    return jnp.zeros((h, 2 * NA_KH, LANES), F32).at[:, :nr, :nc].set(rpb.astype(F32))


def _ctx_attn_kernel(q_ref, k_ref, v_ref, o_ref):
    rows = q_ref.shape[0]
    lane = lax.broadcasted_iota(I32, (rows, 2 * HEAD_DIM), 1)
    first = lane < HEAD_DIM
    for pair in range(N_HEADS // 2):
        sl = slice(pair * 2 * HEAD_DIM, (pair + 1) * 2 * HEAD_DIM)
        q2 = q_ref[:, sl]
        k = k_ref[:, sl]
        v = v_ref[:, sl]
        outs = []
        for sub in range(2):
            qh = jnp.where(first if sub == 0 else ~first, q2, jnp.zeros_like(q2))
            outs.append(_softmax_pv([(_qk(qh, k), v)]))
        o_ref[:, sl] = jnp.where(first, outs[0], outs[1]).astype(o_ref.dtype)


def _ctx_attn_call(q, k, v, batch):
    l = q.shape[0] // batch
    w = q.shape[1]
    spec = pl.BlockSpec((l, w), lambda b: (b, 0))
    return pl.pallas_call(
        _ctx_attn_kernel,
        grid=(batch,),
        in_specs=[spec, spec, spec],
        out_specs=spec,
        out_shape=jax.ShapeDtypeStruct(q.shape, BF16),
        compiler_params=_params(("arbitrary",), 24),
        name="ctx_attention",
    )(q, k, v)


def _softplus(x):
    return jnp.maximum(x, 0.0) + jnp.log1p(jnp.exp(-jnp.abs(x)))


LRU_TIME_CHUNK = 64


def _lru_kernel(xl_hbm, xc_hbm, gl_ref, gc_ref, cw_ref, cb_ref, wg_ref, bg_ref, lam_ref, ol_ref, oc_ref,
                padl_ref, padc_ref, sem, *, tchunk):
    c = pl.program_id(0)
    slot = c % 2
    nb = SUBLANES
    halo_lo, halo_hi = nb, 2 * nb
    zero = jnp.zeros((nb, LANES), F32)

    def fetches(block, slot):
        return [pltpu.make_async_copy(x_hbm.at[block], pad_ref.at[slot, pl.ds(halo_lo, x_hbm.shape[1])], sem.at[slot, k])
                for k, (x_hbm, pad_ref) in enumerate(((xc_hbm, padc_ref), (xl_hbm, padl_ref)))]

    @pl.when(c == 0)
    def _():
        for x_hbm, pad_ref in ((xc_hbm, padc_ref), (xl_hbm, padl_ref)):
            n = x_hbm.shape[1]
            for s in range(2):
                pad_ref[s, 0:halo_lo, :] = zero
                pad_ref[s, halo_lo + n:halo_lo + n + halo_hi, :] = jnp.zeros((halo_hi, LANES), F32)
        for cp in fetches(0, 0):
            cp.start()

    @pl.when(c + 1 < pl.num_programs(0))
    def _():
        for cp in fetches(c + 1, 1 - slot):
            cp.start()

    for cp in fetches(c, slot):
        cp.wait()

    la = [(-0.5 * LRU_C * LOG2E) * _softplus(-lam_ref[d:d + 1, :]) for d in range(2)]
    cb = cb_ref[...]
    cw = [cw_ref[j:j + 1, :] for j in range(CONV_W)]
    rows = tchunk * nb

    def segment(pad_ref, g_ref, o_ref, d, h):
        n = o_ref.shape[0] // rows

        def body(i, h):
            r0 = pl.multiple_of((i if d == 0 else n - 1 - i) * rows, rows)
            u = cb
            for j in range(CONV_W):
                u = u + pad_ref[slot, pl.ds(r0 + j * nb, rows), :] * cw[j]
            ub = u.astype(BF16)
            ta = jnp.tanh(jnp.dot(ub, wg_ref[d, 0], preferred_element_type=F32) + bg_ref[d, 0])
            ti = jnp.tanh(jnp.dot(ub, wg_ref[d, 1], preferred_element_type=F32) + bg_ref[d, 1])
            a = jnp.exp2(la[d] * ta + la[d])
            z = 1.0 - a * a
            b = jnp.where(z > 0.0, z * lax.rsqrt(z), 0.0) * (0.5 * ti + 0.5) * u
            hs = [None] * tchunk
            for t in (range(tchunk) if d == 0 else reversed(range(tchunk))):
                h = a[t * nb:(t + 1) * nb] * h + b[t * nb:(t + 1) * nb]
                hs[t] = h
            hcat = jnp.concatenate(hs, axis=0)
            if d == 0:
                o_ref[pl.ds(r0, rows), :] = hcat
            else:
                y = o_ref[pl.ds(r0, rows), :] + hcat
                o_ref[pl.ds(r0, rows), :] = jax.nn.gelu(g_ref[pl.ds(r0, rows), :]) * y
            return h

        return lax.fori_loop(0, n, body, h)

    for d in range(2):
        h = segment(padc_ref, gc_ref, oc_ref, d, zero)
        segment(padl_ref, gl_ref, ol_ref, d, h)


def _lru_call(xr_l, gr_l, xr_c, gr_c, conv_w, conv_b, wg, bg, lam):
    ncb, rl, _ = xr_l.shape
    rc = xr_c.shape[1]
    tchunk = min(LRU_TIME_CHUNK, rl // SUBLANES, rc // SUBLANES)
    halo = 3 * SUBLANES
    lat = pl.BlockSpec((None, rl, LANES), lambda c: (c, 0, 0))
    ctx = pl.BlockSpec((None, rc, LANES), lambda c: (c, 0, 0))
    per_block = lambda a: jnp.moveaxis(a.reshape(a.shape[:-1] + (ncb, LANES)), -2, 0)
    return pl.pallas_call(
        functools.partial(_lru_kernel, tchunk=tchunk),
        grid=(ncb,),
        in_specs=[pl.BlockSpec(memory_space=pl.ANY), pl.BlockSpec(memory_space=pl.ANY),
                  lat, ctx,
                  pl.BlockSpec((None, CONV_W, LANES), lambda c: (c, 0, 0)),
                  pl.BlockSpec((None, 1, LANES), lambda c: (c, 0, 0)),
                  pl.BlockSpec((2, 2, None, LANES, LANES), lambda c: (0, 0, c, 0, 0)),
                  pl.BlockSpec((None, 2, 2, 1, LANES), lambda c: (c, 0, 0, 0, 0)),
                  pl.BlockSpec((None, 2, LANES), lambda c: (c, 0, 0))],
        out_specs=[lat, ctx],
        out_shape=[jax.ShapeDtypeStruct(xr_l.shape, F32), jax.ShapeDtypeStruct(xr_c.shape, F32)],
        scratch_shapes=[pltpu.VMEM((2, rl + halo, LANES), F32), pltpu.VMEM((2, rc + halo, LANES), F32),
                        pltpu.SemaphoreType.DMA((2, 2))],
        compiler_params=_params(("arbitrary",), 58),
        name="rglru",
    )(xr_l, xr_c, gr_l, gr_c, per_block(conv_w), per_block(conv_b.reshape(1, -1)), wg,
      per_block(bg[:, :, None, :]), per_block(lam))


def _lru_gate_weights(wa, ba, wx, bx, half):
    nblk = wa.shape[1]
    lw = nblk * LRU_BLOCK
    per = half // LRU_BLOCK

    def dense(w):
        w = w.reshape(2, nblk // per, per, LRU_BLOCK, LRU_BLOCK)
        eye = jnp.eye(per, dtype=w.dtype)
        full = w[:, :, :, :, None, :] * eye[None, None, :, None, :, None]
        return full.reshape(2, nblk // per, half, half)

    wg = (0.5 * jnp.stack([dense(wa), dense(wx)], axis=1)).astype(BF16)
    bg = 0.5 * jnp.stack([ba, bx], axis=1).astype(F32)
    return wg, bg


def _out_proj_kernel(oa_ref, ol_ref, h_ref, g1_ref, w_ref, o_ref, *, aw):
    o_ref[...] = _mix_proj(oa_ref, ol_ref, h_ref, g1_ref, w_ref, aw)


def _out_proj_call(oa, ol, h, bmods, layer, w_bf):
    nb, t, d = h.shape
    aw = oa.shape[2]
    ncb = ol.shape[0]
    tt = min(TIME_TILE, t)
    bt = lambda w: pl.BlockSpec((nb, tt, w), lambda j: (0, j, 0))
    return pl.pallas_call(
        functools.partial(_out_proj_kernel, aw=aw),
        grid=(t // tt,),
        in_specs=[bt(aw), pl.BlockSpec((ncb, tt * SUBLANES, LANES), lambda j: (0, j, 0)), bt(d),
                  _batch_mod_spec(nb, d, layer, 2), pl.BlockSpec(w_bf.shape, lambda j: (0, 0))],
        out_specs=bt(d),
        out_shape=jax.ShapeDtypeStruct((nb, t, d), F32),
        compiler_params=_params(("arbitrary",), 32),
        name="out_proj",
    )(oa, ol, h, bmods, w_bf)


def _swiglu_step(a_bf, w1_ref, w3_ref, w2_ref, acc_ref):
    g = jnp.dot(a_bf, w1_ref[...].astype(BF16), preferred_element_type=F32)
    u = jnp.dot(a_bf, w3_ref[...].astype(BF16), preferred_element_type=F32)
    hmid = (g * jax.nn.sigmoid(g) * u).astype(BF16)
    acc_ref[...] += jnp.dot(hmid, w2_ref[...].astype(BF16), preferred_element_type=F32)


def _swiglu_streamed(a_ref, acc_ref, w_cur, w_next, first, has_next, tile, bufs, sem):
    w1b, w3b, w2b = bufs
    fc = w1b.shape[2]
    nc = w_cur[0].shape[1] // fc

    def copies(w, c, slot):
        return (pltpu.make_async_copy(w[0].at[:, pl.ds(c * fc, fc)], w1b.at[slot], sem.at[0, slot]),
                pltpu.make_async_copy(w[1].at[:, pl.ds(c * fc, fc)], w3b.at[slot], sem.at[1, slot]),
                pltpu.make_async_copy(w[2].at[pl.ds(c * fc, fc), :], w2b.at[slot], sem.at[2, slot]))

    @pl.when(first)
    def _():
        for cp in copies(w_cur, 0, (tile * nc) % 2):
            cp.start()

    acc_ref[...] = jnp.zeros_like(acc_ref)
    for c in range(nc):
        slot = (tile * nc + c) % 2
        if c + 1 < nc:
            for cp in copies(w_cur, c + 1, 1 - slot):
                cp.start()
        else:
            @pl.when(has_next)
            def _():
                for cp in copies(w_next, 0, 1 - slot):
                    cp.start()
        for cp in copies(w_cur, c, slot):
            cp.wait()
        _swiglu_step(a_ref[...], w1b.at[slot], w3b.at[slot], w2b.at[slot], acc_ref)


def _swiglu_buffers(d, dtype):
    return [pltpu.VMEM((2, d, FF_CHUNK), dtype), pltpu.VMEM((2, d, FF_CHUNK), dtype),
            pltpu.VMEM((2, FF_CHUNK, d), dtype), pltpu.SemaphoreType.DMA((3, 2))]


def _mix_proj(oa_ref, ol_ref, h_ref, g1_ref, wo_ref, aw):
    nb, tt, d = h_ref.shape
    o = jnp.dot(oa_ref[...].reshape(nb * tt, aw), wo_ref[0:aw, :], preferred_element_type=F32)
    ol = _from_time_major(ol_ref, nb, tt).astype(BF16)
    o = o + jnp.dot(ol, wo_ref[aw:, :], preferred_element_type=F32)
    return h_ref[...] + g1_ref[...] * o.reshape(nb, tt, d)


def _proj_ffn_kernel(oa_ref, ol_ref, h_ref, g1_ref, wo_ref, g_ref, sh_ref, sc_ref, g2_ref,
                     w1_hbm, w3_hbm, w2_hbm, o_ref, a_ref, acc_ref, w1b, w3b, w2b, sem, *, aw):
    j = pl.program_id(0)
    nb, tt, d = h_ref.shape
    h1 = _mix_proj(oa_ref, ol_ref, h_ref, g1_ref, wo_ref, aw)
    o_ref[...] = h1
    a_ref[...] = _rms_mod(h1, g_ref[...], sh_ref[...], sc_ref[...]).astype(BF16).reshape(nb * tt, d)
    w = (w1_hbm, w3_hbm, w2_hbm)
    _swiglu_streamed(a_ref, acc_ref, w, w, j == 0, j + 1 < pl.num_programs(0), j, (w1b, w3b, w2b), sem)
    o_ref[...] += g2_ref[...] * acc_ref[...].reshape(nb, tt, d)


def _proj_ffn_call(oa, ol, h, g, bmods, layer, wo_bf, w1, w3, w2):
    nb, t, d = h.shape
    aw = oa.shape[2]
    ncb = ol.shape[0]
    dff = w1.shape[1]
    tt = min(ROW_TILE_FFN // nb, t)
    assert dff % FF_CHUNK == 0
    bt = lambda w: pl.BlockSpec((nb, tt, w), lambda j: (0, j, 0))
    bm = lambda col: pl.BlockSpec((None, nb, 1, d), lambda j: (layer, 0, 0, col))
    hbm = pl.BlockSpec(memory_space=pl.ANY)
    return pl.pallas_call(
        functools.partial(_proj_ffn_kernel, aw=aw),
        grid=(t // tt,),
        in_specs=[bt(aw), pl.BlockSpec((ncb, tt * SUBLANES, LANES), lambda j: (0, j, 0)), bt(d), bm(2),
                  pl.BlockSpec(wo_bf.shape, lambda j: (0, 0), pipeline_mode=pl.Buffered(1)),
                  pl.BlockSpec((1, d), lambda j: (0, 0)), bm(3), bm(4), bm(5), hbm, hbm, hbm],
        out_specs=bt(d),
        out_shape=jax.ShapeDtypeStruct((nb, t, d), F32),
        scratch_shapes=[pltpu.VMEM((nb * tt, d), BF16), pltpu.VMEM((nb * tt, d), F32)]
        + _swiglu_buffers(d, w1.dtype),
        compiler_params=_params(("arbitrary",), 58),
        name="proj_ffn",
    )(oa, ol, h, bmods, wo_bf, g, bmods, bmods, bmods, w1, w3, w2)


def _to_row_tiles(x, o_ref):
    t, d = x.shape
    ns = d // LANES
    for s in range(ns):
        o_ref[pl.ds(s, t, stride=ns), :] = x[:, s * LANES:(s + 1) * LANES]


def _from_row_tiles(x_ref, idx, t, ns):
    return jnp.concatenate([x_ref[idx + (pl.ds(s, t, stride=ns), slice(None))] for s in range(ns)], axis=1)


def _gffn_kernel(te_ref, nu_ref, x_ref, w1_hbm, w3_hbm, w2_hbm, o_ref, a_ref, acc_ref, w1b, w3b, w2b, sem):
    i = pl.program_id(0)
    tm, d = a_ref.shape
    nu = nu_ref[0]

    @pl.when(i < nu)
    def _():
        a_ref[...] = _from_row_tiles(x_ref, (), tm, d // LANES).astype(BF16)
        e = te_ref[i]
        e_next = te_ref[jnp.minimum(i + 1, pl.num_programs(0) - 1)]
        _swiglu_streamed(a_ref, acc_ref, (w1_hbm.at[e], w3_hbm.at[e], w2_hbm.at[e]),
                         (w1_hbm.at[e_next], w3_hbm.at[e_next], w2_hbm.at[e_next]),
                         i == 0, i + 1 < nu, i, (w1b, w3b, w2b), sem)
        _to_row_tiles(acc_ref[...], o_ref)

    @pl.when(i >= nu)
    def _():
        o_ref[...] = jnp.zeros_like(o_ref)


def _gffn_call(tile_expert, n_used, xg, w1, w3, w2):
    d = w1.shape[1]
    ns = d // LANES
    rg = xg.shape[0] // ns
    dff = w1.shape[2]
    tm = ROW_TILE_FFN
    assert dff % FF_CHUNK == 0

    def tile(i, nu):
        return jnp.maximum(jnp.minimum(i, nu[0] - 1), 0)

    hbm = pl.BlockSpec(memory_space=pl.ANY)
    grid_spec = pltpu.PrefetchScalarGridSpec(
        num_scalar_prefetch=2,
        grid=(rg // tm,),
        in_specs=[pl.BlockSpec((tm * ns, LANES), lambda i, te, nu: (tile(i, nu), 0)), hbm, hbm, hbm],
        out_specs=pl.BlockSpec((tm * ns, LANES), lambda i, te, nu: (i, 0)),
        scratch_shapes=[pltpu.VMEM((tm, d), BF16), pltpu.VMEM((tm, d), F32)] + _swiglu_buffers(d, w1.dtype))
    return pl.pallas_call(
        _gffn_kernel,
        grid_spec=grid_spec,
        out_shape=jax.ShapeDtypeStruct((rg * ns, LANES), F32),
        compiler_params=_params(("arbitrary",), 56),
        name="ffn_grouped",
    )(tile_expert, n_used, xg, w1, w3, w2)


def _route_kernel(x_ref, g_ref, sh_ref, sc_ref, wr_ref, br_ref, a_ref, tab_ref, gcol_ref, cnt_ref, carry_ref):
    i = pl.program_id(0)
    tm = x_ref.shape[0]

    @pl.when(i == 0)
    def _():
        carry_ref[...] = jnp.zeros_like(carry_ref)

    a = _rms_mod(x_ref[...], g_ref[...], sh_ref[...], sc_ref[...])
    _to_row_tiles(a, a_ref)
    lane = lax.broadcasted_iota(I32, (tm, LANES), 1).astype(F32)
    logits = jnp.dot(a, wr_ref[...], preferred_element_type=F32, precision=lax.Precision.HIGHEST) + br_ref[...]
    logits = jnp.where(lane < N_EXPERTS, logits, NEG_INF)
    m1 = jnp.max(logits, axis=-1, keepdims=True)
    i1 = jnp.min(jnp.where(logits == m1, lane, float(LANES)), axis=-1, keepdims=True)
    rest = jnp.where(lane == i1, 2.0 * NEG_INF, logits)
    m2 = jnp.max(rest, axis=-1, keepdims=True)
    i2 = jnp.min(jnp.where(rest == m2, lane, float(LANES)), axis=-1, keepdims=True)
    e = jnp.exp(m2 - m1)
    w1 = 1.0 / (1.0 + e)
    w2 = e * w1
    sel1 = lane == i1
    sel2 = lane == i2
    onehot = jnp.where(sel1 | sel2, 1.0, 0.0)
    row = lax.broadcasted_iota(I32, (tm, tm), 0)
    col = lax.broadcasted_iota(I32, (tm, tm), 1)
    before = jnp.where(col < row, 1.0, 0.0).astype(BF16)
    cum = jnp.dot(before, onehot.astype(BF16), preferred_element_type=F32) + carry_ref[...]
    r1 = jnp.sum(jnp.where(sel1, cum, 0.0), axis=-1, keepdims=True)
    r2 = jnp.sum(jnp.where(sel2, cum, 0.0), axis=-1, keepdims=True)
    carry_ref[...] += jnp.sum(onehot, axis=0, keepdims=True)
    cnt_ref[...] = carry_ref[...]
    cols = (jnp.where(lane == 0, i1, 0.0) + jnp.where(lane == 1, i2, 0.0)
            + jnp.where(lane == 2, r1, 0.0) + jnp.where(lane == 3, r2, 0.0))
    tab_ref[...] = cols.T[0:SUBLANES, :]
    gcol_ref[...] = (jnp.where(lane == 0, w1, 0.0) + jnp.where(lane == 1, w2, 0.0))[:, 0:SUBLANES]


def _route_call(h, g, mods, layer, wr, br, rows_per_mod):
    n, d = h.shape
    tm = min(ROW_TILE, n)
    ms = functools.partial(_mod_spec, d, layer=layer, rows_per_mod=rows_per_mod, tm=tm, fixed_row=None)
    return pl.pallas_call(
        _route_kernel,
        grid=(n // tm,),
        in_specs=[pl.BlockSpec((tm, d), lambda i: (i, 0)),
                  pl.BlockSpec((1, d), lambda i: (0, 0)), ms(col=3), ms(col=4),
                  pl.BlockSpec((d, LANES), lambda i: (0, 0)),
                  pl.BlockSpec((1, LANES), lambda i: (0, 0))],
        out_specs=[pl.BlockSpec((tm * (d // LANES), LANES), lambda i: (i, 0)),
                   pl.BlockSpec((SUBLANES, tm), lambda i: (0, i)),
                   pl.BlockSpec((tm, SUBLANES), lambda i: (i, 0)),
                   pl.BlockSpec((1, LANES), lambda i: (0, 0))],
        out_shape=[jax.ShapeDtypeStruct((n * (d // LANES), LANES), F32),
                   jax.ShapeDtypeStruct((SUBLANES, n), F32),
                   jax.ShapeDtypeStruct((n, SUBLANES), F32),
                   jax.ShapeDtypeStruct((1, LANES), F32)],
        scratch_shapes=[pltpu.VMEM((1, LANES), F32)],
        compiler_params=_params(("arbitrary",), 40),
        name="moe_route",
    )(h, g, mods, mods, wr, br)


DMA_UNROLL = 4


FILL_CHUNK = 64


def _dispatch_kernel(off_ref, fill_ref, tab_ref, a_ref, xg_out, pos_ref, sem, *, ns):
    tm = tab_ref.shape[1]

    def rows_copy(src, dst, nrows):
        return pltpu.make_async_copy(a_ref.at[pl.ds(pl.multiple_of(src * ns, ns), nrows * ns)],
                                     xg_out.at[pl.ds(pl.multiple_of(dst * ns, ns), nrows * ns)], sem)

    def row_copy(src, dst):
        return rows_copy(src, dst, 1)

    @pl.when(pl.program_id(0) == pl.num_programs(0) - 1)
    def _():
        nfill = fill_ref.shape[0] // 2
        for e in range(nfill):
            start = fill_ref[e]
            length = fill_ref[nfill + e]
            nbig = length // FILL_CHUNK
            nsmall = length - nbig * FILL_CHUNK
            small0 = start + nbig * FILL_CHUNK

            def big(j, carry, start=start):
                rows_copy(0, start + j * FILL_CHUNK, FILL_CHUNK).start()
                return carry

            def small(j, carry, small0=small0):
                row_copy(0, small0 + j).start()
                return carry

            lax.fori_loop(0, nbig, big, 0)
            lax.fori_loop(0, nsmall, small, 0)
            lax.fori_loop(0, nbig, lambda j, c: (rows_copy(0, 0, FILL_CHUNK).wait(), c)[1], 0)
            lax.fori_loop(0, nsmall, lambda j, c: (row_copy(0, 0).wait(), c)[1], 0)

    def issue(j, carry):
        for u in range(DMA_UNROLL):
            t = j * DMA_UNROLL + u
            for k in range(2):
                p = off_ref[tab_ref[k, t]] + tab_ref[2 + k, t]
                pos_ref[k, t] = p
                row_copy(t, p).start()
        return carry

    lax.fori_loop(0, tm // DMA_UNROLL, issue, 0)

    def drain(j, carry):
        for _ in range(2 * DMA_UNROLL):
            row_copy(0, 0).wait()
        return carry

    lax.fori_loop(0, tm // DMA_UNROLL, drain, 0)


def _dispatch_call(off, fill, tab_i, a, rows_g, ns):
    n = a.shape[0] // ns
    tm = min(ROW_TILE, n)
    assert tm >= FILL_CHUNK
    grid_spec = pltpu.PrefetchScalarGridSpec(
        num_scalar_prefetch=2,
        grid=(n // tm,),
        in_specs=[pl.BlockSpec((4, tm), lambda i, off, fill: (0, i), memory_space=pltpu.SMEM),
                  pl.BlockSpec((tm * ns, LANES), lambda i, off, fill: (i, 0))],
        out_specs=[pl.BlockSpec(memory_space=pl.ANY),
                   pl.BlockSpec((2, tm), lambda i, off, fill: (0, i), memory_space=pltpu.SMEM)],
        scratch_shapes=[pltpu.SemaphoreType.DMA(())])
    return pl.pallas_call(
        functools.partial(_dispatch_kernel, ns=ns),
        grid_spec=grid_spec,
        out_shape=[jax.ShapeDtypeStruct((rows_g * ns, LANES), a.dtype), jax.ShapeDtypeStruct((2, n), I32)],
        compiler_params=_params(("arbitrary",), 16),
        name="moe_dispatch",
    )(off, fill, tab_i, a)


def _combine_kernel(pos_ref, posn_ref, y_hbm, h_ref, gcol_ref, g2_ref, fg_ref, o_ref, buf_ref, sem):
    i = pl.program_id(0)
    n = pl.num_programs(0)
    tm, d = h_ref.shape
    ns = d // LANES

    def row_copy(src, slot, k, t):
        return pltpu.make_async_copy(y_hbm.at[pl.ds(pl.multiple_of(src * ns, ns), ns)],
                                     buf_ref.at[slot, k, pl.ds(pl.multiple_of(t * ns, ns), ns)], sem.at[slot])

    def issue(p_ref, slot):
        def body(j, carry):
            for u in range(DMA_UNROLL):
                t = j * DMA_UNROLL + u
                for k in range(2):
                    row_copy(p_ref[k, t], slot, k, t).start()
            return carry
        lax.fori_loop(0, tm // DMA_UNROLL, body, 0)

    @pl.when(i == 0)
    def _():
        issue(pos_ref, 0)

    @pl.when(i + 1 < n)
    def _():
        issue(posn_ref, (i + 1) % 2)

    slot = i % 2

    def drain(j, carry):
        for _ in range(2 * DMA_UNROLL):
            row_copy(0, slot, 0, 0).wait()
        return carry

    lax.fori_loop(0, tm // DMA_UNROLL, drain, 0)
    y = (gcol_ref[:, 0:1] * _from_row_tiles(buf_ref, (slot, 0), tm, ns)
         + gcol_ref[:, 1:2] * _from_row_tiles(buf_ref, (slot, 1), tm, ns))
    hn = h_ref[...] + g2_ref[...] * y
    o_ref[...] = hn * lax.rsqrt(jnp.mean(hn * hn, axis=-1, keepdims=True) + EPS) * fg_ref[...]


def _combine_call(pos, y, h, gcol, mods, layer, final_g, rows_per_mod):
    n, d = h.shape
    tm = min(GATHER_TILE, n)
    nt = n // tm
    ms = functools.partial(_mod_spec, d, layer=layer, rows_per_mod=rows_per_mod, tm=tm, fixed_row=None)
    return pl.pallas_call(
        _combine_kernel,
        grid=(nt,),
        in_specs=[pl.BlockSpec((2, tm), lambda i: (0, i), memory_space=pltpu.SMEM),
                  pl.BlockSpec((2, tm), lambda i: (0, jnp.minimum(i + 1, nt - 1)), memory_space=pltpu.SMEM),
                  pl.BlockSpec(memory_space=pl.ANY),
                  pl.BlockSpec((tm, d), lambda i: (i, 0)),
                  pl.BlockSpec((tm, SUBLANES), lambda i: (i, 0)),
                  ms(col=5),
                  pl.BlockSpec((1, d), lambda i: (0, 0))],
        out_specs=pl.BlockSpec((tm, d), lambda i: (i, 0)),
        out_shape=jax.ShapeDtypeStruct((n, d), F32),
        scratch_shapes=[pltpu.VMEM((2, 2, tm * (d // LANES), LANES), F32), pltpu.SemaphoreType.DMA((2,))],
        compiler_params=_params(("arbitrary",), 24),
        name="moe_combine",
    )(pos, pos, y, h, gcol, mods, final_g)


def _moe_layer(h, g, mods, layer, router, router_b, w1, w3, w2, final_g, rows_per_mod):
    n, d = h.shape
    wr = jnp.zeros((d, LANES), F32).at[:, :N_EXPERTS].set(router)
    br = jnp.zeros((1, LANES), F32).at[0, :N_EXPERTS].set(router_b)
    a, tab, gcol, cnt = _route_call(h, g, mods, layer, wr, br, rows_per_mod)
    tm = ROW_TILE_FFN
    n_tiles = (2 * n) // tm + N_EXPERTS
    counts = cnt[0, :N_EXPERTS].astype(I32)
    tiles = (counts + tm - 1) // tm
    ends = jnp.cumsum(tiles)
    off = ((ends - tiles) * tm).astype(I32)
    tile_expert = jnp.minimum(jnp.sum(jnp.arange(n_tiles, dtype=I32)[:, None] >= ends[None, :], axis=1),
                              N_EXPERTS - 1).astype(I32)
    n_used = ends[-1:].astype(I32)
    assert d // LANES == SUBLANES, "one token must fill one (8, 128) tile of the row-tile layout"
    used = ends[-1:] * tm
    fill = jnp.concatenate([off + counts, used, tiles * tm - counts, n_tiles * tm - used]).astype(I32)
    xg, pos = _dispatch_call(off, fill, tab[0:4].astype(I32), a, n_tiles * tm, d // LANES)
    y = _gffn_call(tile_expert, n_used, xg, w1, w3, w2)
    return _combine_call(pos, y, h, gcol, mods, layer, final_g, rows_per_mod)


def kernel(x, c, ctx, c_ctx, ada_w, ada_b, mix_norm_g, ffn_norm_g, w_in, w_out, na_rpb, conv_w, conv_b, lru_wa, lru_ba, lru_wx, lru_bx, lru_lam, ffn_w1, ffn_w3, ffn_w2, moe_router, moe_router_b, moe_w1, moe_w3, moe_w2, final_g):
    batch, seq, d = x.shape
    ctx_len = ctx.shape[1]
    depth = ada_w.shape[0]
    lw = conv_w.shape[-1]
    aw = (w_in.shape[-1] - 2 * lw) // 3
    rows = seq // GRID_W
    ctx_row = batch

    mod_rows = -(-(batch + 1) // SUBLANES) * SUBLANES
    c_all = jnp.zeros((mod_rows, d), F32).at[:batch].set(c).at[ctx_row].set(c_ctx)
    mods = _ada_call(c_all, ada_w, ada_b).reshape(depth, mod_rows, 1, 6 * d)

    bmods_lat = mods[:, :batch]
    bmods_ctx = jnp.broadcast_to(mods[:, ctx_row:ctx_row + 1], bmods_lat.shape)
    flat = lambda a: a.reshape(a.shape[0] * a.shape[1], a.shape[2])
    h_lat, h_ctx = x, ctx
    out = None
    for i in range(depth):
        last = i == depth - 1
        w_in_bf = w_in[i].astype(BF16)
        w_out_bf = w_out[i].astype(BF16)
        g_mix = mix_norm_g[i].reshape(1, d)
        g_ffn = ffn_norm_g[i].reshape(1, d)
        q_l, k_l, v_l, xr_l, gr_l = _in_proj_call(h_lat, g_mix, bmods_lat, i, w_in_bf, aw, lw)
        q_c, k_c, v_c, xr_c, gr_c = _in_proj_call(h_ctx, g_mix, bmods_ctx, i, w_in_bf, aw, lw)
        oa_l = _na_call(flat(q_l), flat(k_l), flat(v_l), flat(k_c), flat(v_c), _pad_rpb(na_rpb[i]), batch)
        wg, bg = _lru_gate_weights(lru_wa[i], lru_ba[i], lru_wx[i], lru_bx[i], LANES)
        ol_l, ol_c = _lru_call(xr_l, gr_l, xr_c, gr_c, conv_w[i], conv_b[i], wg, bg, lru_lam[i])
        oa_l = oa_l.reshape(batch, seq, aw)
        j = i // 2
        if i % 2 == 0:
            ffn_w = (ffn_w1[j], ffn_w3[j], ffn_w2[j])
            h_lat = _proj_ffn_call(oa_l, ol_l, h_lat, g_ffn, bmods_lat, i, w_out_bf, *ffn_w)
            if not last:
                oa_c = _ctx_attn_call(flat(q_c), flat(k_c), flat(v_c), batch).reshape(batch, ctx_len, aw)
                h_ctx = _proj_ffn_call(oa_c, ol_c, h_ctx, g_ffn, bmods_ctx, i, w_out_bf, *ffn_w)
        else:
            assert last, "the routed layer is fused with the final norm"
            h_lat = _out_proj_call(oa_l, ol_l, h_lat, bmods_lat, i, w_out_bf)
            out = _moe_layer(flat(h_lat), g_ffn, mods, i, moe_router[j], moe_router_b[j],
                             moe_w1[j], moe_w3[j], moe_w2[j], final_g.reshape(1, d), seq)
    return out.reshape(batch, seq, d)
```

```python
import functools

import jax
import jax.numpy as jnp
from jax import lax
from jax.experimental import pallas as pl
from jax.experimental.pallas import tpu as pltpu

F32 = jnp.float32
BF16 = jnp.bfloat16
I32 = jnp.int32

GRID_W = 64
HEAD_DIM = 64
N_HEADS = 8
NA_KH = 8
NA_KW = 16
LRU_BLOCK = 64
LRU_C = 8.0
CONV_W = 4
N_EXPERTS = 8
EPS = 1e-6
NEG_INF = -1e30
LOG2E = 1.4426950408889634
LANES = 128
SUBLANES = 8
VMEM_BYTES = 64 * 1024 * 1024

FF_CHUNK = 512
ROW_TILE_FFN = 1024
ROW_TILE = 512


def _params(semantics, vmem_mb):
    return pltpu.CompilerParams(dimension_semantics=semantics,
                                vmem_limit_bytes=min(vmem_mb * 1024 * 1024, VMEM_BYTES - 4 * 1024 * 1024))


def _rms_mod(x, g, sh, sc):
    y = x * lax.rsqrt(jnp.mean(x * x, axis=-1, keepdims=True) + EPS)
    return (y * g) * (1.0 + sc) + sh


def _ada_kernel(c_ref, w_ref, b_ref, o_ref):
    c = c_ref[...]
    s = c * jax.nn.sigmoid(c)
    o_ref[...] = jnp.dot(s, w_ref[...], preferred_element_type=F32,
                         precision=lax.Precision.HIGHEST) + b_ref[...]


def _ada_call(c_all, ada_w, ada_b):
    depth, d, n = ada_w.shape
    rows = c_all.shape[0]
    tn = 1024
    return pl.pallas_call(
        _ada_kernel,
        grid=(depth, n // tn),
        in_specs=[pl.BlockSpec((rows, d), lambda l, j: (0, 0)),
                  pl.BlockSpec((None, d, tn), lambda l, j: (l, 0, j)),
                  pl.BlockSpec((None, 1, tn), lambda l, j: (l, 0, j))],
        out_specs=pl.BlockSpec((None, rows, tn), lambda l, j: (l, 0, j)),
        out_shape=jax.ShapeDtypeStruct((depth, rows, n), F32),
        compiler_params=_params(("arbitrary", "arbitrary"), 24),
        name="ada_mod",
    )(c_all, ada_w, ada_b.reshape(depth, 1, n))


TIME_TILE = 128


def _to_time_major(r, o_ref, nb, tt):
    for c in range(o_ref.shape[0]):
        for b in range(nb):
            o_ref[c, pl.ds(b, tt, stride=SUBLANES), :] = r[b * tt:(b + 1) * tt, c * LANES:(c + 1) * LANES]


def _from_time_major(x_ref, nb, tt):
    return jnp.concatenate(
        [jnp.concatenate([x_ref[c, pl.ds(b, tt, stride=SUBLANES), :] for c in range(x_ref.shape[0])], axis=1)
         for b in range(nb)], axis=0)


def _in_proj_kernel(x_ref, g_ref, sh_ref, sc_ref, w_ref, q_ref, k_ref, v_ref, xr_ref, gr_ref, *, aw, lw):
    nb, tt, d = x_ref.shape
    a = _rms_mod(x_ref[...], g_ref[...], sh_ref[...], sc_ref[...]).astype(BF16).reshape(nb * tt, d)
    r = jnp.dot(a, w_ref[...], preferred_element_type=F32)
    q_ref[...] = (r[:, 0:aw] * (LOG2E * HEAD_DIM ** -0.5)).astype(BF16).reshape(nb, tt, aw)
    k_ref[...] = r[:, aw:2 * aw].astype(BF16).reshape(nb, tt, aw)
    v_ref[...] = r[:, 2 * aw:3 * aw].astype(BF16).reshape(nb, tt, aw)
    _to_time_major(r[:, 3 * aw:3 * aw + lw], xr_ref, nb, tt)
    _to_time_major(r[:, 3 * aw + lw:3 * aw + 2 * lw], gr_ref, nb, tt)


def _batch_mod_spec(nb, d, layer, col):
    return pl.BlockSpec((None, nb, 1, d), lambda j: (layer, 0, 0, col))


def _in_proj_call(x, g, bmods, layer, w_bf, aw, lw):
    nb, t, d = x.shape
    assert nb == SUBLANES, "time-major rows put the batch on the sublanes"
    tt = min(TIME_TILE, t)
    ncol = w_bf.shape[1]
    ncb = lw // LANES
    bt = lambda w: pl.BlockSpec((nb, tt, w), lambda j: (0, j, 0))
    tm = pl.BlockSpec((ncb, tt * SUBLANES, LANES), lambda j: (0, j, 0))
    return pl.pallas_call(
        functools.partial(_in_proj_kernel, aw=aw, lw=lw),
        grid=(t // tt,),
        in_specs=[bt(d), pl.BlockSpec((1, d), lambda j: (0, 0)),
                  _batch_mod_spec(nb, d, layer, 0), _batch_mod_spec(nb, d, layer, 1),
                  pl.BlockSpec((d, ncol), lambda j: (0, 0))],
        out_specs=[bt(aw), bt(aw), bt(aw), tm, tm],
        out_shape=[jax.ShapeDtypeStruct((nb, t, aw), BF16)] * 3
        + [jax.ShapeDtypeStruct((ncb, t * SUBLANES, LANES), F32)] * 2,
        compiler_params=_params(("arbitrary",), 56),
        name="in_proj",
    )(x, g, bmods, bmods, w_bf)


def _softmax_pv(parts):
    m = None
    for s, _ in parts:
        mi = jnp.max(s, axis=-1, keepdims=True)
        m = mi if m is None else jnp.maximum(m, mi)
    l = None
    acc = None
    for s, v in parts:
        p = jnp.exp2(s - m)
        li = jnp.sum(p, axis=-1, keepdims=True)
        ai = jnp.dot(p.astype(BF16), v, preferred_element_type=F32)
        l = li if l is None else l + li
        acc = ai if acc is None else acc + ai
    return acc * (1.0 / l)


def _qk(q, k):
    return lax.dot_general(q, k, (((1,), (1,)), ((), ())), preferred_element_type=F32)


def _build_na_bias(rpb_ref, bias_ref):
    lane = lax.broadcasted_iota(I32, (GRID_W, LANES), 1)
    q = lax.broadcasted_iota(I32, (GRID_W, LANES), 0)
    kcol = lane & (GRID_W - 1)
    cs = jnp.clip(q - NA_KW // 2, 0, GRID_W - NA_KW)
    ok = (kcol >= cs) & (kcol < cs + NA_KW)
    low = lane < GRID_W
    for delta in range(NA_KH):
        for h in range(N_HEADS):
            for jp in range(NA_KH // 2):
                halves = []
                for j in (2 * jp, 2 * jp + 1):
                    dr = j - delta + NA_KH - 1
                    w = jnp.broadcast_to(rpb_ref[h, dr:dr + 1, :], (GRID_W, LANES))
                    base = (j % 2) * GRID_W - (NA_KW - 1)
                    halves.append(pltpu.roll(w, base % LANES, 1, stride=1, stride_axis=0))
                t = jnp.where(low, halves[0], halves[1])
                bias_ref[delta, h, :, jp * LANES:(jp + 1) * LANES] = jnp.where(ok, t * LOG2E, NEG_INF)


NA_ROWS_PER_STEP = 8


def _stack_pair(q2):
    first = lax.broadcasted_iota(I32, q2.shape, 1) < HEAD_DIM
    zero = jnp.zeros_like(q2)
    return jnp.concatenate([jnp.where(first, q2, zero), jnp.where(first, zero, q2)], axis=0)


def _unstack_pair(o2):
    nq = o2.shape[0] // 2
    first = lax.broadcasted_iota(I32, (nq, o2.shape[1]), 1) < HEAD_DIM
    return jnp.where(first, o2[:nq], o2[nq:])


def _na_kernel(q_ref, kl_ref, vl_ref, kc_ref, vc_ref, rpb_ref, o_ref, bias_ref, s_ref, p_ref, *, rows, rb):
    @pl.when((pl.program_id(0) == 0) & (pl.program_id(1) == 0))
    def _():
        _build_na_bias(rpb_ref, bias_ref)

    nwin = NA_KH * GRID_W
    npair = N_HEADS // 2
    for rho in range(rb):
        r = pl.program_id(1) * rb + rho
        rs = jnp.clip(r - NA_KH // 2, 0, rows - NA_KH)
        delta = r - rs
        k0 = pl.multiple_of(rs * GRID_W, GRID_W)
        qrows = slice(rho * GRID_W, (rho + 1) * GRID_W)
        for pair in range(npair):
            sl = slice(pair * 2 * HEAD_DIM, (pair + 1) * 2 * HEAD_DIM)
            keys = jnp.concatenate([kl_ref[pl.ds(k0, nwin), sl], kc_ref[:, sl]], axis=0)
            s = _qk(_stack_pair(q_ref[qrows, sl]), keys)
            for sub in range(2):
                h = pair * 2 + sub
                rr = slice(sub * GRID_W, (sub + 1) * GRID_W)
                s_ref[rho, h, :, 0:nwin] = s[rr, 0:nwin] + bias_ref[delta, h]
                s_ref[rho, h, :, nwin:] = s[rr, nwin:]
        s = s_ref[rho]
        p = jnp.exp2(s - jnp.max(s, axis=-1, keepdims=True))
        inv = 1.0 / jnp.sum(p, axis=-1, keepdims=True)
        p_ref[rho] = p.astype(BF16)
        for pair in range(npair):
            sl = slice(pair * 2 * HEAD_DIM, (pair + 1) * 2 * HEAD_DIM)
            vals = jnp.concatenate([vl_ref[pl.ds(k0, nwin), sl], vc_ref[:, sl]], axis=0)
            p2 = p_ref[rho, 2 * pair:2 * pair + 2].reshape(2 * GRID_W, p_ref.shape[-1])
            o2 = jnp.dot(p2, vals, preferred_element_type=F32) * inv[2 * pair:2 * pair + 2].reshape(2 * GRID_W, 1)
            o_ref[qrows, sl] = _unstack_pair(o2).astype(o_ref.dtype)


def _na_call(q, kl, vl, kc, vc, rpb, batch):
    s = q.shape[0] // batch
    l = kc.shape[0] // batch
    w = q.shape[1]
    rows = s // GRID_W
    assert rows >= NA_KH
    rb = NA_ROWS_PER_STEP if rows % NA_ROWS_PER_STEP == 0 else 1
    nkeys = NA_KH * GRID_W + l
    nrb = rows // rb
    return pl.pallas_call(
        functools.partial(_na_kernel, rows=rows, rb=rb),
        grid=(batch, nrb),
        in_specs=[pl.BlockSpec((rb * GRID_W, w), lambda b, r: (b * nrb + r, 0)),
                  pl.BlockSpec((s, w), lambda b, r: (b, 0)),
                  pl.BlockSpec((s, w), lambda b, r: (b, 0)),
                  pl.BlockSpec((l, w), lambda b, r: (b, 0)),
                  pl.BlockSpec((l, w), lambda b, r: (b, 0)),
                  pl.BlockSpec(rpb.shape, lambda b, r: (0, 0, 0))],
        out_specs=pl.BlockSpec((rb * GRID_W, w), lambda b, r: (b * nrb + r, 0)),
        out_shape=jax.ShapeDtypeStruct(q.shape, BF16),
        scratch_shapes=[pltpu.VMEM((NA_KH, N_HEADS, GRID_W, NA_KH * GRID_W), F32),
                        pltpu.VMEM((rb, N_HEADS, GRID_W, nkeys), F32),
                        pltpu.VMEM((rb, N_HEADS, GRID_W, nkeys), BF16)],
        compiler_params=_params(("arbitrary", "arbitrary"), 48),
        name="na_attention",
    )(q, kl, vl, kc, vc, rpb)


def _pad_rpb(rpb):
    h, nr, nc = rpb.shape
    return jnp.zeros((h, 2 * NA_KH, LANES), F32).at[:, :nr, :nc].set(rpb.astype(F32))


def _ctx_attn_kernel(q_ref, k_ref, v_ref, o_ref):
    rows = q_ref.shape[0]
    lane = lax.broadcasted_iota(I32, (rows, 2 * HEAD_DIM), 1)
    first = lane < HEAD_DIM
    for pair in range(N_HEADS // 2):
        sl = slice(pair * 2 * HEAD_DIM, (pair + 1) * 2 * HEAD_DIM)
        q2 = q_ref[:, sl]
        k = k_ref[:, sl]
        v = v_ref[:, sl]
        outs = []
        for sub in range(2):
            qh = jnp.where(first if sub == 0 else ~first, q2, jnp.zeros_like(q2))
            outs.append(_softmax_pv([(_qk(qh, k), v)]))
        o_ref[:, sl] = jnp.where(first, outs[0], outs[1]).astype(o_ref.dtype)


def _ctx_attn_call(q, k, v, batch):
    l = q.shape[0] // batch
    w = q.shape[1]
    spec = pl.BlockSpec((l, w), lambda b: (b, 0))
    return pl.pallas_call(
        _ctx_attn_kernel,
        grid=(batch,),
        in_specs=[spec, spec, spec],
        out_specs=spec,
        out_shape=jax.ShapeDtypeStruct(q.shape, BF16),
        compiler_params=_params(("arbitrary",), 24),
        name="ctx_attention",
    )(q, k, v)


def _softplus(x):
    return jnp.maximum(x, 0.0) + jnp.log1p(jnp.exp(-jnp.abs(x)))


LRU_TIME_CHUNK = 64


def _lru_kernel(xl_hbm, xc_hbm, gl_ref, gc_ref, cw_ref, cb_ref, wg_ref, bg_ref, lam_ref, ol_ref, oc_ref,
                padl_ref, padc_ref, sem, *, tchunk):
    c = pl.program_id(0)
    slot = c % 2
    nb = SUBLANES
    halo_lo, halo_hi = nb, 2 * nb
    zero = jnp.zeros((nb, LANES), F32)

    def fetches(block, slot):
        return [pltpu.make_async_copy(x_hbm.at[block], pad_ref.at[slot, pl.ds(halo_lo, x_hbm.shape[1])], sem.at[slot, k])
                for k, (x_hbm, pad_ref) in enumerate(((xc_hbm, padc_ref), (xl_hbm, padl_ref)))]

    @pl.when(c == 0)
    def _():
        for x_hbm, pad_ref in ((xc_hbm, padc_ref), (xl_hbm, padl_ref)):
            n = x_hbm.shape[1]
            for s in range(2):
                pad_ref[s, 0:halo_lo, :] = zero
                pad_ref[s, halo_lo + n:halo_lo + n + halo_hi, :] = jnp.zeros((halo_hi, LANES), F32)
        for cp in fetches(0, 0):
            cp.start()

    @pl.when(c + 1 < pl.num_programs(0))
    def _():
        for cp in fetches(c + 1, 1 - slot):
            cp.start()

    for cp in fetches(c, slot):
        cp.wait()

    la = [(-0.5 * LRU_C * LOG2E) * _softplus(-lam_ref[d:d + 1, :]) for d in range(2)]
    cb = cb_ref[...]
    cw = [cw_ref[j:j + 1, :] for j in range(CONV_W)]
    rows = tchunk * nb

    def segment(pad_ref, g_ref, o_ref, d, h):
        n = o_ref.shape[0] // rows

        def body(i, h):
            r0 = pl.multiple_of((i if d == 0 else n - 1 - i) * rows, rows)
            u = cb
            for j in range(CONV_W):
                u = u + pad_ref[slot, pl.ds(r0 + j * nb, rows), :] * cw[j]
            ub = u.astype(BF16)
            ta = jnp.tanh(jnp.dot(ub, wg_ref[d, 0], preferred_element_type=F32) + bg_ref[d, 0])
            ti = jnp.tanh(jnp.dot(ub, wg_ref[d, 1], preferred_element_type=F32) + bg_ref[d, 1])
            a = jnp.exp2(la[d] * ta + la[d])
            z = 1.0 - a * a
            b = jnp.where(z > 0.0, z * lax.rsqrt(z), 0.0) * (0.5 * ti + 0.5) * u
            hs = [None] * tchunk
            for t in (range(tchunk) if d == 0 else reversed(range(tchunk))):
                h = a[t * nb:(t + 1) * nb] * h + b[t * nb:(t + 1) * nb]
                hs[t] = h
            hcat = jnp.concatenate(hs, axis=0)
            if d == 0:
                o_ref[pl.ds(r0, rows), :] = hcat
            else:
                y = o_ref[pl.ds(r0, rows), :] + hcat
                o_ref[pl.ds(r0, rows), :] = jax.nn.gelu(g_ref[pl.ds(r0, rows), :]) * y
            return h

        return lax.fori_loop(0, n, body, h)

    for d in range(2):
        h = segment(padc_ref, gc_ref, oc_ref, d, zero)
        segment(padl_ref, gl_ref, ol_ref, d, h)


def _lru_call(xr_l, gr_l, xr_c, gr_c, conv_w, conv_b, wg, bg, lam):
    ncb, rl, _ = xr_l.shape
    rc = xr_c.shape[1]
    tchunk = min(LRU_TIME_CHUNK, rl // SUBLANES, rc // SUBLANES)
    halo = 3 * SUBLANES
    lat = pl.BlockSpec((None, rl, LANES), lambda c: (c, 0, 0))
    ctx = pl.BlockSpec((None, rc, LANES), lambda c: (c, 0, 0))
    per_block = lambda a: jnp.moveaxis(a.reshape(a.shape[:-1] + (ncb, LANES)), -2, 0)
    return pl.pallas_call(
        functools.partial(_lru_kernel, tchunk=tchunk),
        grid=(ncb,),
        in_specs=[pl.BlockSpec(memory_space=pl.ANY), pl.BlockSpec(memory_space=pl.ANY),
                  lat, ctx,
                  pl.BlockSpec((None, CONV_W, LANES), lambda c: (c, 0, 0)),
                  pl.BlockSpec((None, 1, LANES), lambda c: (c, 0, 0)),
                  pl.BlockSpec((2, 2, None, LANES, LANES), lambda c: (0, 0, c, 0, 0)),
                  pl.BlockSpec((None, 2, 2, 1, LANES), lambda c: (c, 0, 0, 0, 0)),
                  pl.BlockSpec((None, 2, LANES), lambda c: (c, 0, 0))],
        out_specs=[lat, ctx],
        out_shape=[jax.ShapeDtypeStruct(xr_l.shape, F32), jax.ShapeDtypeStruct(xr_c.shape, F32)],
        scratch_shapes=[pltpu.VMEM((2, rl + halo, LANES), F32), pltpu.VMEM((2, rc + halo, LANES), F32),
                        pltpu.SemaphoreType.DMA((2, 2))],
        compiler_params=_params(("arbitrary",), 58),
        name="rglru",
    )(xr_l, xr_c, gr_l, gr_c, per_block(conv_w), per_block(conv_b.reshape(1, -1)), wg,
      per_block(bg[:, :, None, :]), per_block(lam))


def _lru_gate_weights(wa, ba, wx, bx, half):
    nblk = wa.shape[1]
    lw = nblk * LRU_BLOCK
    per = half // LRU_BLOCK

    def dense(w):
        w = w.reshape(2, nblk // per, per, LRU_BLOCK, LRU_BLOCK)
        eye = jnp.eye(per, dtype=w.dtype)
        full = w[:, :, :, :, None, :] * eye[None, None, :, None, :, None]
        return full.reshape(2, nblk // per, half, half)

    wg = (0.5 * jnp.stack([dense(wa), dense(wx)], axis=1)).astype(BF16)
    bg = 0.5 * jnp.stack([ba, bx], axis=1).astype(F32)
    return wg, bg


def _swiglu_step(a_bf, w1_ref, w3_ref, w2_ref, acc_ref):
    g = jnp.dot(a_bf, w1_ref[...].astype(BF16), preferred_element_type=F32)
    u = jnp.dot(a_bf, w3_ref[...].astype(BF16), preferred_element_type=F32)
    hmid = (g * jax.nn.sigmoid(g) * u).astype(BF16)
    acc_ref[...] += jnp.dot(hmid, w2_ref[...].astype(BF16), preferred_element_type=F32)


def _swiglu_streamed(a_ref, acc_ref, w_cur, w_next, first, has_next, tile, bufs, sem):
    w1b, w3b, w2b = bufs
    fc = w1b.shape[2]
    nc = w_cur[0].shape[1] // fc

    def copies(w, c, slot):
        return (pltpu.make_async_copy(w[0].at[:, pl.ds(c * fc, fc)], w1b.at[slot], sem.at[0, slot]),
                pltpu.make_async_copy(w[1].at[:, pl.ds(c * fc, fc)], w3b.at[slot], sem.at[1, slot]),
                pltpu.make_async_copy(w[2].at[pl.ds(c * fc, fc), :], w2b.at[slot], sem.at[2, slot]))

    @pl.when(first)
    def _():
        for cp in copies(w_cur, 0, (tile * nc) % 2):
            cp.start()

    acc_ref[...] = jnp.zeros_like(acc_ref)
    for c in range(nc):
        slot = (tile * nc + c) % 2
        if c + 1 < nc:
            for cp in copies(w_cur, c + 1, 1 - slot):
                cp.start()
        else:
            @pl.when(has_next)
            def _():
                for cp in copies(w_next, 0, 1 - slot):
                    cp.start()
        for cp in copies(w_cur, c, slot):
            cp.wait()
        _swiglu_step(a_ref[...], w1b.at[slot], w3b.at[slot], w2b.at[slot], acc_ref)


def _swiglu_buffers(d, dtype):
    return [pltpu.VMEM((2, d, FF_CHUNK), dtype), pltpu.VMEM((2, d, FF_CHUNK), dtype),
            pltpu.VMEM((2, FF_CHUNK, d), dtype), pltpu.SemaphoreType.DMA((3, 2))]


def _mix_proj(oa_ref, ol_ref, h_ref, g1_ref, wo_ref, aw):
    nb, tt, d = h_ref.shape
    o = jnp.dot(oa_ref[...].reshape(nb * tt, aw), wo_ref[0:aw, :], preferred_element_type=F32)
    ol = _from_time_major(ol_ref, nb, tt).astype(BF16)
    o = o + jnp.dot(ol, wo_ref[aw:, :], preferred_element_type=F32)
    return h_ref[...] + g1_ref[...] * o.reshape(nb, tt, d)


def _proj_ffn_kernel(oa_ref, ol_ref, h_ref, g1_ref, wo_ref, g_ref, sh_ref, sc_ref, g2_ref,
                     w1_hbm, w3_hbm, w2_hbm, o_ref, a_ref, acc_ref, w1b, w3b, w2b, sem, *, aw):
    j = pl.program_id(0)
    nb, tt, d = h_ref.shape
    h1 = _mix_proj(oa_ref, ol_ref, h_ref, g1_ref, wo_ref, aw)
    o_ref[...] = h1
    a_ref[...] = _rms_mod(h1, g_ref[...], sh_ref[...], sc_ref[...]).astype(BF16).reshape(nb * tt, d)
    w = (w1_hbm, w3_hbm, w2_hbm)
    _swiglu_streamed(a_ref, acc_ref, w, w, j == 0, j + 1 < pl.num_programs(0), j, (w1b, w3b, w2b), sem)
    o_ref[...] += g2_ref[...] * acc_ref[...].reshape(nb, tt, d)


def _proj_ffn_call(oa, ol, h, g, bmods, layer, wo_bf, w1, w3, w2):
    nb, t, d = h.shape
    aw = oa.shape[2]
    ncb = ol.shape[0]
    dff = w1.shape[1]
    tt = min(ROW_TILE_FFN // nb, t)
    assert dff % FF_CHUNK == 0
    bt = lambda w: pl.BlockSpec((nb, tt, w), lambda j: (0, j, 0))
    bm = lambda col: pl.BlockSpec((None, nb, 1, d), lambda j: (layer, 0, 0, col))
    hbm = pl.BlockSpec(memory_space=pl.ANY)
    return pl.pallas_call(
        functools.partial(_proj_ffn_kernel, aw=aw),
        grid=(t // tt,),
        in_specs=[bt(aw), pl.BlockSpec((ncb, tt * SUBLANES, LANES), lambda j: (0, j, 0)), bt(d), bm(2),
                  pl.BlockSpec(wo_bf.shape, lambda j: (0, 0), pipeline_mode=pl.Buffered(1)),
                  pl.BlockSpec((1, d), lambda j: (0, 0)), bm(3), bm(4), bm(5), hbm, hbm, hbm],
        out_specs=bt(d),
        out_shape=jax.ShapeDtypeStruct((nb, t, d), F32),
        scratch_shapes=[pltpu.VMEM((nb * tt, d), BF16), pltpu.VMEM((nb * tt, d), F32)]
        + _swiglu_buffers(d, w1.dtype),
        compiler_params=_params(("arbitrary",), 58),
        name="proj_ffn",
    )(oa, ol, h, bmods, wo_bf, g, bmods, bmods, bmods, w1, w3, w2)


def _to_row_tiles(x, o_ref):
    t, d = x.shape
    ns = d // LANES
    for s in range(ns):
        o_ref[pl.ds(s, t, stride=ns), :] = x[:, s * LANES:(s + 1) * LANES]


def _from_row_tiles(x_ref, idx, t, ns):
    return jnp.concatenate([x_ref[idx + (pl.ds(s, t, stride=ns), slice(None))] for s in range(ns)], axis=1)


def _gffn_kernel(te_ref, nu_ref, x_ref, w1_hbm, w3_hbm, w2_hbm, o_ref, a_ref, acc_ref, w1b, w3b, w2b, sem):
    i = pl.program_id(0)
    tm, d = a_ref.shape
    nu = nu_ref[0]

    @pl.when(i < nu)
    def _():
        a_ref[...] = _from_row_tiles(x_ref, (), tm, d // LANES).astype(BF16)
        e = te_ref[i]
        e_next = te_ref[jnp.minimum(i + 1, pl.num_programs(0) - 1)]
        _swiglu_streamed(a_ref, acc_ref, (w1_hbm.at[e], w3_hbm.at[e], w2_hbm.at[e]),
                         (w1_hbm.at[e_next], w3_hbm.at[e_next], w2_hbm.at[e_next]),
                         i == 0, i + 1 < nu, i, (w1b, w3b, w2b), sem)
        _to_row_tiles(acc_ref[...], o_ref)

    @pl.when(i >= nu)
    def _():
        o_ref[...] = jnp.zeros_like(o_ref)


def _gffn_call(tile_expert, n_used, xg, w1, w3, w2):
    d = w1.shape[1]
    ns = d // LANES
    rg = xg.shape[0] // ns
    dff = w1.shape[2]
    tm = ROW_TILE_FFN
    assert dff % FF_CHUNK == 0

    def tile(i, nu):
        return jnp.maximum(jnp.minimum(i, nu[0] - 1), 0)

    hbm = pl.BlockSpec(memory_space=pl.ANY)
    grid_spec = pltpu.PrefetchScalarGridSpec(
        num_scalar_prefetch=2,
        grid=(rg // tm,),
        in_specs=[pl.BlockSpec((tm * ns, LANES), lambda i, te, nu: (tile(i, nu), 0)), hbm, hbm, hbm],
        out_specs=pl.BlockSpec((tm * ns, LANES), lambda i, te, nu: (i, 0)),
        scratch_shapes=[pltpu.VMEM((tm, d), BF16), pltpu.VMEM((tm, d), F32)] + _swiglu_buffers(d, w1.dtype))
    return pl.pallas_call(
        _gffn_kernel,
        grid_spec=grid_spec,
        out_shape=jax.ShapeDtypeStruct((rg * ns, LANES), F32),
        compiler_params=_params(("arbitrary",), 56),
        name="ffn_grouped",
    )(tile_expert, n_used, xg, w1, w3, w2)


ROUTE_TIME_TILE = 64


def _proj_route_kernel(oa_ref, ol_ref, h_ref, g1_ref, wo_ref, g_ref, sh_ref, sc_ref, wr_ref, br_ref,
                       h1_ref, a_ref, tab_ref, gcol_ref, cnt_ref, carry_ref, *, aw):
    i = pl.program_id(0)
    nb, tt, d = h_ref.shape
    tm = nb * tt

    @pl.when(i == 0)
    def _():
        carry_ref[...] = jnp.zeros_like(carry_ref)

    h1 = _mix_proj(oa_ref, ol_ref, h_ref, g1_ref, wo_ref, aw)
    h1_ref[...] = h1
    a = _rms_mod(h1, g_ref[...], sh_ref[...], sc_ref[...]).reshape(tm, d)
    _to_row_tiles(a, a_ref)
    lane = lax.broadcasted_iota(I32, (tm, LANES), 1).astype(F32)
    logits = jnp.dot(a, wr_ref[...], preferred_element_type=F32, precision=lax.Precision.HIGHEST) + br_ref[...]
    logits = jnp.where(lane < N_EXPERTS, logits, NEG_INF)
    m1 = jnp.max(logits, axis=-1, keepdims=True)
    i1 = jnp.min(jnp.where(logits == m1, lane, float(LANES)), axis=-1, keepdims=True)
    rest = jnp.where(lane == i1, 2.0 * NEG_INF, logits)
    m2 = jnp.max(rest, axis=-1, keepdims=True)
    i2 = jnp.min(jnp.where(rest == m2, lane, float(LANES)), axis=-1, keepdims=True)
    e = jnp.exp(m2 - m1)
    w1 = 1.0 / (1.0 + e)
    w2 = e * w1
    sel1 = lane == i1
    sel2 = lane == i2
    onehot = jnp.where(sel1 | sel2, 1.0, 0.0)
    row = lax.broadcasted_iota(I32, (tm, tm), 0)
    col = lax.broadcasted_iota(I32, (tm, tm), 1)
    before = jnp.where(col < row, 1.0, 0.0).astype(BF16)
    cum = jnp.dot(before, onehot.astype(BF16), preferred_element_type=F32) + carry_ref[...]
    r1 = jnp.sum(jnp.where(sel1, cum, 0.0), axis=-1, keepdims=True)
    r2 = jnp.sum(jnp.where(sel2, cum, 0.0), axis=-1, keepdims=True)
    carry_ref[...] += jnp.sum(onehot, axis=0, keepdims=True)
    cnt_ref[...] = carry_ref[...]
    cols = (jnp.where(lane == 0, i1, 0.0) + jnp.where(lane == 1, i2, 0.0)
            + jnp.where(lane == 2, r1, 0.0) + jnp.where(lane == 3, r2, 0.0))
    tab_ref[...] = cols.T[0:SUBLANES, :]
    gcol_ref[...] = (jnp.where(lane == 0, w1, 0.0) + jnp.where(lane == 1, w2, 0.0))[:, 0:SUBLANES]


def _proj_route_call(oa, ol, h, g, bmods, layer, wo_bf, wr, br):
    nb, t, d = h.shape
    aw = oa.shape[2]
    ncb = ol.shape[0]
    tt = min(ROUTE_TIME_TILE, t)
    tm = nb * tt
    n = nb * t
    bt = lambda w: pl.BlockSpec((nb, tt, w), lambda i: (0, i, 0))
    return pl.pallas_call(
        functools.partial(_proj_route_kernel, aw=aw),
        grid=(t // tt,),
        in_specs=[bt(aw), pl.BlockSpec((ncb, tt * SUBLANES, LANES), lambda i: (0, i, 0)), bt(d),
                  _batch_mod_spec(nb, d, layer, 2), pl.BlockSpec(wo_bf.shape, lambda i: (0, 0)),
                  pl.BlockSpec((1, d), lambda i: (0, 0)),
                  _batch_mod_spec(nb, d, layer, 3), _batch_mod_spec(nb, d, layer, 4),
                  pl.BlockSpec((d, LANES), lambda i: (0, 0)),
                  pl.BlockSpec((1, LANES), lambda i: (0, 0))],
        out_specs=[bt(d),
                   pl.BlockSpec((tm * (d // LANES), LANES), lambda i: (i, 0)),
                   pl.BlockSpec((SUBLANES, tm), lambda i: (0, i)),
                   pl.BlockSpec((tm, SUBLANES), lambda i: (i, 0)),
                   pl.BlockSpec((1, LANES), lambda i: (0, 0))],
        out_shape=[jax.ShapeDtypeStruct((nb, t, d), F32),
                   jax.ShapeDtypeStruct((n * (d // LANES), LANES), F32),
                   jax.ShapeDtypeStruct((SUBLANES, n), F32),
                   jax.ShapeDtypeStruct((n, SUBLANES), F32),
                   jax.ShapeDtypeStruct((1, LANES), F32)],
        scratch_shapes=[pltpu.VMEM((1, LANES), F32)],
        compiler_params=_params(("arbitrary",), 48),
        name="proj_route",
    )(oa, ol, h, bmods, wo_bf, g, bmods, bmods, wr, br)


DMA_UNROLL = 4


FILL_CHUNK = 64


def _dispatch_kernel(off_ref, fill_ref, tab_ref, a_ref, xg_out, pos_ref, sem, *, ns):
    tm = tab_ref.shape[1]

    def rows_copy(src, dst, nrows):
        return pltpu.make_async_copy(a_ref.at[pl.ds(pl.multiple_of(src * ns, ns), nrows * ns)],
                                     xg_out.at[pl.ds(pl.multiple_of(dst * ns, ns), nrows * ns)], sem)

    def row_copy(src, dst):
        return rows_copy(src, dst, 1)

    @pl.when(pl.program_id(0) == pl.num_programs(0) - 1)
    def _():
        nfill = fill_ref.shape[0] // 2
        for e in range(nfill):
            start = fill_ref[e]
            length = fill_ref[nfill + e]
            nbig = length // FILL_CHUNK
            nsmall = length - nbig * FILL_CHUNK
            small0 = start + nbig * FILL_CHUNK

            def big(j, carry, start=start):
                rows_copy(0, start + j * FILL_CHUNK, FILL_CHUNK).start()
                return carry

            def small(j, carry, small0=small0):
                row_copy(0, small0 + j).start()
                return carry

            lax.fori_loop(0, nbig, big, 0)
            lax.fori_loop(0, nsmall, small, 0)
            lax.fori_loop(0, nbig, lambda j, c: (rows_copy(0, 0, FILL_CHUNK).wait(), c)[1], 0)
            lax.fori_loop(0, nsmall, lambda j, c: (row_copy(0, 0).wait(), c)[1], 0)

    def issue(j, carry):
        for u in range(DMA_UNROLL):
            t = j * DMA_UNROLL + u
            for k in range(2):
                p = off_ref[tab_ref[k, t]] + tab_ref[2 + k, t]
                pos_ref[k, t] = p
                row_copy(t, p).start()
        return carry

    lax.fori_loop(0, tm // DMA_UNROLL, issue, 0)

    def drain(j, carry):
        for _ in range(2 * DMA_UNROLL):
            row_copy(0, 0).wait()
        return carry

    lax.fori_loop(0, tm // DMA_UNROLL, drain, 0)


def _dispatch_call(off, fill, tab_i, a, rows_g, ns):
    n = a.shape[0] // ns
    tm = min(ROW_TILE, n)
    assert tm >= FILL_CHUNK
    grid_spec = pltpu.PrefetchScalarGridSpec(
        num_scalar_prefetch=2,
        grid=(n // tm,),
        in_specs=[pl.BlockSpec((4, tm), lambda i, off, fill: (0, i), memory_space=pltpu.SMEM),
                  pl.BlockSpec((tm * ns, LANES), lambda i, off, fill: (i, 0))],
        out_specs=[pl.BlockSpec(memory_space=pl.ANY),
                   pl.BlockSpec((2, tm), lambda i, off, fill: (0, i), memory_space=pltpu.SMEM)],
        scratch_shapes=[pltpu.SemaphoreType.DMA(())])
    return pl.pallas_call(
        functools.partial(_dispatch_kernel, ns=ns),
        grid_spec=grid_spec,
        out_shape=[jax.ShapeDtypeStruct((rows_g * ns, LANES), a.dtype), jax.ShapeDtypeStruct((2, n), I32)],
        compiler_params=_params(("arbitrary",), 16),
        name="moe_dispatch",
    )(off, fill, tab_i, a)


def _combine_kernel(pos_ref, posn_ref, y_hbm, h_ref, gcol_ref, g2_ref, fg_ref, o_ref, buf_ref, sem):
    i = pl.program_id(0)
    n = pl.num_programs(0)
    tm = gcol_ref.shape[0]
    ns = h_ref.shape[-1] // LANES

    def row_copy(src, slot, k, t):
        return pltpu.make_async_copy(y_hbm.at[pl.ds(pl.multiple_of(src * ns, ns), ns)],
                                     buf_ref.at[slot, k, pl.ds(pl.multiple_of(t * ns, ns), ns)], sem.at[slot])

    def issue(p_ref, slot):
        def body(j, carry):
            for u in range(DMA_UNROLL):
                t = j * DMA_UNROLL + u
                for k in range(2):
                    row_copy(p_ref[k, t], slot, k, t).start()
            return carry
        lax.fori_loop(0, tm // DMA_UNROLL, body, 0)

    @pl.when(i == 0)
    def _():
        issue(pos_ref, 0)

    @pl.when(i + 1 < n)
    def _():
        issue(posn_ref, (i + 1) % 2)

    slot = i % 2

    def drain(j, carry):
        for _ in range(2 * DMA_UNROLL):
            row_copy(0, slot, 0, 0).wait()
        return carry

    lax.fori_loop(0, tm // DMA_UNROLL, drain, 0)
    y = (gcol_ref[:, 0:1] * _from_row_tiles(buf_ref, (slot, 0), tm, ns)
         + gcol_ref[:, 1:2] * _from_row_tiles(buf_ref, (slot, 1), tm, ns))
    hn = h_ref[...] + g2_ref[...] * y.reshape(h_ref.shape)
    o_ref[...] = hn * lax.rsqrt(jnp.mean(hn * hn, axis=-1, keepdims=True) + EPS) * fg_ref[...]


def _combine_call(pos, y, h, gcol, bmods, layer, final_g):
    nb, t, d = h.shape
    tt = min(ROUTE_TIME_TILE, t)
    tm = nb * tt
    nt = t // tt
    bt = pl.BlockSpec((nb, tt, d), lambda i: (0, i, 0))
    return pl.pallas_call(
        _combine_kernel,
        grid=(nt,),
        in_specs=[pl.BlockSpec((2, tm), lambda i: (0, i), memory_space=pltpu.SMEM),
                  pl.BlockSpec((2, tm), lambda i: (0, jnp.minimum(i + 1, nt - 1)), memory_space=pltpu.SMEM),
                  pl.BlockSpec(memory_space=pl.ANY),
                  bt,
                  pl.BlockSpec((tm, SUBLANES), lambda i: (i, 0)),
                  _batch_mod_spec(nb, d, layer, 5),
                  pl.BlockSpec((1, d), lambda i: (0, 0))],
        out_specs=bt,
        out_shape=jax.ShapeDtypeStruct((nb, t, d), F32),
        scratch_shapes=[pltpu.VMEM((2, 2, tm * (d // LANES), LANES), F32), pltpu.SemaphoreType.DMA((2,))],
        compiler_params=_params(("arbitrary",), 40),
        name="moe_combine",
    )(pos, pos, y, h, gcol, bmods, final_g)


def _moe_layer(oa, ol, h, g, bmods, layer, wo_bf, router, router_b, w1, w3, w2, final_g):
    nb, t, d = h.shape
    n = nb * t
    wr = jnp.zeros((d, LANES), F32).at[:, :N_EXPERTS].set(router)
    br = jnp.zeros((1, LANES), F32).at[0, :N_EXPERTS].set(router_b)
    h1, a, tab, gcol, cnt = _proj_route_call(oa, ol, h, g, bmods, layer, wo_bf, wr, br)
    tm = ROW_TILE_FFN
    n_tiles = (2 * n) // tm + N_EXPERTS
    counts = cnt[0, :N_EXPERTS].astype(I32)
    tiles = (counts + tm - 1) // tm
    ends = jnp.cumsum(tiles)
    off = ((ends - tiles) * tm).astype(I32)
    tile_expert = jnp.minimum(jnp.sum(jnp.arange(n_tiles, dtype=I32)[:, None] >= ends[None, :], axis=1),
                              N_EXPERTS - 1).astype(I32)
    n_used = ends[-1:].astype(I32)
    assert d // LANES == SUBLANES, "one token must fill one (8, 128) tile of the row-tile layout"
    used = ends[-1:] * tm
    fill = jnp.concatenate([off + counts, used, tiles * tm - counts, n_tiles * tm - used]).astype(I32)
    xg, pos = _dispatch_call(off, fill, tab[0:4].astype(I32), a, n_tiles * tm, d // LANES)
    y = _gffn_call(tile_expert, n_used, xg, w1, w3, w2)
    return _combine_call(pos, y, h1, gcol, bmods, layer, final_g)


def kernel(x, c, ctx, c_ctx, ada_w, ada_b, mix_norm_g, ffn_norm_g, w_in, w_out, na_rpb, conv_w, conv_b, lru_wa, lru_ba, lru_wx, lru_bx, lru_lam, ffn_w1, ffn_w3, ffn_w2, moe_router, moe_router_b, moe_w1, moe_w3, moe_w2, final_g):
    batch, seq, d = x.shape
    ctx_len = ctx.shape[1]
    depth = ada_w.shape[0]
    lw = conv_w.shape[-1]
    aw = (w_in.shape[-1] - 2 * lw) // 3
    rows = seq // GRID_W
    ctx_row = batch

    mod_rows = -(-(batch + 1) // SUBLANES) * SUBLANES
    c_all = jnp.zeros((mod_rows, d), F32).at[:batch].set(c).at[ctx_row].set(c_ctx)
    mods = _ada_call(c_all, ada_w, ada_b).reshape(depth, mod_rows, 1, 6 * d)

    bmods_lat = mods[:, :batch]
    bmods_ctx = jnp.broadcast_to(mods[:, ctx_row:ctx_row + 1], bmods_lat.shape)
    flat = lambda a: a.reshape(a.shape[0] * a.shape[1], a.shape[2])
    h_lat, h_ctx = x, ctx
    out = None
    for i in range(depth):
        last = i == depth - 1
        w_in_bf = w_in[i].astype(BF16)
        w_out_bf = w_out[i].astype(BF16)
        g_mix = mix_norm_g[i].reshape(1, d)
        g_ffn = ffn_norm_g[i].reshape(1, d)
        q_l, k_l, v_l, xr_l, gr_l = _in_proj_call(h_lat, g_mix, bmods_lat, i, w_in_bf, aw, lw)
        q_c, k_c, v_c, xr_c, gr_c = _in_proj_call(h_ctx, g_mix, bmods_ctx, i, w_in_bf, aw, lw)
        oa_l = _na_call(flat(q_l), flat(k_l), flat(v_l), flat(k_c), flat(v_c), _pad_rpb(na_rpb[i]), batch)
        wg, bg = _lru_gate_weights(lru_wa[i], lru_ba[i], lru_wx[i], lru_bx[i], LANES)
        ol_l, ol_c = _lru_call(xr_l, gr_l, xr_c, gr_c, conv_w[i], conv_b[i], wg, bg, lru_lam[i])
        oa_l = oa_l.reshape(batch, seq, aw)
        j = i // 2
        if i % 2 == 0:
            ffn_w = (ffn_w1[j], ffn_w3[j], ffn_w2[j])
            h_lat = _proj_ffn_call(oa_l, ol_l, h_lat, g_ffn, bmods_lat, i, w_out_bf, *ffn_w)
            if not last:
                oa_c = _ctx_attn_call(flat(q_c), flat(k_c), flat(v_c), batch).reshape(batch, ctx_len, aw)
                h_ctx = _proj_ffn_call(oa_c, ol_c, h_ctx, g_ffn, bmods_ctx, i, w_out_bf, *ffn_w)
        else:
            assert last, "the routed layer is fused with the final norm"
            out = _moe_layer(oa_l, ol_l, h_lat, g_ffn, bmods_lat, i, w_out_bf, moe_router[j], moe_router_b[j],
                             moe_w1[j], moe_w3[j], moe_w2[j], final_g.reshape(1, d))
    return out
```

```python
import functools

import jax
import jax.numpy as jnp
from jax import lax
from jax.experimental import pallas as pl
from jax.experimental.pallas import tpu as pltpu

F32 = jnp.float32
BF16 = jnp.bfloat16
I32 = jnp.int32

GRID_W = 64
HEAD_DIM = 64
N_HEADS = 8
NA_KH = 8
NA_KW = 16
LRU_BLOCK = 64
LRU_C = 8.0
CONV_W = 4
N_EXPERTS = 8
EPS = 1e-6
NEG_INF = -1e30
LOG2E = 1.4426950408889634
LANES = 128
SUBLANES = 8
VMEM_BYTES = 64 * 1024 * 1024

FF_CHUNK = 512
ROW_TILE_FFN = 1024
ROW_TILE = 512
GATHER_TILE = 256


def _params(semantics, vmem_mb):
    return pltpu.CompilerParams(dimension_semantics=semantics,
                                vmem_limit_bytes=min(vmem_mb * 1024 * 1024, VMEM_BYTES - 4 * 1024 * 1024))


def _rms_mod(x, g, sh, sc):
    y = x * lax.rsqrt(jnp.mean(x * x, axis=-1, keepdims=True) + EPS)
    return (y * g) * (1.0 + sc) + sh


def _ada_kernel(c_ref, w_ref, b_ref, o_ref):
    c = c_ref[...]
    s = c * jax.nn.sigmoid(c)
    o_ref[...] = jnp.dot(s, w_ref[...], preferred_element_type=F32,
                         precision=lax.Precision.HIGHEST) + b_ref[...]


def _ada_call(c_all, ada_w, ada_b):
    depth, d, n = ada_w.shape
    rows = c_all.shape[0]
    tn = 1024
    return pl.pallas_call(
        _ada_kernel,
        grid=(depth, n // tn),
        in_specs=[pl.BlockSpec((rows, d), lambda l, j: (0, 0)),
                  pl.BlockSpec((None, d, tn), lambda l, j: (l, 0, j)),
                  pl.BlockSpec((None, 1, tn), lambda l, j: (l, 0, j))],
        out_specs=pl.BlockSpec((None, rows, tn), lambda l, j: (l, 0, j)),
        out_shape=jax.ShapeDtypeStruct((depth, rows, n), F32),
        compiler_params=_params(("arbitrary", "arbitrary"), 24),
        name="ada_mod",
    )(c_all, ada_w, ada_b.reshape(depth, 1, n))


TIME_TILE = 128


def _to_time_major(r, o_ref, nb, tt):
    for c in range(o_ref.shape[0]):
        for b in range(nb):
            o_ref[c, pl.ds(b, tt, stride=SUBLANES), :] = r[b * tt:(b + 1) * tt, c * LANES:(c + 1) * LANES]


def _from_time_major(x_ref, nb, tt):
    return jnp.concatenate(
        [jnp.concatenate([x_ref[c, pl.ds(b, tt, stride=SUBLANES), :] for c in range(x_ref.shape[0])], axis=1)
         for b in range(nb)], axis=0)


def _in_proj_kernel(x_ref, g_ref, sh_ref, sc_ref, w_ref, q_ref, k_ref, v_ref, xr_ref, gr_ref, *, aw, lw):
    nb, tt, d = x_ref.shape
    a = _rms_mod(x_ref[...], g_ref[...], sh_ref[...], sc_ref[...]).astype(BF16).reshape(nb * tt, d)
    r = jnp.dot(a, w_ref[...], preferred_element_type=F32)
    q_ref[...] = (r[:, 0:aw] * (LOG2E * HEAD_DIM ** -0.5)).astype(BF16).reshape(nb, tt, aw)
    k_ref[...] = r[:, aw:2 * aw].astype(BF16).reshape(nb, tt, aw)
    v_ref[...] = r[:, 2 * aw:3 * aw].astype(BF16).reshape(nb, tt, aw)
    _to_time_major(r[:, 3 * aw:3 * aw + lw], xr_ref, nb, tt)
    _to_time_major(r[:, 3 * aw + lw:3 * aw + 2 * lw], gr_ref, nb, tt)


def _mod_spec(d, col, layer, rows_per_mod, tm, fixed_row):
    if fixed_row is None:
        assert rows_per_mod % tm == 0, "a row tile must not straddle two batch elements"
        return pl.BlockSpec((None, None, 1, d), lambda i, *_: (layer, (i * tm) // rows_per_mod, 0, col))
    return pl.BlockSpec((None, None, 1, d), lambda i, *_: (layer, fixed_row, 0, col))


def _batch_mod_spec(nb, d, layer, col):
    return pl.BlockSpec((None, nb, 1, d), lambda j: (layer, 0, 0, col))


def _in_proj_call(x, g, bmods, layer, w_bf, aw, lw):
    nb, t, d = x.shape
    assert nb == SUBLANES, "time-major rows put the batch on the sublanes"
    tt = min(TIME_TILE, t)
    ncol = w_bf.shape[1]
    ncb = lw // LANES
    bt = lambda w: pl.BlockSpec((nb, tt, w), lambda j: (0, j, 0))
    tm = pl.BlockSpec((ncb, tt * SUBLANES, LANES), lambda j: (0, j, 0))
    return pl.pallas_call(
        functools.partial(_in_proj_kernel, aw=aw, lw=lw),
        grid=(t // tt,),
        in_specs=[bt(d), pl.BlockSpec((1, d), lambda j: (0, 0)),
                  _batch_mod_spec(nb, d, layer, 0), _batch_mod_spec(nb, d, layer, 1),
                  pl.BlockSpec((d, ncol), lambda j: (0, 0))],
        out_specs=[bt(aw), bt(aw), bt(aw), tm, tm],
        out_shape=[jax.ShapeDtypeStruct((nb, t, aw), BF16)] * 3
        + [jax.ShapeDtypeStruct((ncb, t * SUBLANES, LANES), F32)] * 2,
        compiler_params=_params(("arbitrary",), 56),
        name="in_proj",
    )(x, g, bmods, bmods, w_bf)


def _softmax_pv(parts):
    m = None
    for s, _ in parts:
        mi = jnp.max(s, axis=-1, keepdims=True)
        m = mi if m is None else jnp.maximum(m, mi)
    l = None
    acc = None
    for s, v in parts:
        p = jnp.exp2(s - m)
        li = jnp.sum(p, axis=-1, keepdims=True)
        ai = jnp.dot(p.astype(BF16), v, preferred_element_type=F32)
        l = li if l is None else l + li
        acc = ai if acc is None else acc + ai
    return acc * (1.0 / l)


def _qk(q, k):
    return lax.dot_general(q, k, (((1,), (1,)), ((), ())), preferred_element_type=F32)


def _build_na_bias(rpb_ref, bias_ref):
    lane = lax.broadcasted_iota(I32, (GRID_W, LANES), 1)
    q = lax.broadcasted_iota(I32, (GRID_W, LANES), 0)
    kcol = lane & (GRID_W - 1)
    cs = jnp.clip(q - NA_KW // 2, 0, GRID_W - NA_KW)
    ok = (kcol >= cs) & (kcol < cs + NA_KW)
    low = lane < GRID_W
    for delta in range(NA_KH):
        for h in range(N_HEADS):
            for jp in range(NA_KH // 2):
                halves = []
                for j in (2 * jp, 2 * jp + 1):
                    dr = j - delta + NA_KH - 1
                    w = jnp.broadcast_to(rpb_ref[h, dr:dr + 1, :], (GRID_W, LANES))
                    base = (j % 2) * GRID_W - (NA_KW - 1)
                    halves.append(pltpu.roll(w, base % LANES, 1, stride=1, stride_axis=0))
                t = jnp.where(low, halves[0], halves[1])
                bias_ref[delta, h, :, jp * LANES:(jp + 1) * LANES] = jnp.where(ok, t * LOG2E, NEG_INF)


NA_ROWS_PER_STEP = 8


def _stack_pair(q2):
    first = lax.broadcasted_iota(I32, q2.shape, 1) < HEAD_DIM
    zero = jnp.zeros_like(q2)
    return jnp.concatenate([jnp.where(first, q2, zero), jnp.where(first, zero, q2)], axis=0)


def _unstack_pair(o2):
    nq = o2.shape[0] // 2
    first = lax.broadcasted_iota(I32, (nq, o2.shape[1]), 1) < HEAD_DIM
    return jnp.where(first, o2[:nq], o2[nq:])


def _na_kernel(q_ref, kl_ref, vl_ref, kc_ref, vc_ref, rpb_ref, o_ref, bias_ref, s_ref, p_ref, *, rows, rb):
    @pl.when((pl.program_id(0) == 0) & (pl.program_id(1) == 0))
    def _():
        _build_na_bias(rpb_ref, bias_ref)

    nwin = NA_KH * GRID_W
    npair = N_HEADS // 2
    for rho in range(rb):
        r = pl.program_id(1) * rb + rho
        rs = jnp.clip(r - NA_KH // 2, 0, rows - NA_KH)
        delta = r - rs
        k0 = pl.multiple_of(rs * GRID_W, GRID_W)
        qrows = slice(rho * GRID_W, (rho + 1) * GRID_W)
        for pair in range(npair):
            sl = slice(pair * 2 * HEAD_DIM, (pair + 1) * 2 * HEAD_DIM)
            keys = jnp.concatenate([kl_ref[pl.ds(k0, nwin), sl], kc_ref[:, sl]], axis=0)
            s = _qk(_stack_pair(q_ref[qrows, sl]), keys)
            for sub in range(2):
                h = pair * 2 + sub
                rr = slice(sub * GRID_W, (sub + 1) * GRID_W)
                s_ref[rho, h, :, 0:nwin] = s[rr, 0:nwin] + bias_ref[delta, h]
                s_ref[rho, h, :, nwin:] = s[rr, nwin:]
        s = s_ref[rho]
        p = jnp.exp2(s - jnp.max(s, axis=-1, keepdims=True))
        inv = 1.0 / jnp.sum(p, axis=-1, keepdims=True)
        p_ref[rho] = p.astype(BF16)
        for pair in range(npair):
            sl = slice(pair * 2 * HEAD_DIM, (pair + 1) * 2 * HEAD_DIM)
            vals = jnp.concatenate([vl_ref[pl.ds(k0, nwin), sl], vc_ref[:, sl]], axis=0)
            p2 = p_ref[rho, 2 * pair:2 * pair + 2].reshape(2 * GRID_W, p_ref.shape[-1])
            o2 = jnp.dot(p2, vals, preferred_element_type=F32) * inv[2 * pair:2 * pair + 2].reshape(2 * GRID_W, 1)
            o_ref[qrows, sl] = _unstack_pair(o2).astype(o_ref.dtype)


def _na_call(q, kl, vl, kc, vc, rpb, batch):
    s = q.shape[0] // batch
    l = kc.shape[0] // batch
    w = q.shape[1]
    rows = s // GRID_W
    assert rows >= NA_KH
    rb = NA_ROWS_PER_STEP if rows % NA_ROWS_PER_STEP == 0 else 1
    nkeys = NA_KH * GRID_W + l
    nrb = rows // rb
    return pl.pallas_call(
        functools.partial(_na_kernel, rows=rows, rb=rb),
        grid=(batch, nrb),
        in_specs=[pl.BlockSpec((rb * GRID_W, w), lambda b, r: (b * nrb + r, 0)),
                  pl.BlockSpec((s, w), lambda b, r: (b, 0)),
                  pl.BlockSpec((s, w), lambda b, r: (b, 0)),
                  pl.BlockSpec((l, w), lambda b, r: (b, 0)),
                  pl.BlockSpec((l, w), lambda b, r: (b, 0)),
                  pl.BlockSpec(rpb.shape, lambda b, r: (0, 0, 0))],
        out_specs=pl.BlockSpec((rb * GRID_W, w), lambda b, r: (b * nrb + r, 0)),
        out_shape=jax.ShapeDtypeStruct(q.shape, BF16),
        scratch_shapes=[pltpu.VMEM((NA_KH, N_HEADS, GRID_W, NA_KH * GRID_W), F32),
                        pltpu.VMEM((rb, N_HEADS, GRID_W, nkeys), F32),
                        pltpu.VMEM((rb, N_HEADS, GRID_W, nkeys), BF16)],
        compiler_params=_params(("arbitrary", "arbitrary"), 48),
        name="na_attention",
    )(q, kl, vl, kc, vc, rpb)


def _pad_rpb(rpb):
    h, nr, nc = rpb.shape
    return jnp.zeros((h, 2 * NA_KH, LANES), F32).at[:, :nr, :nc].set(rpb.astype(F32))


def _ctx_attn_kernel(q_ref, k_ref, v_ref, o_ref):
    rows = q_ref.shape[0]
    lane = lax.broadcasted_iota(I32, (rows, 2 * HEAD_DIM), 1)
    first = lane < HEAD_DIM
    for pair in range(N_HEADS // 2):
        sl = slice(pair * 2 * HEAD_DIM, (pair + 1) * 2 * HEAD_DIM)
        q2 = q_ref[:, sl]
        k = k_ref[:, sl]
        v = v_ref[:, sl]
        outs = []
        for sub in range(2):
            qh = jnp.where(first if sub == 0 else ~first, q2, jnp.zeros_like(q2))
            outs.append(_softmax_pv([(_qk(qh, k), v)]))
        o_ref[:, sl] = jnp.where(first, outs[0], outs[1]).astype(o_ref.dtype)


def _ctx_attn_call(q, k, v, batch):
    l = q.shape[0] // batch
    w = q.shape[1]
    spec = pl.BlockSpec((l, w), lambda b: (b, 0))
    return pl.pallas_call(
        _ctx_attn_kernel,
        grid=(batch,),
        in_specs=[spec, spec, spec],
        out_specs=spec,
        out_shape=jax.ShapeDtypeStruct(q.shape, BF16),
        compiler_params=_params(("arbitrary",), 24),
        name="ctx_attention",
    )(q, k, v)


def _softplus(x):
    return jnp.maximum(x, 0.0) + jnp.log1p(jnp.exp(-jnp.abs(x)))


LRU_TIME_CHUNK = 64


def _lru_kernel(xl_hbm, xc_hbm, gl_ref, gc_ref, cw_ref, cb_ref, wg_ref, bg_ref, lam_ref, ol_ref, oc_ref,
                padl_ref, padc_ref, sem, *, tchunk):
    c = pl.program_id(0)
    slot = c % 2
    nb = SUBLANES
    halo_lo, halo_hi = nb, 2 * nb
    zero = jnp.zeros((nb, LANES), F32)

    def fetches(block, slot):
        return [pltpu.make_async_copy(x_hbm.at[block], pad_ref.at[slot, pl.ds(halo_lo, x_hbm.shape[1])], sem.at[slot, k])
                for k, (x_hbm, pad_ref) in enumerate(((xc_hbm, padc_ref), (xl_hbm, padl_ref)))]

    @pl.when(c == 0)
    def _():
        for x_hbm, pad_ref in ((xc_hbm, padc_ref), (xl_hbm, padl_ref)):
            n = x_hbm.shape[1]
            for s in range(2):
                pad_ref[s, 0:halo_lo, :] = zero
                pad_ref[s, halo_lo + n:halo_lo + n + halo_hi, :] = jnp.zeros((halo_hi, LANES), F32)
        for cp in fetches(0, 0):
            cp.start()

    @pl.when(c + 1 < pl.num_programs(0))
    def _():
        for cp in fetches(c + 1, 1 - slot):
            cp.start()

    for cp in fetches(c, slot):
        cp.wait()

    la = [(-0.5 * LRU_C * LOG2E) * _softplus(-lam_ref[d:d + 1, :]) for d in range(2)]
    cb = cb_ref[...]
    cw = [cw_ref[j:j + 1, :] for j in range(CONV_W)]
    rows = tchunk * nb

    def segment(pad_ref, g_ref, o_ref, d, h):
        n = o_ref.shape[0] // rows

        def body(i, h):
            r0 = pl.multiple_of((i if d == 0 else n - 1 - i) * rows, rows)
            u = cb
            for j in range(CONV_W):
                u = u + pad_ref[slot, pl.ds(r0 + j * nb, rows), :] * cw[j]
            ub = u.astype(BF16)
            ta = jnp.tanh(jnp.dot(ub, wg_ref[d, 0], preferred_element_type=F32) + bg_ref[d, 0])
            ti = jnp.tanh(jnp.dot(ub, wg_ref[d, 1], preferred_element_type=F32) + bg_ref[d, 1])
            a = jnp.exp2(la[d] * ta + la[d])
            z = 1.0 - a * a
            b = jnp.where(z > 0.0, z * lax.rsqrt(z), 0.0) * (0.5 * ti + 0.5) * u
            hs = [None] * tchunk
            for t in (range(tchunk) if d == 0 else reversed(range(tchunk))):
                h = a[t * nb:(t + 1) * nb] * h + b[t * nb:(t + 1) * nb]
                hs[t] = h
            hcat = jnp.concatenate(hs, axis=0)
            if d == 0:
                o_ref[pl.ds(r0, rows), :] = hcat
            else:
                y = o_ref[pl.ds(r0, rows), :] + hcat
                o_ref[pl.ds(r0, rows), :] = jax.nn.gelu(g_ref[pl.ds(r0, rows), :]) * y
            return h

        return lax.fori_loop(0, n, body, h)

    for d in range(2):
        h = segment(padc_ref, gc_ref, oc_ref, d, zero)
        segment(padl_ref, gl_ref, ol_ref, d, h)


def _lru_call(xr_l, gr_l, xr_c, gr_c, conv_w, conv_b, wg, bg, lam):
    ncb, rl, _ = xr_l.shape
    rc = xr_c.shape[1]
    tchunk = min(LRU_TIME_CHUNK, rl // SUBLANES, rc // SUBLANES)
    halo = 3 * SUBLANES
    lat = pl.BlockSpec((None, rl, LANES), lambda c: (c, 0, 0))
    ctx = pl.BlockSpec((None, rc, LANES), lambda c: (c, 0, 0))
    per_block = lambda a: jnp.moveaxis(a.reshape(a.shape[:-1] + (ncb, LANES)), -2, 0)
    return pl.pallas_call(
        functools.partial(_lru_kernel, tchunk=tchunk),
        grid=(ncb,),
        in_specs=[pl.BlockSpec(memory_space=pl.ANY), pl.BlockSpec(memory_space=pl.ANY),
                  lat, ctx,
                  pl.BlockSpec((None, CONV_W, LANES), lambda c: (c, 0, 0)),
                  pl.BlockSpec((None, 1, LANES), lambda c: (c, 0, 0)),
                  pl.BlockSpec((2, 2, None, LANES, LANES), lambda c: (0, 0, c, 0, 0)),
                  pl.BlockSpec((None, 2, 2, 1, LANES), lambda c: (c, 0, 0, 0, 0)),
                  pl.BlockSpec((None, 2, LANES), lambda c: (c, 0, 0))],
        out_specs=[lat, ctx],
        out_shape=[jax.ShapeDtypeStruct(xr_l.shape, F32), jax.ShapeDtypeStruct(xr_c.shape, F32)],
        scratch_shapes=[pltpu.VMEM((2, rl + halo, LANES), F32), pltpu.VMEM((2, rc + halo, LANES), F32),
                        pltpu.SemaphoreType.DMA((2, 2))],
        compiler_params=_params(("arbitrary",), 58),
        name="rglru",
    )(xr_l, xr_c, gr_l, gr_c, per_block(conv_w), per_block(conv_b.reshape(1, -1)), wg,
      per_block(bg[:, :, None, :]), per_block(lam))


def _lru_gate_weights(wa, ba, wx, bx, half):
    nblk = wa.shape[1]
    lw = nblk * LRU_BLOCK
    per = half // LRU_BLOCK

    def dense(w):
        w = w.reshape(2, nblk // per, per, LRU_BLOCK, LRU_BLOCK)
        eye = jnp.eye(per, dtype=w.dtype)
        full = w[:, :, :, :, None, :] * eye[None, None, :, None, :, None]
        return full.reshape(2, nblk // per, half, half)

    wg = (0.5 * jnp.stack([dense(wa), dense(wx)], axis=1)).astype(BF16)
    bg = 0.5 * jnp.stack([ba, bx], axis=1).astype(F32)
    return wg, bg


def _out_proj_kernel(oa_ref, ol_ref, h_ref, g1_ref, w_ref, o_ref, *, aw):
    o_ref[...] = _mix_proj(oa_ref, ol_ref, h_ref, g1_ref, w_ref, aw)


def _out_proj_call(oa, ol, h, bmods, layer, w_bf):
    nb, t, d = h.shape
    aw = oa.shape[2]
    ncb = ol.shape[0]
    tt = min(TIME_TILE, t)
    bt = lambda w: pl.BlockSpec((nb, tt, w), lambda j: (0, j, 0))
    return pl.pallas_call(
        functools.partial(_out_proj_kernel, aw=aw),
        grid=(t // tt,),
        in_specs=[bt(aw), pl.BlockSpec((ncb, tt * SUBLANES, LANES), lambda j: (0, j, 0)), bt(d),
                  _batch_mod_spec(nb, d, layer, 2), pl.BlockSpec(w_bf.shape, lambda j: (0, 0))],
        out_specs=bt(d),
        out_shape=jax.ShapeDtypeStruct((nb, t, d), F32),
        compiler_params=_params(("arbitrary",), 40),
        name="out_proj",
    )(oa, ol, h, bmods, w_bf)


def _swiglu_step(a_bf, w1_ref, w3_ref, w2_ref, acc_ref):
    g = jnp.dot(a_bf, w1_ref[...].astype(BF16), preferred_element_type=F32)
    u = jnp.dot(a_bf, w3_ref[...].astype(BF16), preferred_element_type=F32)
    hmid = (g * jax.nn.sigmoid(g) * u).astype(BF16)
    acc_ref[...] += jnp.dot(hmid, w2_ref[...].astype(BF16), preferred_element_type=F32)


def _swiglu_streamed(a_ref, acc_ref, w_cur, w_next, first, has_next, tile, bufs, sem):
    w1b, w3b, w2b = bufs
    fc = w1b.shape[2]
    nc = w_cur[0].shape[1] // fc

    def copies(w, c, slot):
        return (pltpu.make_async_copy(w[0].at[:, pl.ds(c * fc, fc)], w1b.at[slot], sem.at[0, slot]),
                pltpu.make_async_copy(w[1].at[:, pl.ds(c * fc, fc)], w3b.at[slot], sem.at[1, slot]),
                pltpu.make_async_copy(w[2].at[pl.ds(c * fc, fc), :], w2b.at[slot], sem.at[2, slot]))

    @pl.when(first)
    def _():
        for cp in copies(w_cur, 0, (tile * nc) % 2):
            cp.start()

    acc_ref[...] = jnp.zeros_like(acc_ref)
    for c in range(nc):
        slot = (tile * nc + c) % 2
        if c + 1 < nc:
            for cp in copies(w_cur, c + 1, 1 - slot):
                cp.start()
        else:
            @pl.when(has_next)
            def _():
                for cp in copies(w_next, 0, 1 - slot):
                    cp.start()
        for cp in copies(w_cur, c, slot):
            cp.wait()
        _swiglu_step(a_ref[...], w1b.at[slot], w3b.at[slot], w2b.at[slot], acc_ref)


def _swiglu_buffers(d, dtype):
    return [pltpu.VMEM((2, d, FF_CHUNK), dtype), pltpu.VMEM((2, d, FF_CHUNK), dtype),
            pltpu.VMEM((2, FF_CHUNK, d), dtype), pltpu.SemaphoreType.DMA((3, 2))]


def _mix_proj(oa_ref, ol_ref, h_ref, g1_ref, wo_ref, aw):
    nb, tt, d = h_ref.shape
    o = jnp.dot(oa_ref[...].reshape(nb * tt, aw), wo_ref[0:aw, :], preferred_element_type=F32)
    ol = _from_time_major(ol_ref, nb, tt).astype(BF16)
    o = o + jnp.dot(ol, wo_ref[aw:, :], preferred_element_type=F32)
    return h_ref[...] + g1_ref[...] * o.reshape(nb, tt, d)


def _proj_ffn_kernel(oa_ref, ol_ref, h_ref, g1_ref, wo_ref, g_ref, sh_ref, sc_ref, g2_ref,
                     w1_hbm, w3_hbm, w2_hbm, o_ref, a_ref, acc_ref, w1b, w3b, w2b, sem, *, aw):
    j = pl.program_id(0)
    nb, tt, d = h_ref.shape
    h1 = _mix_proj(oa_ref, ol_ref, h_ref, g1_ref, wo_ref, aw)
    o_ref[...] = h1
    a_ref[...] = _rms_mod(h1, g_ref[...], sh_ref[...], sc_ref[...]).astype(BF16).reshape(nb * tt, d)
    w = (w1_hbm, w3_hbm, w2_hbm)
    _swiglu_streamed(a_ref, acc_ref, w, w, j == 0, j + 1 < pl.num_programs(0), j, (w1b, w3b, w2b), sem)
    o_ref[...] += g2_ref[...] * acc_ref[...].reshape(nb, tt, d)


def _proj_ffn_call(oa, ol, h, g, bmods, layer, wo_bf, w1, w3, w2):
    nb, t, d = h.shape
    aw = oa.shape[2]
    ncb = ol.shape[0]
    dff = w1.shape[1]
    tt = min(ROW_TILE_FFN // nb, t)
    assert dff % FF_CHUNK == 0
    bt = lambda w: pl.BlockSpec((nb, tt, w), lambda j: (0, j, 0))
    bm = lambda col: pl.BlockSpec((None, nb, 1, d), lambda j: (layer, 0, 0, col))
    hbm = pl.BlockSpec(memory_space=pl.ANY)
    return pl.pallas_call(
        functools.partial(_proj_ffn_kernel, aw=aw),
        grid=(t // tt,),
        in_specs=[bt(aw), pl.BlockSpec((ncb, tt * SUBLANES, LANES), lambda j: (0, j, 0)), bt(d), bm(2),
                  pl.BlockSpec(wo_bf.shape, lambda j: (0, 0), pipeline_mode=pl.Buffered(1)),
                  pl.BlockSpec((1, d), lambda j: (0, 0)), bm(3), bm(4), bm(5), hbm, hbm, hbm],
        out_specs=bt(d),
        out_shape=jax.ShapeDtypeStruct((nb, t, d), F32),
        scratch_shapes=[pltpu.VMEM((nb * tt, d), BF16), pltpu.VMEM((nb * tt, d), F32)]
        + _swiglu_buffers(d, w1.dtype),
        compiler_params=_params(("arbitrary",), 58),
        name="proj_ffn",
    )(oa, ol, h, bmods, wo_bf, g, bmods, bmods, bmods, w1, w3, w2)


def _to_row_tiles(x, o_ref):
    t, d = x.shape
    ns = d // LANES
    for s in range(ns):
        o_ref[pl.ds(s, t, stride=ns), :] = x[:, s * LANES:(s + 1) * LANES]


def _from_row_tiles(x_ref, idx, t, ns):
    return jnp.concatenate([x_ref[idx + (pl.ds(s, t, stride=ns), slice(None))] for s in range(ns)], axis=1)


def _gffn_kernel(te_ref, nu_ref, x_ref, w1_hbm, w3_hbm, w2_hbm, o_ref, a_ref, acc_ref, w1b, w3b, w2b, sem):
    i = pl.program_id(0)
    tm, d = a_ref.shape
    nu = nu_ref[0]

    @pl.when(i < nu)
    def _():
        a_ref[...] = _from_row_tiles(x_ref, (), tm, d // LANES).astype(BF16)
        e = te_ref[i]
        e_next = te_ref[jnp.minimum(i + 1, pl.num_programs(0) - 1)]
        _swiglu_streamed(a_ref, acc_ref, (w1_hbm.at[e], w3_hbm.at[e], w2_hbm.at[e]),
                         (w1_hbm.at[e_next], w3_hbm.at[e_next], w2_hbm.at[e_next]),
                         i == 0, i + 1 < nu, i, (w1b, w3b, w2b), sem)
        _to_row_tiles(acc_ref[...], o_ref)

    @pl.when(i >= nu)
    def _():
        o_ref[...] = jnp.zeros_like(o_ref)


def _gffn_call(tile_expert, n_used, xg, w1, w3, w2):
    d = w1.shape[1]
    ns = d // LANES
    rg = xg.shape[0] // ns
    dff = w1.shape[2]
    tm = ROW_TILE_FFN
    assert dff % FF_CHUNK == 0

    def tile(i, nu):
        return jnp.maximum(jnp.minimum(i, nu[0] - 1), 0)

    hbm = pl.BlockSpec(memory_space=pl.ANY)
    grid_spec = pltpu.PrefetchScalarGridSpec(
        num_scalar_prefetch=2,
        grid=(rg // tm,),
        in_specs=[pl.BlockSpec((tm * ns, LANES), lambda i, te, nu: (tile(i, nu), 0)), hbm, hbm, hbm],
        out_specs=pl.BlockSpec((tm * ns, LANES), lambda i, te, nu: (i, 0)),
        scratch_shapes=[pltpu.VMEM((tm, d), BF16), pltpu.VMEM((tm, d), F32)] + _swiglu_buffers(d, w1.dtype))
    return pl.pallas_call(
        _gffn_kernel,
        grid_spec=grid_spec,
        out_shape=jax.ShapeDtypeStruct((rg * ns, LANES), F32),
        compiler_params=_params(("arbitrary",), 56),
        name="ffn_grouped",
    )(tile_expert, n_used, xg, w1, w3, w2)


def _route_kernel(x_ref, g_ref, sh_ref, sc_ref, wr_ref, br_ref, a_ref, tab_ref, gcol_ref, cnt_ref, carry_ref):
    i = pl.program_id(0)
    tm = x_ref.shape[0]

    @pl.when(i == 0)
    def _():
        carry_ref[...] = jnp.zeros_like(carry_ref)

    a = _rms_mod(x_ref[...], g_ref[...], sh_ref[...], sc_ref[...])
    _to_row_tiles(a, a_ref)
    lane = lax.broadcasted_iota(I32, (tm, LANES), 1).astype(F32)
    logits = jnp.dot(a, wr_ref[...], preferred_element_type=F32, precision=lax.Precision.HIGHEST) + br_ref[...]
    logits = jnp.where(lane < N_EXPERTS, logits, NEG_INF)
    m1 = jnp.max(logits, axis=-1, keepdims=True)
    i1 = jnp.min(jnp.where(logits == m1, lane, float(LANES)), axis=-1, keepdims=True)
    rest = jnp.where(lane == i1, 2.0 * NEG_INF, logits)
    m2 = jnp.max(rest, axis=-1, keepdims=True)
    i2 = jnp.min(jnp.where(rest == m2, lane, float(LANES)), axis=-1, keepdims=True)
    e = jnp.exp(m2 - m1)
    w1 = 1.0 / (1.0 + e)
    w2 = e * w1
    sel1 = lane == i1
    sel2 = lane == i2
    onehot = jnp.where(sel1 | sel2, 1.0, 0.0)
    row = lax.broadcasted_iota(I32, (tm, tm), 0)
    col = lax.broadcasted_iota(I32, (tm, tm), 1)
    before = jnp.where(col < row, 1.0, 0.0).astype(BF16)
    cum = jnp.dot(before, onehot.astype(BF16), preferred_element_type=F32) + carry_ref[...]
    r1 = jnp.sum(jnp.where(sel1, cum, 0.0), axis=-1, keepdims=True)
    r2 = jnp.sum(jnp.where(sel2, cum, 0.0), axis=-1, keepdims=True)
    carry_ref[...] += jnp.sum(onehot, axis=0, keepdims=True)
    cnt_ref[...] = carry_ref[...]
    cols = (jnp.where(lane == 0, i1, 0.0) + jnp.where(lane == 1, i2, 0.0)
            + jnp.where(lane == 2, r1, 0.0) + jnp.where(lane == 3, r2, 0.0))
    tab_ref[...] = cols.T[0:SUBLANES, :]
    gcol_ref[...] = (jnp.where(lane == 0, w1, 0.0) + jnp.where(lane == 1, w2, 0.0))[:, 0:SUBLANES]


def _route_call(h, g, mods, layer, wr, br, rows_per_mod):
    n, d = h.shape
    tm = min(ROW_TILE, n)
    ms = functools.partial(_mod_spec, d, layer=layer, rows_per_mod=rows_per_mod, tm=tm, fixed_row=None)
    return pl.pallas_call(
        _route_kernel,
        grid=(n // tm,),
        in_specs=[pl.BlockSpec((tm, d), lambda i: (i, 0)),
                  pl.BlockSpec((1, d), lambda i: (0, 0)), ms(col=3), ms(col=4),
                  pl.BlockSpec((d, LANES), lambda i: (0, 0)),
                  pl.BlockSpec((1, LANES), lambda i: (0, 0))],
        out_specs=[pl.BlockSpec((tm * (d // LANES), LANES), lambda i: (i, 0)),
                   pl.BlockSpec((SUBLANES, tm), lambda i: (0, i)),
                   pl.BlockSpec((tm, SUBLANES), lambda i: (i, 0)),
                   pl.BlockSpec((1, LANES), lambda i: (0, 0))],
        out_shape=[jax.ShapeDtypeStruct((n * (d // LANES), LANES), F32),
                   jax.ShapeDtypeStruct((SUBLANES, n), F32),
                   jax.ShapeDtypeStruct((n, SUBLANES), F32),
                   jax.ShapeDtypeStruct((1, LANES), F32)],
        scratch_shapes=[pltpu.VMEM((1, LANES), F32)],
        compiler_params=_params(("arbitrary",), 40),
        name="moe_route",
    )(h, g, mods, mods, wr, br)


DMA_UNROLL = 4


FILL_CHUNK = 64


def _dispatch_kernel(off_ref, fill_ref, tab_ref, a_ref, xg_out, pos_ref, sem, *, ns):
    tm = tab_ref.shape[1]

    def rows_copy(src, dst, nrows):
        return pltpu.make_async_copy(a_ref.at[pl.ds(pl.multiple_of(src * ns, ns), nrows * ns)],
                                     xg_out.at[pl.ds(pl.multiple_of(dst * ns, ns), nrows * ns)], sem)

    def row_copy(src, dst):
        return rows_copy(src, dst, 1)

    @pl.when(pl.program_id(0) == pl.num_programs(0) - 1)
    def _():
        nfill = fill_ref.shape[0] // 2
        for e in range(nfill):
            start = fill_ref[e]
            length = fill_ref[nfill + e]
            nbig = length // FILL_CHUNK
            nsmall = length - nbig * FILL_CHUNK
            small0 = start + nbig * FILL_CHUNK

            def big(j, carry, start=start):
                rows_copy(0, start + j * FILL_CHUNK, FILL_CHUNK).start()
                return carry

            def small(j, carry, small0=small0):
                row_copy(0, small0 + j).start()
                return carry

            lax.fori_loop(0, nbig, big, 0)
            lax.fori_loop(0, nsmall, small, 0)
            lax.fori_loop(0, nbig, lambda j, c: (rows_copy(0, 0, FILL_CHUNK).wait(), c)[1], 0)
            lax.fori_loop(0, nsmall, lambda j, c: (row_copy(0, 0).wait(), c)[1], 0)

    def issue(j, carry):
        for u in range(DMA_UNROLL):
            t = j * DMA_UNROLL + u
            for k in range(2):
                p = off_ref[tab_ref[k, t]] + tab_ref[2 + k, t]
                pos_ref[k, t] = p
                row_copy(t, p).start()
        return carry

    lax.fori_loop(0, tm // DMA_UNROLL, issue, 0)

    def drain(j, carry):
        for _ in range(2 * DMA_UNROLL):
            row_copy(0, 0).wait()
        return carry

    lax.fori_loop(0, tm // DMA_UNROLL, drain, 0)


def _dispatch_call(off, fill, tab_i, a, rows_g, ns):
    n = a.shape[0] // ns
    tm = min(ROW_TILE, n)
    assert tm >= FILL_CHUNK
    grid_spec = pltpu.PrefetchScalarGridSpec(
        num_scalar_prefetch=2,
        grid=(n // tm,),
        in_specs=[pl.BlockSpec((4, tm), lambda i, off, fill: (0, i), memory_space=pltpu.SMEM),
                  pl.BlockSpec((tm * ns, LANES), lambda i, off, fill: (i, 0))],
        out_specs=[pl.BlockSpec(memory_space=pl.ANY),
                   pl.BlockSpec((2, tm), lambda i, off, fill: (0, i), memory_space=pltpu.SMEM)],
        scratch_shapes=[pltpu.SemaphoreType.DMA(())])
    return pl.pallas_call(
        functools.partial(_dispatch_kernel, ns=ns),
        grid_spec=grid_spec,
        out_shape=[jax.ShapeDtypeStruct((rows_g * ns, LANES), a.dtype), jax.ShapeDtypeStruct((2, n), I32)],
        compiler_params=_params(("arbitrary",), 16),
        name="moe_dispatch",
    )(off, fill, tab_i, a)


def _combine_kernel(pos_ref, posn_ref, y_hbm, h_ref, gcol_ref, g2_ref, fg_ref, o_ref, buf_ref, sem):
    i = pl.program_id(0)
    n = pl.num_programs(0)
    tm, d = h_ref.shape
    ns = d // LANES

    def row_copy(src, slot, k, t):
        return pltpu.make_async_copy(y_hbm.at[pl.ds(pl.multiple_of(src * ns, ns), ns)],
                                     buf_ref.at[slot, k, pl.ds(pl.multiple_of(t * ns, ns), ns)], sem.at[slot])

    def issue(p_ref, slot):
        def body(j, carry):
            for u in range(DMA_UNROLL):
                t = j * DMA_UNROLL + u
                for k in range(2):
                    row_copy(p_ref[k, t], slot, k, t).start()
            return carry
        lax.fori_loop(0, tm // DMA_UNROLL, body, 0)

    @pl.when(i == 0)
    def _():
        issue(pos_ref, 0)

    @pl.when(i + 1 < n)
    def _():
        issue(posn_ref, (i + 1) % 2)

    slot = i % 2

    def drain(j, carry):
        for _ in range(2 * DMA_UNROLL):
            row_copy(0, slot, 0, 0).wait()
        return carry

    lax.fori_loop(0, tm // DMA_UNROLL, drain, 0)
    y = (gcol_ref[:, 0:1] * _from_row_tiles(buf_ref, (slot, 0), tm, ns)
         + gcol_ref[:, 1:2] * _from_row_tiles(buf_ref, (slot, 1), tm, ns))
    hn = h_ref[...] + g2_ref[...] * y
    o_ref[...] = hn * lax.rsqrt(jnp.mean(hn * hn, axis=-1, keepdims=True) + EPS) * fg_ref[...]


def _combine_call(pos, y, h, gcol, mods, layer, final_g, rows_per_mod):
    n, d = h.shape
    tm = min(GATHER_TILE, n)
    nt = n // tm
    ms = functools.partial(_mod_spec, d, layer=layer, rows_per_mod=rows_per_mod, tm=tm, fixed_row=None)
    return pl.pallas_call(
        _combine_kernel,
        grid=(nt,),
        in_specs=[pl.BlockSpec((2, tm), lambda i: (0, i), memory_space=pltpu.SMEM),
                  pl.BlockSpec((2, tm), lambda i: (0, jnp.minimum(i + 1, nt - 1)), memory_space=pltpu.SMEM),
                  pl.BlockSpec(memory_space=pl.ANY),
                  pl.BlockSpec((tm, d), lambda i: (i, 0)),
                  pl.BlockSpec((tm, SUBLANES), lambda i: (i, 0)),
                  ms(col=5),
                  pl.BlockSpec((1, d), lambda i: (0, 0))],
        out_specs=pl.BlockSpec((tm, d), lambda i: (i, 0)),
        out_shape=jax.ShapeDtypeStruct((n, d), F32),
        scratch_shapes=[pltpu.VMEM((2, 2, tm * (d // LANES), LANES), F32), pltpu.SemaphoreType.DMA((2,))],
        compiler_params=_params(("arbitrary",), 24),
        name="moe_combine",
    )(pos, pos, y, h, gcol, mods, final_g)


def _moe_layer(h, g, mods, layer, router, router_b, w1, w3, w2, final_g, rows_per_mod):
    n, d = h.shape
    wr = jnp.zeros((d, LANES), F32).at[:, :N_EXPERTS].set(router)
    br = jnp.zeros((1, LANES), F32).at[0, :N_EXPERTS].set(router_b)
    a, tab, gcol, cnt = _route_call(h, g, mods, layer, wr, br, rows_per_mod)
    tm = ROW_TILE_FFN
    n_tiles = (2 * n) // tm + N_EXPERTS
    counts = cnt[0, :N_EXPERTS].astype(I32)
    tiles = (counts + tm - 1) // tm
    ends = jnp.cumsum(tiles)
    off = ((ends - tiles) * tm).astype(I32)
    tile_expert = jnp.minimum(jnp.sum(jnp.arange(n_tiles, dtype=I32)[:, None] >= ends[None, :], axis=1),
                              N_EXPERTS - 1).astype(I32)
    n_used = ends[-1:].astype(I32)
    assert d // LANES == SUBLANES, "one token must fill one (8, 128) tile of the row-tile layout"
    used = ends[-1:] * tm
    fill = jnp.concatenate([off + counts, used, tiles * tm - counts, n_tiles * tm - used]).astype(I32)
    xg, pos = _dispatch_call(off, fill, tab[0:4].astype(I32), a, n_tiles * tm, d // LANES)
    y = _gffn_call(tile_expert, n_used, xg, w1, w3, w2)
    return _combine_call(pos, y, h, gcol, mods, layer, final_g, rows_per_mod)


def kernel(x, c, ctx, c_ctx, ada_w, ada_b, mix_norm_g, ffn_norm_g, w_in, w_out, na_rpb, conv_w, conv_b, lru_wa, lru_ba, lru_wx, lru_bx, lru_lam, ffn_w1, ffn_w3, ffn_w2, moe_router, moe_router_b, moe_w1, moe_w3, moe_w2, final_g):
    batch, seq, d = x.shape
    ctx_len = ctx.shape[1]
    depth = ada_w.shape[0]
    lw = conv_w.shape[-1]
    aw = (w_in.shape[-1] - 2 * lw) // 3
    ctx_row = batch

    mod_rows = -(-(batch + 1) // SUBLANES) * SUBLANES
    c_all = jnp.zeros((mod_rows, d), F32).at[:batch].set(c).at[ctx_row].set(c_ctx)
    mods = _ada_call(c_all, ada_w, ada_b).reshape(depth, mod_rows, 1, 6 * d)

    bmods_lat = mods[:, :batch]
    bmods_ctx = jnp.broadcast_to(mods[:, ctx_row:ctx_row + 1], bmods_lat.shape)
    flat = lambda a: a.reshape(a.shape[0] * a.shape[1], a.shape[2])
    h_lat, h_ctx = x, ctx
    out = None
    for i in range(depth):
        last = i == depth - 1
        w_in_bf = w_in[i].astype(BF16)
        w_out_bf = w_out[i].astype(BF16)
        g_mix = mix_norm_g[i].reshape(1, d)
        g_ffn = ffn_norm_g[i].reshape(1, d)
        q_l, k_l, v_l, xr_l, gr_l = _in_proj_call(h_lat, g_mix, bmods_lat, i, w_in_bf, aw, lw)
        q_c, k_c, v_c, xr_c, gr_c = _in_proj_call(h_ctx, g_mix, bmods_ctx, i, w_in_bf, aw, lw)
        oa_l = _na_call(flat(q_l), flat(k_l), flat(v_l), flat(k_c), flat(v_c), _pad_rpb(na_rpb[i]), batch)
        wg, bg = _lru_gate_weights(lru_wa[i], lru_ba[i], lru_wx[i], lru_bx[i], LANES)
        ol_l, ol_c = _lru_call(xr_l, gr_l, xr_c, gr_c, conv_w[i], conv_b[i], wg, bg, lru_lam[i])
        oa_l = oa_l.reshape(batch, seq, aw)
        j = i // 2
        if i % 2 == 0:
            ffn_w = (ffn_w1[j], ffn_w3[j], ffn_w2[j])
            h_lat = _proj_ffn_call(oa_l, ol_l, h_lat, g_ffn, bmods_lat, i, w_out_bf, *ffn_w)
            if not last:
                oa_c = _ctx_attn_call(flat(q_c), flat(k_c), flat(v_c), batch).reshape(batch, ctx_len, aw)
                h_ctx = _proj_ffn_call(oa_c, ol_c, h_ctx, g_ffn, bmods_ctx, i, w_out_bf, *ffn_w)
        else:
            assert last, "the routed layer is fused with the final norm"
            h_lat = _out_proj_call(oa_l, ol_l, h_lat, bmods_lat, i, w_out_bf)
            out = _moe_layer(flat(h_lat), g_ffn, mods, i, moe_router[j], moe_router_b[j],
                             moe_w1[j], moe_w3[j], moe_w2[j], final_g.reshape(1, d), seq)
    return out.reshape(batch, seq, d)
```

```python
import functools

import jax
import jax.numpy as jnp
from jax import lax
from jax.experimental import pallas as pl
from jax.experimental.pallas import tpu as pltpu

F32 = jnp.float32
BF16 = jnp.bfloat16
I32 = jnp.int32

GRID_W = 64
HEAD_DIM = 64
N_HEADS = 8
NA_KH = 8
NA_KW = 16
LRU_BLOCK = 64
LRU_C = 8.0
CONV_W = 4
N_EXPERTS = 8
EPS = 1e-6
NEG_INF = -1e30
LOG2E = 1.4426950408889634
LANES = 128
SUBLANES = 8
VMEM_BYTES = 64 * 1024 * 1024

FF_CHUNK = 512
ROW_TILE_FFN = 1024
ROW_TILE = 512
GATHER_TILE = 256


def _params(semantics, vmem_mb):
    return pltpu.CompilerParams(dimension_semantics=semantics,
                                vmem_limit_bytes=min(vmem_mb * 1024 * 1024, VMEM_BYTES - 4 * 1024 * 1024))


def _rms_mod(x, g, sh, sc):
    y = x * lax.rsqrt(jnp.mean(x * x, axis=-1, keepdims=True) + EPS)
    return (y * g) * (1.0 + sc) + sh


def _ada_kernel(c_ref, w_ref, b_ref, o_ref):
    c = c_ref[...]
    s = c * jax.nn.sigmoid(c)
    o_ref[...] = jnp.dot(s, w_ref[...], preferred_element_type=F32,
                         precision=lax.Precision.HIGHEST) + b_ref[...]


def _ada_call(c_all, ada_w, ada_b):
    depth, d, n = ada_w.shape
    rows = c_all.shape[0]
    tn = 1024
    return pl.pallas_call(
        _ada_kernel,
        grid=(depth, n // tn),
        in_specs=[pl.BlockSpec((rows, d), lambda l, j: (0, 0)),
                  pl.BlockSpec((None, d, tn), lambda l, j: (l, 0, j)),
                  pl.BlockSpec((None, 1, tn), lambda l, j: (l, 0, j))],
        out_specs=pl.BlockSpec((None, rows, tn), lambda l, j: (l, 0, j)),
        out_shape=jax.ShapeDtypeStruct((depth, rows, n), F32),
        compiler_params=_params(("arbitrary", "arbitrary"), 24),
        name="ada_mod",
    )(c_all, ada_w, ada_b.reshape(depth, 1, n))


TIME_TILE = 128


def _to_time_major(r, o_ref, nb, tt):
    for c in range(o_ref.shape[0]):
        for b in range(nb):
            o_ref[c, pl.ds(b, tt, stride=SUBLANES), :] = r[b * tt:(b + 1) * tt, c * LANES:(c + 1) * LANES]


def _from_time_major(x_ref, nb, tt):
    return jnp.concatenate(
        [jnp.concatenate([x_ref[c, pl.ds(b, tt, stride=SUBLANES), :] for c in range(x_ref.shape[0])], axis=1)
         for b in range(nb)], axis=0)


def _in_proj_kernel(x_ref, g_ref, sh_ref, sc_ref, w_ref, q_ref, k_ref, v_ref, xr_ref, gr_ref, *, aw, lw):
    nb, tt, d = x_ref.shape
    a = _rms_mod(x_ref[...], g_ref[...], sh_ref[...], sc_ref[...]).astype(BF16).reshape(nb * tt, d)
    r = jnp.dot(a, w_ref[...], preferred_element_type=F32)
    q_ref[...] = (r[:, 0:aw] * (LOG2E * HEAD_DIM ** -0.5)).astype(BF16).reshape(nb, tt, aw)
    k_ref[...] = r[:, aw:2 * aw].astype(BF16).reshape(nb, tt, aw)
    v_ref[...] = r[:, 2 * aw:3 * aw].astype(BF16).reshape(nb, tt, aw)
    _to_time_major(r[:, 3 * aw:3 * aw + lw], xr_ref, nb, tt)
    _to_time_major(r[:, 3 * aw + lw:3 * aw + 2 * lw], gr_ref, nb, tt)


def _mod_spec(d, col, layer, rows_per_mod, tm, fixed_row):
    if fixed_row is None:
        assert rows_per_mod % tm == 0, "a row tile must not straddle two batch elements"
        return pl.BlockSpec((None, None, 1, d), lambda i, *_: (layer, (i * tm) // rows_per_mod, 0, col))
    return pl.BlockSpec((None, None, 1, d), lambda i, *_: (layer, fixed_row, 0, col))


def _batch_mod_spec(nb, d, layer, col):
    return pl.BlockSpec((None, nb, 1, d), lambda j: (layer, 0, 0, col))


def _in_proj_call(x, g, bmods, layer, w_bf, aw, lw):
    nb, t, d = x.shape
    assert nb == SUBLANES, "time-major rows put the batch on the sublanes"
    tt = min(TIME_TILE, t)
    ncol = w_bf.shape[1]
    ncb = lw // LANES
    bt = lambda w: pl.BlockSpec((nb, tt, w), lambda j: (0, j, 0))
    tm = pl.BlockSpec((ncb, tt * SUBLANES, LANES), lambda j: (0, j, 0))
    return pl.pallas_call(
        functools.partial(_in_proj_kernel, aw=aw, lw=lw),
        grid=(t // tt,),
        in_specs=[bt(d), pl.BlockSpec((1, d), lambda j: (0, 0)),
                  _batch_mod_spec(nb, d, layer, 0), _batch_mod_spec(nb, d, layer, 1),
                  pl.BlockSpec((d, ncol), lambda j: (0, 0))],
        out_specs=[bt(aw), bt(aw), bt(aw), tm, tm],
        out_shape=[jax.ShapeDtypeStruct((nb, t, aw), BF16)] * 3
        + [jax.ShapeDtypeStruct((ncb, t * SUBLANES, LANES), F32)] * 2,
        compiler_params=_params(("arbitrary",), 56),
        name="in_proj",
    )(x, g, bmods, bmods, w_bf)


def _softmax_pv(parts):
    m = None
    for s, _ in parts:
        mi = jnp.max(s, axis=-1, keepdims=True)
        m = mi if m is None else jnp.maximum(m, mi)
    l = None
    acc = None
    for s, v in parts:
        p = jnp.exp2(s - m)
        li = jnp.sum(p, axis=-1, keepdims=True)
        ai = jnp.dot(p.astype(BF16), v, preferred_element_type=F32)
        l = li if l is None else l + li
        acc = ai if acc is None else acc + ai
    return acc * (1.0 / l)


def _qk(q, k):
    return lax.dot_general(q, k, (((1,), (1,)), ((), ())), preferred_element_type=F32)


def _build_na_bias(rpb_ref, bias_ref):
    lane = lax.broadcasted_iota(I32, (GRID_W, LANES), 1)
    q = lax.broadcasted_iota(I32, (GRID_W, LANES), 0)
    kcol = lane & (GRID_W - 1)
    cs = jnp.clip(q - NA_KW // 2, 0, GRID_W - NA_KW)
    ok = (kcol >= cs) & (kcol < cs + NA_KW)
    low = lane < GRID_W
    for delta in range(NA_KH):
        for h in range(N_HEADS):
            for jp in range(NA_KH // 2):
                halves = []
                for j in (2 * jp, 2 * jp + 1):
                    dr = j - delta + NA_KH - 1
                    w = jnp.broadcast_to(rpb_ref[h, dr:dr + 1, :], (GRID_W, LANES))
                    base = (j % 2) * GRID_W - (NA_KW - 1)
                    halves.append(pltpu.roll(w, base % LANES, 1, stride=1, stride_axis=0))
                t = jnp.where(low, halves[0], halves[1])
                bias_ref[delta, h, :, jp * LANES:(jp + 1) * LANES] = jnp.where(ok, t * LOG2E, NEG_INF)


NA_ROWS_PER_STEP = 8


def _stack_pair(q2):
    first = lax.broadcasted_iota(I32, q2.shape, 1) < HEAD_DIM
    zero = jnp.zeros_like(q2)
    return jnp.concatenate([jnp.where(first, q2, zero), jnp.where(first, zero, q2)], axis=0)


def _unstack_pair(o2):
    nq = o2.shape[0] // 2
    first = lax.broadcasted_iota(I32, (nq, o2.shape[1]), 1) < HEAD_DIM
    return jnp.where(first, o2[:nq], o2[nq:])


def _na_kernel(q_ref, kl_ref, vl_ref, kc_ref, vc_ref, rpb_ref, o_ref, bias_ref, s_ref, p_ref, *, rows, rb):
    @pl.when((pl.program_id(0) == 0) & (pl.program_id(1) == 0))
    def _():
        _build_na_bias(rpb_ref, bias_ref)

    nwin = NA_KH * GRID_W
    npair = N_HEADS // 2
    for rho in range(rb):
        r = pl.program_id(1) * rb + rho
        rs = jnp.clip(r - NA_KH // 2, 0, rows - NA_KH)
        delta = r - rs
        k0 = pl.multiple_of(rs * GRID_W, GRID_W)
        qrows = slice(rho * GRID_W, (rho + 1) * GRID_W)
        for pair in range(npair):
            sl = slice(pair * 2 * HEAD_DIM, (pair + 1) * 2 * HEAD_DIM)
            keys = jnp.concatenate([kl_ref[pl.ds(k0, nwin), sl], kc_ref[:, sl]], axis=0)
            s = _qk(_stack_pair(q_ref[qrows, sl]), keys)
            for sub in range(2):
                h = pair * 2 + sub
                rr = slice(sub * GRID_W, (sub + 1) * GRID_W)
                s_ref[rho, h, :, 0:nwin] = s[rr, 0:nwin] + bias_ref[delta, h]
                s_ref[rho, h, :, nwin:] = s[rr, nwin:]
        s = s_ref[rho]
        p = jnp.exp2(s - jnp.max(s, axis=-1, keepdims=True))
        inv = 1.0 / jnp.sum(p, axis=-1, keepdims=True)
        p_ref[rho] = p.astype(BF16)
        for pair in range(npair):
            sl = slice(pair * 2 * HEAD_DIM, (pair + 1) * 2 * HEAD_DIM)
            vals = jnp.concatenate([vl_ref[pl.ds(k0, nwin), sl], vc_ref[:, sl]], axis=0)
            p2 = p_ref[rho, 2 * pair:2 * pair + 2].reshape(2 * GRID_W, p_ref.shape[-1])
            o2 = jnp.dot(p2, vals, preferred_element_type=F32) * inv[2 * pair:2 * pair + 2].reshape(2 * GRID_W, 1)
            o_ref[qrows, sl] = _unstack_pair(o2).astype(o_ref.dtype)


def _na_call(q, kl, vl, kc, vc, rpb, batch):
    s = q.shape[0] // batch
    l = kc.shape[0] // batch
    w = q.shape[1]
    rows = s // GRID_W
    assert rows >= NA_KH
    rb = NA_ROWS_PER_STEP if rows % NA_ROWS_PER_STEP == 0 else 1
    nkeys = NA_KH * GRID_W + l
    nrb = rows // rb
    return pl.pallas_call(
        functools.partial(_na_kernel, rows=rows, rb=rb),
        grid=(batch, nrb),
        in_specs=[pl.BlockSpec((rb * GRID_W, w), lambda b, r: (b * nrb + r, 0)),
                  pl.BlockSpec((s, w), lambda b, r: (b, 0)),
                  pl.BlockSpec((s, w), lambda b, r: (b, 0)),
                  pl.BlockSpec((l, w), lambda b, r: (b, 0)),
                  pl.BlockSpec((l, w), lambda b, r: (b, 0)),
                  pl.BlockSpec(rpb.shape, lambda b, r: (0, 0, 0))],
        out_specs=pl.BlockSpec((rb * GRID_W, w), lambda b, r: (b * nrb + r, 0)),
        out_shape=jax.ShapeDtypeStruct(q.shape, BF16),
        scratch_shapes=[pltpu.VMEM((NA_KH, N_HEADS, GRID_W, NA_KH * GRID_W), F32),
                        pltpu.VMEM((rb, N_HEADS, GRID_W, nkeys), F32),
                        pltpu.VMEM((rb, N_HEADS, GRID_W, nkeys), BF16)],
        compiler_params=_params(("arbitrary", "arbitrary"), 48),
        name="na_attention",
    )(q, kl, vl, kc, vc, rpb)


def _pad_rpb(rpb):
    h, nr, nc = rpb.shape
    return jnp.zeros((h, 2 * NA_KH, LANES), F32).at[:, :nr, :nc].set(rpb.astype(F32))


def _ctx_attn_kernel(q_ref, k_ref, v_ref, o_ref):
    rows = q_ref.shape[0]
    lane = lax.broadcasted_iota(I32, (rows, 2 * HEAD_DIM), 1)
    first = lane < HEAD_DIM
    for pair in range(N_HEADS // 2):
        sl = slice(pair * 2 * HEAD_DIM, (pair + 1) * 2 * HEAD_DIM)
        q2 = q_ref[:, sl]
        k = k_ref[:, sl]
        v = v_ref[:, sl]
        outs = []
        for sub in range(2):
            qh = jnp.where(first if sub == 0 else ~first, q2, jnp.zeros_like(q2))
            outs.append(_softmax_pv([(_qk(qh, k), v)]))
        o_ref[:, sl] = jnp.where(first, outs[0], outs[1]).astype(o_ref.dtype)


def _ctx_attn_call(q, k, v, batch):
    l = q.shape[0] // batch
    w = q.shape[1]
    spec = pl.BlockSpec((l, w), lambda b: (b, 0))
    return pl.pallas_call(
        _ctx_attn_kernel,
        grid=(batch,),
        in_specs=[spec, spec, spec],
        out_specs=spec,
        out_shape=jax.ShapeDtypeStruct(q.shape, BF16),
        compiler_params=_params(("arbitrary",), 24),
        name="ctx_attention",
    )(q, k, v)


def _softplus(x):
    return jnp.maximum(x, 0.0) + jnp.log1p(jnp.exp(-jnp.abs(x)))


LRU_TIME_CHUNK = 64


def _lru_kernel(xl_hbm, xc_hbm, gl_ref, gc_ref, cw_ref, cb_ref, wg_ref, bg_ref, lam_ref, ol_ref, oc_ref,
                padl_ref, padc_ref, sem, *, tchunk):
    c = pl.program_id(0)
    slot = c % 2
    nb = SUBLANES
    halo_lo, halo_hi = nb, 2 * nb
    zero = jnp.zeros((nb, LANES), F32)

    def fetches(block, slot):
        return [pltpu.make_async_copy(x_hbm.at[block], pad_ref.at[slot, pl.ds(halo_lo, x_hbm.shape[1])], sem.at[slot, k])
                for k, (x_hbm, pad_ref) in enumerate(((xc_hbm, padc_ref), (xl_hbm, padl_ref)))]

    @pl.when(c == 0)
    def _():
        for x_hbm, pad_ref in ((xc_hbm, padc_ref), (xl_hbm, padl_ref)):
            n = x_hbm.shape[1]
            for s in range(2):
                pad_ref[s, 0:halo_lo, :] = zero
                pad_ref[s, halo_lo + n:halo_lo + n + halo_hi, :] = jnp.zeros((halo_hi, LANES), F32)
        for cp in fetches(0, 0):
            cp.start()

    @pl.when(c + 1 < pl.num_programs(0))
    def _():
        for cp in fetches(c + 1, 1 - slot):
            cp.start()

    for cp in fetches(c, slot):
        cp.wait()

    la = [(-0.5 * LRU_C * LOG2E) * _softplus(-lam_ref[d:d + 1, :]) for d in range(2)]
    cb = cb_ref[...]
    cw = [cw_ref[j:j + 1, :] for j in range(CONV_W)]
    rows = tchunk * nb

    def segment(pad_ref, g_ref, o_ref, d, h):
        n = o_ref.shape[0] // rows

        def body(i, h):
            r0 = pl.multiple_of((i if d == 0 else n - 1 - i) * rows, rows)
            u = cb
            for j in range(CONV_W):
                u = u + pad_ref[slot, pl.ds(r0 + j * nb, rows), :] * cw[j]
            ub = u.astype(BF16)
            ta = jnp.tanh(jnp.dot(ub, wg_ref[d, 0], preferred_element_type=F32) + bg_ref[d, 0])
            ti = jnp.tanh(jnp.dot(ub, wg_ref[d, 1], preferred_element_type=F32) + bg_ref[d, 1])
            a = jnp.exp2(la[d] * ta + la[d])
            z = 1.0 - a * a
            b = jnp.where(z > 0.0, z * lax.rsqrt(z), 0.0) * (0.5 * ti + 0.5) * u
            hs = [None] * tchunk
            for t in (range(tchunk) if d == 0 else reversed(range(tchunk))):
                h = a[t * nb:(t + 1) * nb] * h + b[t * nb:(t + 1) * nb]
                hs[t] = h
            hcat = jnp.concatenate(hs, axis=0)
            if d == 0:
                o_ref[pl.ds(r0, rows), :] = hcat
            else:
                y = o_ref[pl.ds(r0, rows), :] + hcat
                o_ref[pl.ds(r0, rows), :] = jax.nn.gelu(g_ref[pl.ds(r0, rows), :]) * y
            return h

        return lax.fori_loop(0, n, body, h)

    for d in range(2):
        h = segment(padc_ref, gc_ref, oc_ref, d, zero)
        segment(padl_ref, gl_ref, ol_ref, d, h)


def _lru_call(xr_l, gr_l, xr_c, gr_c, conv_w, conv_b, wg, bg, lam):
    ncb, rl, _ = xr_l.shape
    rc = xr_c.shape[1]
    tchunk = min(LRU_TIME_CHUNK, rl // SUBLANES, rc // SUBLANES)
    halo = 3 * SUBLANES
    lat = pl.BlockSpec((None, rl, LANES), lambda c: (c, 0, 0))
    ctx = pl.BlockSpec((None, rc, LANES), lambda c: (c, 0, 0))
    per_block = lambda a: jnp.moveaxis(a.reshape(a.shape[:-1] + (ncb, LANES)), -2, 0)
    return pl.pallas_call(
        functools.partial(_lru_kernel, tchunk=tchunk),
        grid=(ncb,),
        in_specs=[pl.BlockSpec(memory_space=pl.ANY), pl.BlockSpec(memory_space=pl.ANY),
                  lat, ctx,
                  pl.BlockSpec((None, CONV_W, LANES), lambda c: (c, 0, 0)),
                  pl.BlockSpec((None, 1, LANES), lambda c: (c, 0, 0)),
                  pl.BlockSpec((2, 2, None, LANES, LANES), lambda c: (0, 0, c, 0, 0)),
                  pl.BlockSpec((None, 2, 2, 1, LANES), lambda c: (c, 0, 0, 0, 0)),
                  pl.BlockSpec((None, 2, LANES), lambda c: (c, 0, 0))],
        out_specs=[lat, ctx],
        out_shape=[jax.ShapeDtypeStruct(xr_l.shape, F32), jax.ShapeDtypeStruct(xr_c.shape, F32)],
        scratch_shapes=[pltpu.VMEM((2, rl + halo, LANES), F32), pltpu.VMEM((2, rc + halo, LANES), F32),
                        pltpu.SemaphoreType.DMA((2, 2))],
        compiler_params=_params(("arbitrary",), 58),
        name="rglru",
    )(xr_l, xr_c, gr_l, gr_c, per_block(conv_w), per_block(conv_b.reshape(1, -1)), wg,
      per_block(bg[:, :, None, :]), per_block(lam))


def _lru_gate_weights(wa, ba, wx, bx, half):
    nblk = wa.shape[1]
    lw = nblk * LRU_BLOCK
    per = half // LRU_BLOCK

    def dense(w):
        w = w.reshape(2, nblk // per, per, LRU_BLOCK, LRU_BLOCK)
        eye = jnp.eye(per, dtype=w.dtype)
        full = w[:, :, :, :, None, :] * eye[None, None, :, None, :, None]
        return full.reshape(2, nblk // per, half, half)

    wg = (0.5 * jnp.stack([dense(wa), dense(wx)], axis=1)).astype(BF16)
    bg = 0.5 * jnp.stack([ba, bx], axis=1).astype(F32)
    return wg, bg


def _out_proj_kernel(oa_ref, ol_ref, h_ref, g1_ref, w_ref, o_ref, *, aw):
    o_ref[...] = _mix_proj(oa_ref, ol_ref, h_ref, g1_ref, w_ref, aw)


def _out_proj_call(oa, ol, h, bmods, layer, w_bf):
    nb, t, d = h.shape
    aw = oa.shape[2]
    ncb = ol.shape[0]
    tt = min(TIME_TILE, t)
    bt = lambda w: pl.BlockSpec((nb, tt, w), lambda j: (0, j, 0))
    return pl.pallas_call(
        functools.partial(_out_proj_kernel, aw=aw),
        grid=(t // tt,),
        in_specs=[bt(aw), pl.BlockSpec((ncb, tt * SUBLANES, LANES), lambda j: (0, j, 0)), bt(d),
                  _batch_mod_spec(nb, d, layer, 2), pl.BlockSpec(w_bf.shape, lambda j: (0, 0))],
        out_specs=bt(d),
        out_shape=jax.ShapeDtypeStruct((nb, t, d), F32),
        compiler_params=_params(("arbitrary",), 40),
        name="out_proj",
    )(oa, ol, h, bmods, w_bf)


def _swiglu_step(a_bf, w1_ref, w3_ref, w2_ref, acc_ref):
    g = jnp.dot(a_bf, w1_ref[...].astype(BF16), preferred_element_type=F32)
    u = jnp.dot(a_bf, w3_ref[...].astype(BF16), preferred_element_type=F32)
    hmid = (g * jax.nn.sigmoid(g) * u).astype(BF16)
    acc_ref[...] += jnp.dot(hmid, w2_ref[...].astype(BF16), preferred_element_type=F32)


def _swiglu_streamed(a_ref, acc_ref, w_cur, w_next, first, has_next, tile, bufs, sem):
    w1b, w3b, w2b = bufs
    fc = w1b.shape[2]
    nc = w_cur[0].shape[1] // fc

    def copies(w, c, slot):
        return (pltpu.make_async_copy(w[0].at[:, pl.ds(c * fc, fc)], w1b.at[slot], sem.at[0, slot]),
                pltpu.make_async_copy(w[1].at[:, pl.ds(c * fc, fc)], w3b.at[slot], sem.at[1, slot]),
                pltpu.make_async_copy(w[2].at[pl.ds(c * fc, fc), :], w2b.at[slot], sem.at[2, slot]))

    @pl.when(first)
    def _():
        for cp in copies(w_cur, 0, (tile * nc) % 2):
            cp.start()

    acc_ref[...] = jnp.zeros_like(acc_ref)
    for c in range(nc):
        slot = (tile * nc + c) % 2
        if c + 1 < nc:
            for cp in copies(w_cur, c + 1, 1 - slot):
                cp.start()
        else:
            @pl.when(has_next)
            def _():
                for cp in copies(w_next, 0, 1 - slot):
                    cp.start()
        for cp in copies(w_cur, c, slot):
            cp.wait()
        _swiglu_step(a_ref[...], w1b.at[slot], w3b.at[slot], w2b.at[slot], acc_ref)


def _swiglu_buffers(d, dtype):
    return [pltpu.VMEM((2, d, FF_CHUNK), dtype), pltpu.VMEM((2, d, FF_CHUNK), dtype),
            pltpu.VMEM((2, FF_CHUNK, d), dtype), pltpu.SemaphoreType.DMA((3, 2))]


def _mix_proj(oa_ref, ol_ref, h_ref, g1_ref, wo_ref, aw):
    nb, tt, d = h_ref.shape
    o = jnp.dot(oa_ref[...].reshape(nb * tt, aw), wo_ref[0:aw, :], preferred_element_type=F32)
    ol = _from_time_major(ol_ref, nb, tt).astype(BF16)
    o = o + jnp.dot(ol, wo_ref[aw:, :], preferred_element_type=F32)
    return h_ref[...] + g1_ref[...] * o.reshape(nb, tt, d)


def _proj_ffn_kernel(oa_ref, ol_ref, h_ref, g1_ref, wo_ref, g_ref, sh_ref, sc_ref, g2_ref,
                     w1_hbm, w3_hbm, w2_hbm, o_ref, a_ref, acc_ref, w1b, w3b, w2b, sem, *, aw):
    j = pl.program_id(0)
    nb, tt, d = h_ref.shape
    h1 = _mix_proj(oa_ref, ol_ref, h_ref, g1_ref, wo_ref, aw)
    o_ref[...] = h1
    a_ref[...] = _rms_mod(h1, g_ref[...], sh_ref[...], sc_ref[...]).astype(BF16).reshape(nb * tt, d)
    w = (w1_hbm, w3_hbm, w2_hbm)
    _swiglu_streamed(a_ref, acc_ref, w, w, j == 0, j + 1 < pl.num_programs(0), j, (w1b, w3b, w2b), sem)
    o_ref[...] += g2_ref[...] * acc_ref[...].reshape(nb, tt, d)


def _proj_ffn_call(oa, ol, h, g, bmods, layer, wo_bf, w1, w3, w2):
    nb, t, d = h.shape
    aw = oa.shape[2]
    ncb = ol.shape[0]
    dff = w1.shape[1]
    tt = min(ROW_TILE_FFN // nb, t)
    assert dff % FF_CHUNK == 0
    bt = lambda w: pl.BlockSpec((nb, tt, w), lambda j: (0, j, 0))
    bm = lambda col: pl.BlockSpec((None, nb, 1, d), lambda j: (layer, 0, 0, col))
    hbm = pl.BlockSpec(memory_space=pl.ANY)
    return pl.pallas_call(
        functools.partial(_proj_ffn_kernel, aw=aw),
        grid=(t // tt,),
        in_specs=[bt(aw), pl.BlockSpec((ncb, tt * SUBLANES, LANES), lambda j: (0, j, 0)), bt(d), bm(2),
                  pl.BlockSpec(wo_bf.shape, lambda j: (0, 0), pipeline_mode=pl.Buffered(1)),
                  pl.BlockSpec((1, d), lambda j: (0, 0)), bm(3), bm(4), bm(5), hbm, hbm, hbm],
        out_specs=bt(d),
        out_shape=jax.ShapeDtypeStruct((nb, t, d), F32),
        scratch_shapes=[pltpu.VMEM((nb * tt, d), BF16), pltpu.VMEM((nb * tt, d), F32)]
        + _swiglu_buffers(d, w1.dtype),
        compiler_params=_params(("arbitrary",), 58),
        name="proj_ffn",
    )(oa, ol, h, bmods, wo_bf, g, bmods, bmods, bmods, w1, w3, w2)


def _to_row_tiles(x, o_ref):
    t, d = x.shape
    ns = d // LANES
    for s in range(ns):
        o_ref[pl.ds(s, t, stride=ns), :] = x[:, s * LANES:(s + 1) * LANES]


def _from_row_tiles(x_ref, idx, t, ns):
    return jnp.concatenate([x_ref[idx + (pl.ds(s, t, stride=ns), slice(None))] for s in range(ns)], axis=1)


def _gffn_kernel(te_ref, nu_ref, x_ref, w1_hbm, w3_hbm, w2_hbm, o_ref, a_ref, acc_ref, w1b, w3b, w2b, sem):
    i = pl.program_id(0)
    tm, d = a_ref.shape
    nu = nu_ref[0]

    @pl.when(i < nu)
    def _():
        a_ref[...] = _from_row_tiles(x_ref, (), tm, d // LANES).astype(BF16)
        e = te_ref[i]
        e_next = te_ref[jnp.minimum(i + 1, pl.num_programs(0) - 1)]
        _swiglu_streamed(a_ref, acc_ref, (w1_hbm.at[e], w3_hbm.at[e], w2_hbm.at[e]),
                         (w1_hbm.at[e_next], w3_hbm.at[e_next], w2_hbm.at[e_next]),
                         i == 0, i + 1 < nu, i, (w1b, w3b, w2b), sem)
        _to_row_tiles(acc_ref[...], o_ref)

    @pl.when(i >= nu)
    def _():
        o_ref[...] = jnp.zeros_like(o_ref)


def _gffn_call(tile_expert, n_used, xg, w1, w3, w2):
    d = w1.shape[1]
    ns = d // LANES
    rg = xg.shape[0] // ns
    dff = w1.shape[2]
    tm = ROW_TILE_FFN
    assert dff % FF_CHUNK == 0

    def tile(i, nu):
        return jnp.maximum(jnp.minimum(i, nu[0] - 1), 0)

    hbm = pl.BlockSpec(memory_space=pl.ANY)
    grid_spec = pltpu.PrefetchScalarGridSpec(
        num_scalar_prefetch=2,
        grid=(rg // tm,),
        in_specs=[pl.BlockSpec((tm * ns, LANES), lambda i, te, nu: (tile(i, nu), 0)), hbm, hbm, hbm],
        out_specs=pl.BlockSpec((tm * ns, LANES), lambda i, te, nu: (i, 0)),
        scratch_shapes=[pltpu.VMEM((tm, d), BF16), pltpu.VMEM((tm, d), F32)] + _swiglu_buffers(d, w1.dtype))
    return pl.pallas_call(
        _gffn_kernel,
        grid_spec=grid_spec,
        out_shape=jax.ShapeDtypeStruct((rg * ns, LANES), F32),
        compiler_params=_params(("arbitrary",), 56),
        name="ffn_grouped",
    )(tile_expert, n_used, xg, w1, w3, w2)


def _route_kernel(x_ref, g_ref, sh_ref, sc_ref, wr_ref, br_ref, a_ref, tab_ref, gcol_ref, cnt_ref, carry_ref):
    i = pl.program_id(0)
    tm = x_ref.shape[0]

    @pl.when(i == 0)
    def _():
        carry_ref[...] = jnp.zeros_like(carry_ref)

    a = _rms_mod(x_ref[...], g_ref[...], sh_ref[...], sc_ref[...])
    _to_row_tiles(a, a_ref)
    lane = lax.broadcasted_iota(I32, (tm, LANES), 1).astype(F32)
    logits = jnp.dot(a, wr_ref[...], preferred_element_type=F32, precision=lax.Precision.HIGHEST) + br_ref[...]
    logits = jnp.where(lane < N_EXPERTS, logits, NEG_INF)
    m1 = jnp.max(logits, axis=-1, keepdims=True)
    i1 = jnp.min(jnp.where(logits == m1, lane, float(LANES)), axis=-1, keepdims=True)
    rest = jnp.where(lane == i1, 2.0 * NEG_INF, logits)
    m2 = jnp.max(rest, axis=-1, keepdims=True)
    i2 = jnp.min(jnp.where(rest == m2, lane, float(LANES)), axis=-1, keepdims=True)
    e = jnp.exp(m2 - m1)
    w1 = 1.0 / (1.0 + e)
    w2 = e * w1
    sel1 = lane == i1
    sel2 = lane == i2
    onehot = jnp.where(sel1 | sel2, 1.0, 0.0)
    row = lax.broadcasted_iota(I32, (tm, tm), 0)
    col = lax.broadcasted_iota(I32, (tm, tm), 1)
    before = jnp.where(col < row, 1.0, 0.0).astype(BF16)
    cum = jnp.dot(before, onehot.astype(BF16), preferred_element_type=F32) + carry_ref[...]
    r1 = jnp.sum(jnp.where(sel1, cum, 0.0), axis=-1, keepdims=True)
    r2 = jnp.sum(jnp.where(sel2, cum, 0.0), axis=-1, keepdims=True)
    carry_ref[...] += jnp.sum(onehot, axis=0, keepdims=True)
    cnt_ref[...] = carry_ref[...]
    cols = (jnp.where(lane == 0, i1, 0.0) + jnp.where(lane == 1, i2, 0.0)
            + jnp.where(lane == 2, r1, 0.0) + jnp.where(lane == 3, r2, 0.0))
    tab_ref[...] = cols.T[0:SUBLANES, :]
    gcol_ref[...] = (jnp.where(lane == 0, w1, 0.0) + jnp.where(lane == 1, w2, 0.0))[:, 0:SUBLANES]


def _route_call(h, g, mods, layer, wr, br, rows_per_mod):
    n, d = h.shape
    tm = min(ROW_TILE, n)
    ms = functools.partial(_mod_spec, d, layer=layer, rows_per_mod=rows_per_mod, tm=tm, fixed_row=None)
    return pl.pallas_call(
        _route_kernel,
        grid=(n // tm,),
        in_specs=[pl.BlockSpec((tm, d), lambda i: (i, 0)),
                  pl.BlockSpec((1, d), lambda i: (0, 0)), ms(col=3), ms(col=4),
                  pl.BlockSpec((d, LANES), lambda i: (0, 0)),
                  pl.BlockSpec((1, LANES), lambda i: (0, 0))],
        out_specs=[pl.BlockSpec((tm * (d // LANES), LANES), lambda i: (i, 0)),
                   pl.BlockSpec((SUBLANES, tm), lambda i: (0, i)),
                   pl.BlockSpec((tm, SUBLANES), lambda i: (i, 0)),
                   pl.BlockSpec((1, LANES), lambda i: (0, 0))],
        out_shape=[jax.ShapeDtypeStruct((n * (d // LANES), LANES), F32),
                   jax.ShapeDtypeStruct((SUBLANES, n), F32),
                   jax.ShapeDtypeStruct((n, SUBLANES), F32),
                   jax.ShapeDtypeStruct((1, LANES), F32)],
        scratch_shapes=[pltpu.VMEM((1, LANES), F32)],
        compiler_params=_params(("arbitrary",), 40),
        name="moe_route",
    )(h, g, mods, mods, wr, br)


DMA_UNROLL = 4


FILL_CHUNK = 64


def _dispatch_kernel(off_ref, fill_ref, tab_ref, a_ref, xg_out, pos_ref, sem, *, ns):
    tm = tab_ref.shape[1]

    def rows_copy(src, dst, nrows):
        return pltpu.make_async_copy(a_ref.at[pl.ds(pl.multiple_of(src * ns, ns), nrows * ns)],
                                     xg_out.at[pl.ds(pl.multiple_of(dst * ns, ns), nrows * ns)], sem)

    def row_copy(src, dst):
        return rows_copy(src, dst, 1)

    @pl.when(pl.program_id(0) == pl.num_programs(0) - 1)
    def _():
        nfill = fill_ref.shape[0] // 2
        for e in range(nfill):
            start = fill_ref[e]
            length = fill_ref[nfill + e]
            nbig = length // FILL_CHUNK
            nsmall = length - nbig * FILL_CHUNK
            small0 = start + nbig * FILL_CHUNK

            def big(j, carry, start=start):
                rows_copy(0, start + j * FILL_CHUNK, FILL_CHUNK).start()
                return carry

            def small(j, carry, small0=small0):
                row_copy(0, small0 + j).start()
                return carry

            lax.fori_loop(0, nbig, big, 0)
            lax.fori_loop(0, nsmall, small, 0)
            lax.fori_loop(0, nbig, lambda j, c: (rows_copy(0, 0, FILL_CHUNK).wait(), c)[1], 0)
            lax.fori_loop(0, nsmall, lambda j, c: (row_copy(0, 0).wait(), c)[1], 0)

    def issue(j, carry):
        for u in range(DMA_UNROLL):
            t = j * DMA_UNROLL + u
            for k in range(2):
                p = off_ref[tab_ref[k, t]] + tab_ref[2 + k, t]
                pos_ref[k, t] = p
                row_copy(t, p).start(priority=k)
        return carry

    lax.fori_loop(0, tm // DMA_UNROLL, issue, 0)

    def drain(j, carry):
        for _ in range(2 * DMA_UNROLL):
            row_copy(0, 0).wait()
        return carry

    lax.fori_loop(0, tm // DMA_UNROLL, drain, 0)


def _dispatch_call(off, fill, tab_i, a, rows_g, ns):
    n = a.shape[0] // ns
    tm = min(ROW_TILE, n)
    assert tm >= FILL_CHUNK
    grid_spec = pltpu.PrefetchScalarGridSpec(
        num_scalar_prefetch=2,
        grid=(n // tm,),
        in_specs=[pl.BlockSpec((4, tm), lambda i, off, fill: (0, i), memory_space=pltpu.SMEM),
                  pl.BlockSpec((tm * ns, LANES), lambda i, off, fill: (i, 0))],
        out_specs=[pl.BlockSpec(memory_space=pl.ANY),
                   pl.BlockSpec((2, tm), lambda i, off, fill: (0, i), memory_space=pltpu.SMEM)],
        scratch_shapes=[pltpu.SemaphoreType.DMA(())])
    return pl.pallas_call(
        functools.partial(_dispatch_kernel, ns=ns),
        grid_spec=grid_spec,
        out_shape=[jax.ShapeDtypeStruct((rows_g * ns, LANES), a.dtype), jax.ShapeDtypeStruct((2, n), I32)],
        compiler_params=_params(("arbitrary",), 16),
        name="moe_dispatch",
    )(off, fill, tab_i, a)


def _combine_kernel(pos_ref, posn_ref, y_hbm, h_ref, gcol_ref, g2_ref, fg_ref, o_ref, buf_ref, sem):
    i = pl.program_id(0)
    n = pl.num_programs(0)
    tm, d = h_ref.shape
    ns = d // LANES

    def row_copy(src, slot, k, t):
        return pltpu.make_async_copy(y_hbm.at[pl.ds(pl.multiple_of(src * ns, ns), ns)],
                                     buf_ref.at[slot, k, pl.ds(pl.multiple_of(t * ns, ns), ns)], sem.at[slot])

    def issue(p_ref, slot):
        def body(j, carry):
            for u in range(DMA_UNROLL):
                t = j * DMA_UNROLL + u
                for k in range(2):
                    row_copy(p_ref[k, t], slot, k, t).start(priority=k)
            return carry
        lax.fori_loop(0, tm // DMA_UNROLL, body, 0)

    @pl.when(i == 0)
    def _():
        issue(pos_ref, 0)

    @pl.when(i + 1 < n)
    def _():
        issue(posn_ref, (i + 1) % 2)

    slot = i % 2

    def drain(j, carry):
        for _ in range(2 * DMA_UNROLL):
            row_copy(0, slot, 0, 0).wait()
        return carry

    lax.fori_loop(0, tm // DMA_UNROLL, drain, 0)
    y = (gcol_ref[:, 0:1] * _from_row_tiles(buf_ref, (slot, 0), tm, ns)
         + gcol_ref[:, 1:2] * _from_row_tiles(buf_ref, (slot, 1), tm, ns))
    hn = h_ref[...] + g2_ref[...] * y
    o_ref[...] = hn * lax.rsqrt(jnp.mean(hn * hn, axis=-1, keepdims=True) + EPS) * fg_ref[...]


def _combine_call(pos, y, h, gcol, mods, layer, final_g, rows_per_mod):
    n, d = h.shape
    tm = min(GATHER_TILE, n)
    nt = n // tm
    ms = functools.partial(_mod_spec, d, layer=layer, rows_per_mod=rows_per_mod, tm=tm, fixed_row=None)
    return pl.pallas_call(
        _combine_kernel,
        grid=(nt,),
        in_specs=[pl.BlockSpec((2, tm), lambda i: (0, i), memory_space=pltpu.SMEM),
                  pl.BlockSpec((2, tm), lambda i: (0, jnp.minimum(i + 1, nt - 1)), memory_space=pltpu.SMEM),
                  pl.BlockSpec(memory_space=pl.ANY),
                  pl.BlockSpec((tm, d), lambda i: (i, 0)),
                  pl.BlockSpec((tm, SUBLANES), lambda i: (i, 0)),
                  ms(col=5),
                  pl.BlockSpec((1, d), lambda i: (0, 0))],
        out_specs=pl.BlockSpec((tm, d), lambda i: (i, 0)),
        out_shape=jax.ShapeDtypeStruct((n, d), F32),
        scratch_shapes=[pltpu.VMEM((2, 2, tm * (d // LANES), LANES), F32), pltpu.SemaphoreType.DMA((2,))],
        compiler_params=_params(("arbitrary",), 24),
        name="moe_combine",
    )(pos, pos, y, h, gcol, mods, final_g)


def _moe_layer(h, g, mods, layer, router, router_b, w1, w3, w2, final_g, rows_per_mod):
    n, d = h.shape
    wr = jnp.zeros((d, LANES), F32).at[:, :N_EXPERTS].set(router)
    br = jnp.zeros((1, LANES), F32).at[0, :N_EXPERTS].set(router_b)
    a, tab, gcol, cnt = _route_call(h, g, mods, layer, wr, br, rows_per_mod)
    tm = ROW_TILE_FFN
    n_tiles = (2 * n) // tm + N_EXPERTS
    counts = cnt[0, :N_EXPERTS].astype(I32)
    tiles = (counts + tm - 1) // tm
    ends = jnp.cumsum(tiles)
    off = ((ends - tiles) * tm).astype(I32)
    tile_expert = jnp.minimum(jnp.sum(jnp.arange(n_tiles, dtype=I32)[:, None] >= ends[None, :], axis=1),
                              N_EXPERTS - 1).astype(I32)
    n_used = ends[-1:].astype(I32)
    assert d // LANES == SUBLANES, "one token must fill one (8, 128) tile of the row-tile layout"
    used = ends[-1:] * tm
    fill = jnp.concatenate([off + counts, used, tiles * tm - counts, n_tiles * tm - used]).astype(I32)
    xg, pos = _dispatch_call(off, fill, tab[0:4].astype(I32), a, n_tiles * tm, d // LANES)
    y = _gffn_call(tile_expert, n_used, xg, w1, w3, w2)
    return _combine_call(pos, y, h, gcol, mods, layer, final_g, rows_per_mod)


def kernel(x, c, ctx, c_ctx, ada_w, ada_b, mix_norm_g, ffn_norm_g, w_in, w_out, na_rpb, conv_w, conv_b, lru_wa, lru_ba, lru_wx, lru_bx, lru_lam, ffn_w1, ffn_w3, ffn_w2, moe_router, moe_router_b, moe_w1, moe_w3, moe_w2, final_g):
    batch, seq, d = x.shape
    ctx_len = ctx.shape[1]
    depth = ada_w.shape[0]
    lw = conv_w.shape[-1]
    aw = (w_in.shape[-1] - 2 * lw) // 3
    ctx_row = batch

    mod_rows = -(-(batch + 1) // SUBLANES) * SUBLANES
    c_all = jnp.zeros((mod_rows, d), F32).at[:batch].set(c).at[ctx_row].set(c_ctx)
    mods = _ada_call(c_all, ada_w, ada_b).reshape(depth, mod_rows, 1, 6 * d)

    bmods_lat = mods[:, :batch]
    bmods_ctx = jnp.broadcast_to(mods[:, ctx_row:ctx_row + 1], bmods_lat.shape)
    flat = lambda a: a.reshape(a.shape[0] * a.shape[1], a.shape[2])
    h_lat, h_ctx = x, ctx
    out = None
    for i in range(depth):
        last = i == depth - 1
        w_in_bf = w_in[i].astype(BF16)
        w_out_bf = w_out[i].astype(BF16)
        g_mix = mix_norm_g[i].reshape(1, d)
        g_ffn = ffn_norm_g[i].reshape(1, d)
        q_l, k_l, v_l, xr_l, gr_l = _in_proj_call(h_lat, g_mix, bmods_lat, i, w_in_bf, aw, lw)
        q_c, k_c, v_c, xr_c, gr_c = _in_proj_call(h_ctx, g_mix, bmods_ctx, i, w_in_bf, aw, lw)
        oa_l = _na_call(flat(q_l), flat(k_l), flat(v_l), flat(k_c), flat(v_c), _pad_rpb(na_rpb[i]), batch)
        wg, bg = _lru_gate_weights(lru_wa[i], lru_ba[i], lru_wx[i], lru_bx[i], LANES)
        ol_l, ol_c = _lru_call(xr_l, gr_l, xr_c, gr_c, conv_w[i], conv_b[i], wg, bg, lru_lam[i])
        oa_l = oa_l.reshape(batch, seq, aw)
        j = i // 2
        if i % 2 == 0:
            ffn_w = (ffn_w1[j], ffn_w3[j], ffn_w2[j])
            h_lat = _proj_ffn_call(oa_l, ol_l, h_lat, g_ffn, bmods_lat, i, w_out_bf, *ffn_w)
            if not last:
                oa_c = _ctx_attn_call(flat(q_c), flat(k_c), flat(v_c), batch).reshape(batch, ctx_len, aw)
                h_ctx = _proj_ffn_call(oa_c, ol_c, h_ctx, g_ffn, bmods_ctx, i, w_out_bf, *ffn_w)
        else:
            assert last, "the routed layer is fused with the final norm"
            h_lat = _out_proj_call(oa_l, ol_l, h_lat, bmods_lat, i, w_out_bf)
            out = _moe_layer(flat(h_lat), g_ffn, mods, i, moe_router[j], moe_router_b[j],
                             moe_w1[j], moe_w3[j], moe_w2[j], final_g.reshape(1, d), seq)
    return out.reshape(batch, seq, d)
```

```python
import functools

import jax
import jax.numpy as jnp
from jax import lax
from jax.experimental import pallas as pl
from jax.experimental.pallas import tpu as pltpu

F32 = jnp.float32
BF16 = jnp.bfloat16
I32 = jnp.int32

GRID_W = 64
HEAD_DIM = 64
N_HEADS = 8
NA_KH = 8
NA_KW = 16
LRU_BLOCK = 64
LRU_C = 8.0
CONV_W = 4
N_EXPERTS = 8
EPS = 1e-6
NEG_INF = -1e30
LOG2E = 1.4426950408889634
LANES = 128
SUBLANES = 8
VMEM_BYTES = 64 * 1024 * 1024

FF_CHUNK = 512
ROW_TILE_FFN = 1024
ROW_TILE = 512
GATHER_TILE = 256


def _params(semantics, vmem_mb):
    return pltpu.CompilerParams(dimension_semantics=semantics,
                                vmem_limit_bytes=min(vmem_mb * 1024 * 1024, VMEM_BYTES - 4 * 1024 * 1024))


def _rms_mod(x, g, sh, sc):
    y = x * lax.rsqrt(jnp.mean(x * x, axis=-1, keepdims=True) + EPS)
    return (y * g) * (1.0 + sc) + sh


def _ada_kernel(c_ref, w_ref, b_ref, o_ref):
    c = c_ref[...]
    s = c * jax.nn.sigmoid(c)
    o_ref[...] = jnp.dot(s, w_ref[...], preferred_element_type=F32,
                         precision=lax.Precision.HIGHEST) + b_ref[...]


def _ada_call(c_all, ada_w, ada_b):
    depth, d, n = ada_w.shape
    rows = c_all.shape[0]
    tn = 1024
    return pl.pallas_call(
        _ada_kernel,
        grid=(depth, n // tn),
        in_specs=[pl.BlockSpec((rows, d), lambda l, j: (0, 0)),
                  pl.BlockSpec((None, d, tn), lambda l, j: (l, 0, j)),
                  pl.BlockSpec((None, 1, tn), lambda l, j: (l, 0, j))],
        out_specs=pl.BlockSpec((None, rows, tn), lambda l, j: (l, 0, j)),
        out_shape=jax.ShapeDtypeStruct((depth, rows, n), F32),
        compiler_params=_params(("arbitrary", "arbitrary"), 24),
        name="ada_mod",
    )(c_all, ada_w, ada_b.reshape(depth, 1, n))


TIME_TILE = 128


def _to_time_major(r, o_ref, nb, tt):
    for c in range(o_ref.shape[0]):
        for b in range(nb):
            o_ref[c, pl.ds(b, tt, stride=SUBLANES), :] = r[b * tt:(b + 1) * tt, c * LANES:(c + 1) * LANES]


def _from_time_major(x_ref, nb, tt):
    return jnp.concatenate(
        [jnp.concatenate([x_ref[c, pl.ds(b, tt, stride=SUBLANES), :] for c in range(x_ref.shape[0])], axis=1)
         for b in range(nb)], axis=0)


def _in_proj_kernel(x_ref, g_ref, sh_ref, sc_ref, w_ref, q_ref, k_ref, v_ref, xr_ref, gr_ref, *, aw, lw):
    nb, tt, d = x_ref.shape
    a = _rms_mod(x_ref[...], g_ref[...], sh_ref[...], sc_ref[...]).astype(BF16).reshape(nb * tt, d)
    r = jnp.dot(a, w_ref[...], preferred_element_type=F32)
    q_ref[...] = (r[:, 0:aw] * (LOG2E * HEAD_DIM ** -0.5)).astype(BF16).reshape(nb, tt, aw)
    k_ref[...] = r[:, aw:2 * aw].astype(BF16).reshape(nb, tt, aw)
    v_ref[...] = r[:, 2 * aw:3 * aw].astype(BF16).reshape(nb, tt, aw)
    _to_time_major(r[:, 3 * aw:3 * aw + lw], xr_ref, nb, tt)
    _to_time_major(r[:, 3 * aw + lw:3 * aw + 2 * lw], gr_ref, nb, tt)


def _mod_spec(d, col, layer, rows_per_mod, tm, fixed_row):
    if fixed_row is None:
        assert rows_per_mod % tm == 0, "a row tile must not straddle two batch elements"
        return pl.BlockSpec((None, None, 1, d), lambda i, *_: (layer, (i * tm) // rows_per_mod, 0, col))
    return pl.BlockSpec((None, None, 1, d), lambda i, *_: (layer, fixed_row, 0, col))


def _batch_mod_spec(nb, d, layer, col):
    return pl.BlockSpec((None, nb, 1, d), lambda j: (layer, 0, 0, col))


def _in_proj_call(x, g, bmods, layer, w_bf, aw, lw):
    nb, t, d = x.shape
    assert nb == SUBLANES, "time-major rows put the batch on the sublanes"
    tt = min(TIME_TILE, t)
    ncol = w_bf.shape[1]
    ncb = lw // LANES
    bt = lambda w: pl.BlockSpec((nb, tt, w), lambda j: (0, j, 0))
    tm = pl.BlockSpec((ncb, tt * SUBLANES, LANES), lambda j: (0, j, 0))
    return pl.pallas_call(
        functools.partial(_in_proj_kernel, aw=aw, lw=lw),
        grid=(t // tt,),
        in_specs=[bt(d), pl.BlockSpec((1, d), lambda j: (0, 0)),
                  _batch_mod_spec(nb, d, layer, 0), _batch_mod_spec(nb, d, layer, 1),
                  pl.BlockSpec((d, ncol), lambda j: (0, 0))],
        out_specs=[bt(aw), bt(aw), bt(aw), tm, tm],
        out_shape=[jax.ShapeDtypeStruct((nb, t, aw), BF16)] * 3
        + [jax.ShapeDtypeStruct((ncb, t * SUBLANES, LANES), F32)] * 2,
        compiler_params=_params(("arbitrary",), 56),
        name="in_proj",
    )(x, g, bmods, bmods, w_bf)


def _softmax_pv(parts):
    m = None
    for s, _ in parts:
        mi = jnp.max(s, axis=-1, keepdims=True)
        m = mi if m is None else jnp.maximum(m, mi)
    l = None
    acc = None
    for s, v in parts:
        p = jnp.exp2(s - m)
        li = jnp.sum(p, axis=-1, keepdims=True)
        ai = jnp.dot(p.astype(BF16), v, preferred_element_type=F32)
        l = li if l is None else l + li
        acc = ai if acc is None else acc + ai
    return acc * (1.0 / l)


def _qk(q, k):
    return lax.dot_general(q, k, (((1,), (1,)), ((), ())), preferred_element_type=F32)


def _build_na_bias(rpb_ref, bias_ref):
    lane = lax.broadcasted_iota(I32, (GRID_W, LANES), 1)
    q = lax.broadcasted_iota(I32, (GRID_W, LANES), 0)
    kcol = lane & (GRID_W - 1)
    cs = jnp.clip(q - NA_KW // 2, 0, GRID_W - NA_KW)
    ok = (kcol >= cs) & (kcol < cs + NA_KW)
    low = lane < GRID_W
    for delta in range(NA_KH):
        for h in range(N_HEADS):
            for jp in range(NA_KH // 2):
                halves = []
                for j in (2 * jp, 2 * jp + 1):
                    dr = j - delta + NA_KH - 1
                    w = jnp.broadcast_to(rpb_ref[h, dr:dr + 1, :], (GRID_W, LANES))
                    base = (j % 2) * GRID_W - (NA_KW - 1)
                    halves.append(pltpu.roll(w, base % LANES, 1, stride=1, stride_axis=0))
                t = jnp.where(low, halves[0], halves[1])
                bias_ref[delta, h, :, jp * LANES:(jp + 1) * LANES] = jnp.where(ok, t * LOG2E, NEG_INF)


NA_ROWS_PER_STEP = 8


def _stack_pair(q2):
    first = lax.broadcasted_iota(I32, q2.shape, 1) < HEAD_DIM
    zero = jnp.zeros_like(q2)
    return jnp.concatenate([jnp.where(first, q2, zero), jnp.where(first, zero, q2)], axis=0)


def _unstack_pair(o2):
    nq = o2.shape[0] // 2
    first = lax.broadcasted_iota(I32, (nq, o2.shape[1]), 1) < HEAD_DIM
    return jnp.where(first, o2[:nq], o2[nq:])


def _na_kernel(q_ref, kl_ref, vl_ref, kc_ref, vc_ref, rpb_ref, o_ref, bias_ref, s_ref, p_ref, *, rows, rb):
    @pl.when((pl.program_id(0) == 0) & (pl.program_id(1) == 0))
    def _():
        _build_na_bias(rpb_ref, bias_ref)

    nwin = NA_KH * GRID_W
    npair = N_HEADS // 2
    for rho in range(rb):
        r = pl.program_id(1) * rb + rho
        rs = jnp.clip(r - NA_KH // 2, 0, rows - NA_KH)
        delta = r - rs
        k0 = pl.multiple_of(rs * GRID_W, GRID_W)
        qrows = slice(rho * GRID_W, (rho + 1) * GRID_W)
        for pair in range(npair):
            sl = slice(pair * 2 * HEAD_DIM, (pair + 1) * 2 * HEAD_DIM)
            keys = jnp.concatenate([kl_ref[pl.ds(k0, nwin), sl], kc_ref[:, sl]], axis=0)
            s = _qk(_stack_pair(q_ref[qrows, sl]), keys)
            for sub in range(2):
                h = pair * 2 + sub
                rr = slice(sub * GRID_W, (sub + 1) * GRID_W)
                s_ref[rho, h, :, 0:nwin] = s[rr, 0:nwin] + bias_ref[delta, h]
                s_ref[rho, h, :, nwin:] = s[rr, nwin:]
        s = s_ref[rho]
        p = jnp.exp2(s - jnp.max(s, axis=-1, keepdims=True))
        inv = 1.0 / jnp.sum(p, axis=-1, keepdims=True)
        p_ref[rho] = p.astype(BF16)
        for pair in range(npair):
            sl = slice(pair * 2 * HEAD_DIM, (pair + 1) * 2 * HEAD_DIM)
            vals = jnp.concatenate([vl_ref[pl.ds(k0, nwin), sl], vc_ref[:, sl]], axis=0)
            p2 = p_ref[rho, 2 * pair:2 * pair + 2].reshape(2 * GRID_W, p_ref.shape[-1])
            o2 = jnp.dot(p2, vals, preferred_element_type=F32) * inv[2 * pair:2 * pair + 2].reshape(2 * GRID_W, 1)
            o_ref[qrows, sl] = _unstack_pair(o2).astype(o_ref.dtype)


def _na_call(q, kl, vl, kc, vc, rpb, batch):
    s = q.shape[0] // batch
    l = kc.shape[0] // batch
    w = q.shape[1]
    rows = s // GRID_W
    assert rows >= NA_KH
    rb = NA_ROWS_PER_STEP if rows % NA_ROWS_PER_STEP == 0 else 1
    nkeys = NA_KH * GRID_W + l
    nrb = rows // rb
    return pl.pallas_call(
        functools.partial(_na_kernel, rows=rows, rb=rb),
        grid=(batch, nrb),
        in_specs=[pl.BlockSpec((rb * GRID_W, w), lambda b, r: (b * nrb + r, 0)),
                  pl.BlockSpec((s, w), lambda b, r: (b, 0)),
                  pl.BlockSpec((s, w), lambda b, r: (b, 0)),
                  pl.BlockSpec((l, w), lambda b, r: (b, 0)),
                  pl.BlockSpec((l, w), lambda b, r: (b, 0)),
                  pl.BlockSpec(rpb.shape, lambda b, r: (0, 0, 0))],
        out_specs=pl.BlockSpec((rb * GRID_W, w), lambda b, r: (b * nrb + r, 0)),
        out_shape=jax.ShapeDtypeStruct(q.shape, BF16),
        scratch_shapes=[pltpu.VMEM((NA_KH, N_HEADS, GRID_W, NA_KH * GRID_W), F32),
                        pltpu.VMEM((rb, N_HEADS, GRID_W, nkeys), F32),
                        pltpu.VMEM((rb, N_HEADS, GRID_W, nkeys), BF16)],
        compiler_params=_params(("arbitrary", "arbitrary"), 48),
        name="na_attention",
    )(q, kl, vl, kc, vc, rpb)


def _pad_rpb(rpb):
    h, nr, nc = rpb.shape
    return jnp.zeros((h, 2 * NA_KH, LANES), F32).at[:, :nr, :nc].set(rpb.astype(F32))


def _ctx_attn_kernel(q_ref, k_ref, v_ref, o_ref):
    rows = q_ref.shape[0]
    lane = lax.broadcasted_iota(I32, (rows, 2 * HEAD_DIM), 1)
    first = lane < HEAD_DIM
    for pair in range(N_HEADS // 2):
        sl = slice(pair * 2 * HEAD_DIM, (pair + 1) * 2 * HEAD_DIM)
        q2 = q_ref[:, sl]
        k = k_ref[:, sl]
        v = v_ref[:, sl]
        outs = []
        for sub in range(2):
            qh = jnp.where(first if sub == 0 else ~first, q2, jnp.zeros_like(q2))
            outs.append(_softmax_pv([(_qk(qh, k), v)]))
        o_ref[:, sl] = jnp.where(first, outs[0], outs[1]).astype(o_ref.dtype)


def _ctx_attn_call(q, k, v, batch):
    l = q.shape[0] // batch
    w = q.shape[1]
    spec = pl.BlockSpec((l, w), lambda b: (b, 0))
    return pl.pallas_call(
        _ctx_attn_kernel,
        grid=(batch,),
        in_specs=[spec, spec, spec],
        out_specs=spec,
        out_shape=jax.ShapeDtypeStruct(q.shape, BF16),
        compiler_params=_params(("arbitrary",), 24),
        name="ctx_attention",
    )(q, k, v)


def _softplus(x):
    return jnp.maximum(x, 0.0) + jnp.log1p(jnp.exp(-jnp.abs(x)))


LRU_TIME_CHUNK = 64


def _lru_kernel(xl_hbm, xc_hbm, gl_ref, gc_ref, cw_ref, cb_ref, wg_ref, bg_ref, lam_ref, ol_ref, oc_ref,
                padl_ref, padc_ref, sem, *, tchunk):
    c = pl.program_id(0)
    slot = c % 2
    nb = SUBLANES
    halo_lo, halo_hi = nb, 2 * nb
    zero = jnp.zeros((nb, LANES), F32)

    def fetches(block, slot):
        return [pltpu.make_async_copy(x_hbm.at[block], pad_ref.at[slot, pl.ds(halo_lo, x_hbm.shape[1])], sem.at[slot, k])
                for k, (x_hbm, pad_ref) in enumerate(((xc_hbm, padc_ref), (xl_hbm, padl_ref)))]

    @pl.when(c == 0)
    def _():
        for x_hbm, pad_ref in ((xc_hbm, padc_ref), (xl_hbm, padl_ref)):
            n = x_hbm.shape[1]
            for s in range(2):
                pad_ref[s, 0:halo_lo, :] = zero
                pad_ref[s, halo_lo + n:halo_lo + n + halo_hi, :] = jnp.zeros((halo_hi, LANES), F32)
        for cp in fetches(0, 0):
            cp.start()

    @pl.when(c + 1 < pl.num_programs(0))
    def _():
        for cp in fetches(c + 1, 1 - slot):
            cp.start()

    for cp in fetches(c, slot):
        cp.wait()

    la = [(-0.5 * LRU_C * LOG2E) * _softplus(-lam_ref[d:d + 1, :]) for d in range(2)]
    cb = cb_ref[...]
    cw = [cw_ref[j:j + 1, :] for j in range(CONV_W)]
    rows = tchunk * nb

    def segment(pad_ref, g_ref, o_ref, d, h):
        n = o_ref.shape[0] // rows

        def body(i, h):
            r0 = pl.multiple_of((i if d == 0 else n - 1 - i) * rows, rows)
            u = cb
            for j in range(CONV_W):
                u = u + pad_ref[slot, pl.ds(r0 + j * nb, rows), :] * cw[j]
            ub = u.astype(BF16)
            ta = jnp.tanh(jnp.dot(ub, wg_ref[d, 0], preferred_element_type=F32) + bg_ref[d, 0])
            ti = jnp.tanh(jnp.dot(ub, wg_ref[d, 1], preferred_element_type=F32) + bg_ref[d, 1])
            a = jnp.exp2(la[d] * ta + la[d])
            z = 1.0 - a * a
            b = jnp.where(z > 0.0, z * lax.rsqrt(z), 0.0) * (0.5 * ti + 0.5) * u
            hs = [None] * tchunk
            for t in (range(tchunk) if d == 0 else reversed(range(tchunk))):
                h = a[t * nb:(t + 1) * nb] * h + b[t * nb:(t + 1) * nb]
                hs[t] = h
            hcat = jnp.concatenate(hs, axis=0)
            if d == 0:
                o_ref[pl.ds(r0, rows), :] = hcat
            else:
                y = o_ref[pl.ds(r0, rows), :] + hcat
                o_ref[pl.ds(r0, rows), :] = jax.nn.gelu(g_ref[pl.ds(r0, rows), :]) * y
            return h

        return lax.fori_loop(0, n, body, h)

    for d in range(2):
        h = segment(padc_ref, gc_ref, oc_ref, d, zero)
        segment(padl_ref, gl_ref, ol_ref, d, h)


def _lru_call(xr_l, gr_l, xr_c, gr_c, conv_w, conv_b, wg, bg, lam):
    ncb, rl, _ = xr_l.shape
    rc = xr_c.shape[1]
    tchunk = min(LRU_TIME_CHUNK, rl // SUBLANES, rc // SUBLANES)
    halo = 3 * SUBLANES
    lat = pl.BlockSpec((None, rl, LANES), lambda c: (c, 0, 0))
    ctx = pl.BlockSpec((None, rc, LANES), lambda c: (c, 0, 0))
    per_block = lambda a: jnp.moveaxis(a.reshape(a.shape[:-1] + (ncb, LANES)), -2, 0)
    return pl.pallas_call(
        functools.partial(_lru_kernel, tchunk=tchunk),
        grid=(ncb,),
        in_specs=[pl.BlockSpec(memory_space=pl.ANY), pl.BlockSpec(memory_space=pl.ANY),
                  lat, ctx,
                  pl.BlockSpec((None, CONV_W, LANES), lambda c: (c, 0, 0)),
                  pl.BlockSpec((None, 1, LANES), lambda c: (c, 0, 0)),
                  pl.BlockSpec((2, 2, None, LANES, LANES), lambda c: (0, 0, c, 0, 0)),
                  pl.BlockSpec((None, 2, 2, 1, LANES), lambda c: (c, 0, 0, 0, 0)),
                  pl.BlockSpec((None, 2, LANES), lambda c: (c, 0, 0))],
        out_specs=[lat, ctx],
        out_shape=[jax.ShapeDtypeStruct(xr_l.shape, F32), jax.ShapeDtypeStruct(xr_c.shape, F32)],
        scratch_shapes=[pltpu.VMEM((2, rl + halo, LANES), F32), pltpu.VMEM((2, rc + halo, LANES), F32),
                        pltpu.SemaphoreType.DMA((2, 2))],
        compiler_params=_params(("arbitrary",), 58),
        name="rglru",
    )(xr_l, xr_c, gr_l, gr_c, per_block(conv_w), per_block(conv_b.reshape(1, -1)), wg,
      per_block(bg[:, :, None, :]), per_block(lam))


def _lru_gate_weights(wa, ba, wx, bx, half):
    nblk = wa.shape[1]
    lw = nblk * LRU_BLOCK
    per = half // LRU_BLOCK

    def dense(w):
        w = w.reshape(2, nblk // per, per, LRU_BLOCK, LRU_BLOCK)
        eye = jnp.eye(per, dtype=w.dtype)
        full = w[:, :, :, :, None, :] * eye[None, None, :, None, :, None]
        return full.reshape(2, nblk // per, half, half)

    wg = (0.5 * jnp.stack([dense(wa), dense(wx)], axis=1)).astype(BF16)
    bg = 0.5 * jnp.stack([ba, bx], axis=1).astype(F32)
    return wg, bg


def _out_proj_kernel(oa_ref, ol_ref, h_ref, g1_ref, w_ref, o_ref, *, aw):
    o_ref[...] = _mix_proj(oa_ref, ol_ref, h_ref, g1_ref, w_ref, aw)


def _out_proj_call(oa, ol, h, bmods, layer, w_bf):
    nb, t, d = h.shape
    aw = oa.shape[2]
    ncb = ol.shape[0]
    tt = min(TIME_TILE, t)
    bt = lambda w: pl.BlockSpec((nb, tt, w), lambda j: (0, j, 0))
    return pl.pallas_call(
        functools.partial(_out_proj_kernel, aw=aw),
        grid=(t // tt,),
        in_specs=[bt(aw), pl.BlockSpec((ncb, tt * SUBLANES, LANES), lambda j: (0, j, 0)), bt(d),
                  _batch_mod_spec(nb, d, layer, 2), pl.BlockSpec(w_bf.shape, lambda j: (0, 0))],
        out_specs=bt(d),
        out_shape=jax.ShapeDtypeStruct((nb, t, d), F32),
        compiler_params=_params(("arbitrary",), 40),
        name="out_proj",
    )(oa, ol, h, bmods, w_bf)


def _swiglu_step(a_bf, w1_ref, w3_ref, w2_ref, acc_ref):
    g = jnp.dot(a_bf, w1_ref[...].astype(BF16), preferred_element_type=F32)
    u = jnp.dot(a_bf, w3_ref[...].astype(BF16), preferred_element_type=F32)
    hmid = (g * jax.nn.sigmoid(g) * u).astype(BF16)
    acc_ref[...] += jnp.dot(hmid, w2_ref[...].astype(BF16), preferred_element_type=F32)


def _swiglu_streamed(a_ref, acc_ref, w_cur, w_next, first, has_next, tile, bufs, sem):
    w1b, w3b, w2b = bufs
    fc = w1b.shape[2]
    nc = w_cur[0].shape[1] // fc

    def copies(w, c, slot):
        return (pltpu.make_async_copy(w[0].at[:, pl.ds(c * fc, fc)], w1b.at[slot], sem.at[0, slot]),
                pltpu.make_async_copy(w[1].at[:, pl.ds(c * fc, fc)], w3b.at[slot], sem.at[1, slot]),
                pltpu.make_async_copy(w[2].at[pl.ds(c * fc, fc), :], w2b.at[slot], sem.at[2, slot]))

    @pl.when(first)
    def _():
        for cp in copies(w_cur, 0, (tile * nc) % 2):
            cp.start()

    acc_ref[...] = jnp.zeros_like(acc_ref)
    for c in range(nc):
        slot = (tile * nc + c) % 2
        if c + 1 < nc:
            for cp in copies(w_cur, c + 1, 1 - slot):
                cp.start()
        else:
            @pl.when(has_next)
            def _():
                for cp in copies(w_next, 0, 1 - slot):
                    cp.start()
        for cp in copies(w_cur, c, slot):
            cp.wait()
        _swiglu_step(a_ref[...], w1b.at[slot], w3b.at[slot], w2b.at[slot], acc_ref)


def _swiglu_buffers(d, dtype):
    return [pltpu.VMEM((2, d, FF_CHUNK), dtype), pltpu.VMEM((2, d, FF_CHUNK), dtype),
            pltpu.VMEM((2, FF_CHUNK, d), dtype), pltpu.SemaphoreType.DMA((3, 2))]


def _mix_proj(oa_ref, ol_ref, h_ref, g1_ref, wo_ref, aw):
    nb, tt, d = h_ref.shape
    o = jnp.dot(oa_ref[...].reshape(nb * tt, aw), wo_ref[0:aw, :], preferred_element_type=F32)
    ol = _from_time_major(ol_ref, nb, tt).astype(BF16)
    o = o + jnp.dot(ol, wo_ref[aw:, :], preferred_element_type=F32)
    return h_ref[...] + g1_ref[...] * o.reshape(nb, tt, d)


def _proj_ffn_kernel(oa_ref, ol_ref, h_ref, g1_ref, wo_ref, g_ref, sh_ref, sc_ref, g2_ref,
                     w1_hbm, w3_hbm, w2_hbm, o_ref, a_ref, acc_ref, w1b, w3b, w2b, sem, *, aw):
    j = pl.program_id(0)
    nb, tt, d = h_ref.shape
    h1 = _mix_proj(oa_ref, ol_ref, h_ref, g1_ref, wo_ref, aw)
    o_ref[...] = h1
    a_ref[...] = _rms_mod(h1, g_ref[...], sh_ref[...], sc_ref[...]).astype(BF16).reshape(nb * tt, d)
    w = (w1_hbm, w3_hbm, w2_hbm)
    _swiglu_streamed(a_ref, acc_ref, w, w, j == 0, j + 1 < pl.num_programs(0), j, (w1b, w3b, w2b), sem)
    o_ref[...] += g2_ref[...] * acc_ref[...].reshape(nb, tt, d)


def _proj_ffn_call(oa, ol, h, g, bmods, layer, wo_bf, w1, w3, w2):
    nb, t, d = h.shape
    aw = oa.shape[2]
    ncb = ol.shape[0]
    dff = w1.shape[1]
    tt = min(ROW_TILE_FFN // nb, t)
    assert dff % FF_CHUNK == 0
    bt = lambda w: pl.BlockSpec((nb, tt, w), lambda j: (0, j, 0))
    bm = lambda col: pl.BlockSpec((None, nb, 1, d), lambda j: (layer, 0, 0, col))
    hbm = pl.BlockSpec(memory_space=pl.ANY)
    return pl.pallas_call(
        functools.partial(_proj_ffn_kernel, aw=aw),
        grid=(t // tt,),
        in_specs=[bt(aw), pl.BlockSpec((ncb, tt * SUBLANES, LANES), lambda j: (0, j, 0)), bt(d), bm(2),
                  pl.BlockSpec(wo_bf.shape, lambda j: (0, 0), pipeline_mode=pl.Buffered(1)),
                  pl.BlockSpec((1, d), lambda j: (0, 0)), bm(3), bm(4), bm(5), hbm, hbm, hbm],
        out_specs=bt(d),
        out_shape=jax.ShapeDtypeStruct((nb, t, d), F32),
        scratch_shapes=[pltpu.VMEM((nb * tt, d), BF16), pltpu.VMEM((nb * tt, d), F32)]
        + _swiglu_buffers(d, w1.dtype),
        compiler_params=_params(("arbitrary",), 58),
        name="proj_ffn",
    )(oa, ol, h, bmods, wo_bf, g, bmods, bmods, bmods, w1, w3, w2)


def _to_row_tiles(x, o_ref):
    t, d = x.shape
    ns = d // LANES
    for s in range(ns):
        o_ref[pl.ds(s, t, stride=ns), :] = x[:, s * LANES:(s + 1) * LANES]


def _from_row_tiles(x_ref, idx, t, ns):
    return jnp.concatenate([x_ref[idx + (pl.ds(s, t, stride=ns), slice(None))] for s in range(ns)], axis=1)


def _gffn_kernel(te_ref, nu_ref, half_ref, x_ref, w1_hbm, w3_hbm, w2_hbm, o_ref, a_ref, acc_ref,
                 w1b, w3b, w2b, sem):
    i = pl.program_id(0)
    tm, d = a_ref.shape
    nu = nu_ref[0]

    @pl.when(i < nu)
    def _():
        a_ref[...] = _from_row_tiles(x_ref, (), tm, d // LANES).astype(BF16)
        e = te_ref[i]
        e_next = te_ref[jnp.minimum(i + 1, pl.num_programs(0) - 1)]

        def run(rows):
            av = a_ref if rows == tm else a_ref.at[pl.ds(0, rows)]
            accv = acc_ref if rows == tm else acc_ref.at[pl.ds(0, rows)]
            _swiglu_streamed(av, accv, (w1_hbm.at[e], w3_hbm.at[e], w2_hbm.at[e]),
                             (w1_hbm.at[e_next], w3_hbm.at[e_next], w2_hbm.at[e_next]),
                             i == 0, i + 1 < nu, i, (w1b, w3b, w2b), sem)

        @pl.when(half_ref[i] == 0)
        def _():
            run(tm)

        @pl.when(half_ref[i] != 0)
        def _():
            acc_ref[pl.ds(tm // 2, tm // 2), :] = jnp.zeros((tm // 2, d), F32)
            run(tm // 2)

        _to_row_tiles(acc_ref[...], o_ref)

    @pl.when(i >= nu)
    def _():
        o_ref[...] = jnp.zeros_like(o_ref)


def _gffn_call(tile_expert, n_used, tile_half, xg, w1, w3, w2):
    d = w1.shape[1]
    ns = d // LANES
    rg = xg.shape[0] // ns
    dff = w1.shape[2]
    tm = ROW_TILE_FFN
    assert dff % FF_CHUNK == 0

    def tile(i, nu):
        return jnp.maximum(jnp.minimum(i, nu[0] - 1), 0)

    hbm = pl.BlockSpec(memory_space=pl.ANY)
    grid_spec = pltpu.PrefetchScalarGridSpec(
        num_scalar_prefetch=3,
        grid=(rg // tm,),
        in_specs=[pl.BlockSpec((tm * ns, LANES), lambda i, te, nu, hv: (tile(i, nu), 0)), hbm, hbm, hbm],
        out_specs=pl.BlockSpec((tm * ns, LANES), lambda i, te, nu, hv: (i, 0)),
        scratch_shapes=[pltpu.VMEM((tm, d), BF16), pltpu.VMEM((tm, d), F32)] + _swiglu_buffers(d, w1.dtype))
    return pl.pallas_call(
        _gffn_kernel,
        grid_spec=grid_spec,
        out_shape=jax.ShapeDtypeStruct((rg * ns, LANES), F32),
        compiler_params=_params(("arbitrary",), 56),
        name="ffn_grouped",
    )(tile_expert, n_used, tile_half, xg, w1, w3, w2)


def _route_kernel(x_ref, g_ref, sh_ref, sc_ref, wr_ref, br_ref, a_ref, tab_ref, gcol_ref, cnt_ref, carry_ref):
    i = pl.program_id(0)
    tm = x_ref.shape[0]

    @pl.when(i == 0)
    def _():
        carry_ref[...] = jnp.zeros_like(carry_ref)

    a = _rms_mod(x_ref[...], g_ref[...], sh_ref[...], sc_ref[...])
    _to_row_tiles(a, a_ref)
    lane = lax.broadcasted_iota(I32, (tm, LANES), 1).astype(F32)
    logits = jnp.dot(a, wr_ref[...], preferred_element_type=F32, precision=lax.Precision.HIGHEST) + br_ref[...]
    logits = jnp.where(lane < N_EXPERTS, logits, NEG_INF)
    m1 = jnp.max(logits, axis=-1, keepdims=True)
    i1 = jnp.min(jnp.where(logits == m1, lane, float(LANES)), axis=-1, keepdims=True)
    rest = jnp.where(lane == i1, 2.0 * NEG_INF, logits)
    m2 = jnp.max(rest, axis=-1, keepdims=True)
    i2 = jnp.min(jnp.where(rest == m2, lane, float(LANES)), axis=-1, keepdims=True)
    e = jnp.exp(m2 - m1)
    w1 = 1.0 / (1.0 + e)
    w2 = e * w1
    sel1 = lane == i1
    sel2 = lane == i2
    onehot = jnp.where(sel1 | sel2, 1.0, 0.0)
    row = lax.broadcasted_iota(I32, (tm, tm), 0)
    col = lax.broadcasted_iota(I32, (tm, tm), 1)
    before = jnp.where(col < row, 1.0, 0.0).astype(BF16)
    cum = jnp.dot(before, onehot.astype(BF16), preferred_element_type=F32) + carry_ref[...]
    r1 = jnp.sum(jnp.where(sel1, cum, 0.0), axis=-1, keepdims=True)
    r2 = jnp.sum(jnp.where(sel2, cum, 0.0), axis=-1, keepdims=True)
    carry_ref[...] += jnp.sum(onehot, axis=0, keepdims=True)
    cnt_ref[...] = carry_ref[...]
    cols = (jnp.where(lane == 0, i1, 0.0) + jnp.where(lane == 1, i2, 0.0)
            + jnp.where(lane == 2, r1, 0.0) + jnp.where(lane == 3, r2, 0.0))
    tab_ref[...] = cols.T[0:SUBLANES, :]
    gcol_ref[...] = (jnp.where(lane == 0, w1, 0.0) + jnp.where(lane == 1, w2, 0.0))[:, 0:SUBLANES]


def _route_call(h, g, mods, layer, wr, br, rows_per_mod):
    n, d = h.shape
    tm = min(ROW_TILE, n)
    ms = functools.partial(_mod_spec, d, layer=layer, rows_per_mod=rows_per_mod, tm=tm, fixed_row=None)
    return pl.pallas_call(
        _route_kernel,
        grid=(n // tm,),
        in_specs=[pl.BlockSpec((tm, d), lambda i: (i, 0)),
                  pl.BlockSpec((1, d), lambda i: (0, 0)), ms(col=3), ms(col=4),
                  pl.BlockSpec((d, LANES), lambda i: (0, 0)),
                  pl.BlockSpec((1, LANES), lambda i: (0, 0))],
        out_specs=[pl.BlockSpec((tm * (d // LANES), LANES), lambda i: (i, 0)),
                   pl.BlockSpec((SUBLANES, tm), lambda i: (0, i)),
                   pl.BlockSpec((tm, SUBLANES), lambda i: (i, 0)),
                   pl.BlockSpec((1, LANES), lambda i: (0, 0))],
        out_shape=[jax.ShapeDtypeStruct((n * (d // LANES), LANES), F32),
                   jax.ShapeDtypeStruct((SUBLANES, n), F32),
                   jax.ShapeDtypeStruct((n, SUBLANES), F32),
                   jax.ShapeDtypeStruct((1, LANES), F32)],
        scratch_shapes=[pltpu.VMEM((1, LANES), F32)],
        compiler_params=_params(("arbitrary",), 40),
        name="moe_route",
    )(h, g, mods, mods, wr, br)


DMA_UNROLL = 4


FILL_CHUNK = 64


def _dispatch_kernel(off_ref, fill_ref, tab_ref, a_ref, xg_out, pos_ref, sem, *, ns):
    tm = tab_ref.shape[1]

    def rows_copy(src, dst, nrows):
        return pltpu.make_async_copy(a_ref.at[pl.ds(pl.multiple_of(src * ns, ns), nrows * ns)],
                                     xg_out.at[pl.ds(pl.multiple_of(dst * ns, ns), nrows * ns)], sem)

    def row_copy(src, dst):
        return rows_copy(src, dst, 1)

    @pl.when(pl.program_id(0) == pl.num_programs(0) - 1)
    def _():
        nfill = fill_ref.shape[0] // 2
        for e in range(nfill):
            start = fill_ref[e]
            length = fill_ref[nfill + e]
            nbig = length // FILL_CHUNK
            nsmall = length - nbig * FILL_CHUNK
            small0 = start + nbig * FILL_CHUNK

            def big(j, carry, start=start):
                rows_copy(0, start + j * FILL_CHUNK, FILL_CHUNK).start()
                return carry

            def small(j, carry, small0=small0):
                row_copy(0, small0 + j).start()
                return carry

            lax.fori_loop(0, nbig, big, 0)
            lax.fori_loop(0, nsmall, small, 0)
            lax.fori_loop(0, nbig, lambda j, c: (rows_copy(0, 0, FILL_CHUNK).wait(), c)[1], 0)
            lax.fori_loop(0, nsmall, lambda j, c: (row_copy(0, 0).wait(), c)[1], 0)

    def issue(j, carry):
        for u in range(DMA_UNROLL):
            t = j * DMA_UNROLL + u
            for k in range(2):
                p = off_ref[tab_ref[k, t]] + tab_ref[2 + k, t]
                pos_ref[k, t] = p
                row_copy(t, p).start()
        return carry

    lax.fori_loop(0, tm // DMA_UNROLL, issue, 0)

    def drain(j, carry):
        for _ in range(2 * DMA_UNROLL):
            row_copy(0, 0).wait()
        return carry

    lax.fori_loop(0, tm // DMA_UNROLL, drain, 0)


def _dispatch_call(off, fill, tab_i, a, rows_g, ns):
    n = a.shape[0] // ns
    tm = min(ROW_TILE, n)
    assert tm >= FILL_CHUNK
    grid_spec = pltpu.PrefetchScalarGridSpec(
        num_scalar_prefetch=2,
        grid=(n // tm,),
        in_specs=[pl.BlockSpec((4, tm), lambda i, off, fill: (0, i), memory_space=pltpu.SMEM),
                  pl.BlockSpec((tm * ns, LANES), lambda i, off, fill: (i, 0))],
        out_specs=[pl.BlockSpec(memory_space=pl.ANY),
                   pl.BlockSpec((2, tm), lambda i, off, fill: (0, i), memory_space=pltpu.SMEM)],
        scratch_shapes=[pltpu.SemaphoreType.DMA(())])
    return pl.pallas_call(
        functools.partial(_dispatch_kernel, ns=ns),
        grid_spec=grid_spec,
        out_shape=[jax.ShapeDtypeStruct((rows_g * ns, LANES), a.dtype), jax.ShapeDtypeStruct((2, n), I32)],
        compiler_params=_params(("arbitrary",), 16),
        name="moe_dispatch",
    )(off, fill, tab_i, a)


def _combine_kernel(pos_ref, posn_ref, y_hbm, h_ref, gcol_ref, g2_ref, fg_ref, o_ref, buf_ref, sem):
    i = pl.program_id(0)
    n = pl.num_programs(0)
    tm, d = h_ref.shape
    ns = d // LANES

    def row_copy(src, slot, k, t):
        return pltpu.make_async_copy(y_hbm.at[pl.ds(pl.multiple_of(src * ns, ns), ns)],
                                     buf_ref.at[slot, k, pl.ds(pl.multiple_of(t * ns, ns), ns)], sem.at[slot])

    def issue(p_ref, slot):
        def body(j, carry):
            for u in range(DMA_UNROLL):
                t = j * DMA_UNROLL + u
                for k in range(2):
                    row_copy(p_ref[k, t], slot, k, t).start()
            return carry
        lax.fori_loop(0, tm // DMA_UNROLL, body, 0)

    @pl.when(i == 0)
    def _():
        issue(pos_ref, 0)

    @pl.when(i + 1 < n)
    def _():
        issue(posn_ref, (i + 1) % 2)

    slot = i % 2

    def drain(j, carry):
        for _ in range(2 * DMA_UNROLL):
            row_copy(0, slot, 0, 0).wait()
        return carry

    lax.fori_loop(0, tm // DMA_UNROLL, drain, 0)
    y = (gcol_ref[:, 0:1] * _from_row_tiles(buf_ref, (slot, 0), tm, ns)
         + gcol_ref[:, 1:2] * _from_row_tiles(buf_ref, (slot, 1), tm, ns))
    hn = h_ref[...] + g2_ref[...] * y
    o_ref[...] = hn * lax.rsqrt(jnp.mean(hn * hn, axis=-1, keepdims=True) + EPS) * fg_ref[...]


def _combine_call(pos, y, h, gcol, mods, layer, final_g, rows_per_mod):
    n, d = h.shape
    tm = min(GATHER_TILE, n)
    nt = n // tm
    ms = functools.partial(_mod_spec, d, layer=layer, rows_per_mod=rows_per_mod, tm=tm, fixed_row=None)
    return pl.pallas_call(
        _combine_kernel,
        grid=(nt,),
        in_specs=[pl.BlockSpec((2, tm), lambda i: (0, i), memory_space=pltpu.SMEM),
                  pl.BlockSpec((2, tm), lambda i: (0, jnp.minimum(i + 1, nt - 1)), memory_space=pltpu.SMEM),
                  pl.BlockSpec(memory_space=pl.ANY),
                  pl.BlockSpec((tm, d), lambda i: (i, 0)),
                  pl.BlockSpec((tm, SUBLANES), lambda i: (i, 0)),
                  ms(col=5),
                  pl.BlockSpec((1, d), lambda i: (0, 0))],
        out_specs=pl.BlockSpec((tm, d), lambda i: (i, 0)),
        out_shape=jax.ShapeDtypeStruct((n, d), F32),
        scratch_shapes=[pltpu.VMEM((2, 2, tm * (d // LANES), LANES), F32), pltpu.SemaphoreType.DMA((2,))],
        compiler_params=_params(("arbitrary",), 24),
        name="moe_combine",
    )(pos, pos, y, h, gcol, mods, final_g)


def _moe_layer(h, g, mods, layer, router, router_b, w1, w3, w2, final_g, rows_per_mod):
    n, d = h.shape
    wr = jnp.zeros((d, LANES), F32).at[:, :N_EXPERTS].set(router)
    br = jnp.zeros((1, LANES), F32).at[0, :N_EXPERTS].set(router_b)
    a, tab, gcol, cnt = _route_call(h, g, mods, layer, wr, br, rows_per_mod)
    tm = ROW_TILE_FFN
    n_tiles = (2 * n) // tm + N_EXPERTS
    counts = cnt[0, :N_EXPERTS].astype(I32)
    tiles = (counts + tm - 1) // tm
    ends = jnp.cumsum(tiles)
    off = ((ends - tiles) * tm).astype(I32)
    tile_expert = jnp.minimum(jnp.sum(jnp.arange(n_tiles, dtype=I32)[:, None] >= ends[None, :], axis=1),
                              N_EXPERTS - 1).astype(I32)
    n_used = ends[-1:].astype(I32)
    assert d // LANES == SUBLANES, "one token must fill one (8, 128) tile of the row-tile layout"
    used = ends[-1:] * tm
    fill = jnp.concatenate([off + counts, used, tiles * tm - counts, n_tiles * tm - used]).astype(I32)
    xg, pos = _dispatch_call(off, fill, tab[0:4].astype(I32), a, n_tiles * tm, d // LANES)
    mine = (tile_expert[:, None] == jnp.arange(N_EXPERTS, dtype=I32)[None, :]).astype(I32)
    local = jnp.arange(n_tiles, dtype=I32) - jnp.sum(mine * (ends - tiles)[None, :], axis=1)
    valid = jnp.clip(jnp.sum(mine * counts[None, :], axis=1) - local * tm, 0, tm)
    tile_half = (valid <= tm // 2).astype(I32)
    y = _gffn_call(tile_expert, n_used, tile_half, xg, w1, w3, w2)
    return _combine_call(pos, y, h, gcol, mods, layer, final_g, rows_per_mod)


def kernel(x, c, ctx, c_ctx, ada_w, ada_b, mix_norm_g, ffn_norm_g, w_in, w_out, na_rpb, conv_w, conv_b, lru_wa, lru_ba, lru_wx, lru_bx, lru_lam, ffn_w1, ffn_w3, ffn_w2, moe_router, moe_router_b, moe_w1, moe_w3, moe_w2, final_g):
    batch, seq, d = x.shape
    ctx_len = ctx.shape[1]
    depth = ada_w.shape[0]
    lw = conv_w.shape[-1]
    aw = (w_in.shape[-1] - 2 * lw) // 3
    ctx_row = batch

    mod_rows = -(-(batch + 1) // SUBLANES) * SUBLANES
    c_all = jnp.zeros((mod_rows, d), F32).at[:batch].set(c).at[ctx_row].set(c_ctx)
    mods = _ada_call(c_all, ada_w, ada_b).reshape(depth, mod_rows, 1, 6 * d)

    bmods_lat = mods[:, :batch]
    bmods_ctx = jnp.broadcast_to(mods[:, ctx_row:ctx_row + 1], bmods_lat.shape)
    flat = lambda a: a.reshape(a.shape[0] * a.shape[1], a.shape[2])
    h_lat, h_ctx = x, ctx
    out = None
    for i in range(depth):
        last = i == depth - 1
        w_in_bf = w_in[i].astype(BF16)
        w_out_bf = w_out[i].astype(BF16)
        g_mix = mix_norm_g[i].reshape(1, d)
        g_ffn = ffn_norm_g[i].reshape(1, d)
        q_l, k_l, v_l, xr_l, gr_l = _in_proj_call(h_lat, g_mix, bmods_lat, i, w_in_bf, aw, lw)
        q_c, k_c, v_c, xr_c, gr_c = _in_proj_call(h_ctx, g_mix, bmods_ctx, i, w_in_bf, aw, lw)
        oa_l = _na_call(flat(q_l), flat(k_l), flat(v_l), flat(k_c), flat(v_c), _pad_rpb(na_rpb[i]), batch)
        wg, bg = _lru_gate_weights(lru_wa[i], lru_ba[i], lru_wx[i], lru_bx[i], LANES)
        ol_l, ol_c = _lru_call(xr_l, gr_l, xr_c, gr_c, conv_w[i], conv_b[i], wg, bg, lru_lam[i])
        oa_l = oa_l.reshape(batch, seq, aw)
        j = i // 2
        if i % 2 == 0:
            ffn_w = (ffn_w1[j], ffn_w3[j], ffn_w2[j])
            h_lat = _proj_ffn_call(oa_l, ol_l, h_lat, g_ffn, bmods_lat, i, w_out_bf, *ffn_w)
            if not last:
                oa_c = _ctx_attn_call(flat(q_c), flat(k_c), flat(v_c), batch).reshape(batch, ctx_len, aw)
                h_ctx = _proj_ffn_call(oa_c, ol_c, h_ctx, g_ffn, bmods_ctx, i, w_out_bf, *ffn_w)
        else:
            assert last, "the routed layer is fused with the final norm"
            h_lat = _out_proj_call(oa_l, ol_l, h_lat, bmods_lat, i, w_out_bf)
            out = _moe_layer(flat(h_lat), g_ffn, mods, i, moe_router[j], moe_router_b[j],
                             moe_w1[j], moe_w3[j], moe_w2[j], final_g.reshape(1, d), seq)
    return out.reshape(batch, seq, d)
```

```python
import functools

import jax
import jax.numpy as jnp
from jax import lax
from jax.experimental import pallas as pl
from jax.experimental.pallas import tpu as pltpu

F32 = jnp.float32
BF16 = jnp.bfloat16
I32 = jnp.int32

GRID_W = 64
HEAD_DIM = 64
N_HEADS = 8
NA_KH = 8
NA_KW = 16
LRU_BLOCK = 64
LRU_C = 8.0
CONV_W = 4
N_EXPERTS = 8
EPS = 1e-6
NEG_INF = -1e30
LOG2E = 1.4426950408889634
LANES = 128
SUBLANES = 8
VMEM_BYTES = 64 * 1024 * 1024

FF_CHUNK = 512
ROW_TILE_FFN = 1024
ROW_TILE = 512
GATHER_TILE = 256


def _params(semantics, vmem_mb):
    return pltpu.CompilerParams(dimension_semantics=semantics,
                                vmem_limit_bytes=min(vmem_mb * 1024 * 1024, VMEM_BYTES - 4 * 1024 * 1024))


def _rms_mod(x, g, sh, sc):
    y = x * lax.rsqrt(jnp.mean(x * x, axis=-1, keepdims=True) + EPS)
    return (y * g) * (1.0 + sc) + sh


def _ada_kernel(c_ref, w_ref, b_ref, o_ref):
    c = c_ref[...]
    s = c * jax.nn.sigmoid(c)
    o_ref[...] = jnp.dot(s, w_ref[...], preferred_element_type=F32,
                         precision=lax.Precision.HIGHEST) + b_ref[...]


def _ada_call(c_all, ada_w, ada_b):
    depth, d, n = ada_w.shape
    rows = c_all.shape[0]
    tn = 1024
    return pl.pallas_call(
        _ada_kernel,
        grid=(depth, n // tn),
        in_specs=[pl.BlockSpec((rows, d), lambda l, j: (0, 0)),
                  pl.BlockSpec((None, d, tn), lambda l, j: (l, 0, j)),
                  pl.BlockSpec((None, 1, tn), lambda l, j: (l, 0, j))],
        out_specs=pl.BlockSpec((None, rows, tn), lambda l, j: (l, 0, j)),
        out_shape=jax.ShapeDtypeStruct((depth, rows, n), F32),
        compiler_params=_params(("arbitrary", "arbitrary"), 24),
        name="ada_mod",
    )(c_all, ada_w, ada_b.reshape(depth, 1, n))


TIME_TILE = 128


def _to_time_major(r, o_ref, nb, tt):
    for c in range(o_ref.shape[0]):
        for b in range(nb):
            o_ref[c, pl.ds(b, tt, stride=SUBLANES), :] = r[b * tt:(b + 1) * tt, c * LANES:(c + 1) * LANES]


def _from_time_major(x_ref, nb, tt):
    return jnp.concatenate(
        [jnp.concatenate([x_ref[c, pl.ds(b, tt, stride=SUBLANES), :] for c in range(x_ref.shape[0])], axis=1)
         for b in range(nb)], axis=0)


def _in_proj_kernel(x_ref, g_ref, sh_ref, sc_ref, w_ref, q_ref, k_ref, v_ref, xr_ref, gr_ref, *, aw, lw):
    nb, tt, d = x_ref.shape
    a = _rms_mod(x_ref[...], g_ref[...], sh_ref[...], sc_ref[...]).astype(BF16).reshape(nb * tt, d)
    r = jnp.dot(a, w_ref[...], preferred_element_type=F32)
    q_ref[...] = (r[:, 0:aw] * (LOG2E * HEAD_DIM ** -0.5)).astype(BF16).reshape(nb, tt, aw)
    k_ref[...] = r[:, aw:2 * aw].astype(BF16).reshape(nb, tt, aw)
    v_ref[...] = r[:, 2 * aw:3 * aw].astype(BF16).reshape(nb, tt, aw)
    _to_time_major(r[:, 3 * aw:3 * aw + lw], xr_ref, nb, tt)
    _to_time_major(r[:, 3 * aw + lw:3 * aw + 2 * lw], gr_ref, nb, tt)


def _mod_spec(d, col, layer, rows_per_mod, tm, fixed_row):
    if fixed_row is None:
        assert rows_per_mod % tm == 0, "a row tile must not straddle two batch elements"
        return pl.BlockSpec((None, None, 1, d), lambda i, *_: (layer, (i * tm) // rows_per_mod, 0, col))
    return pl.BlockSpec((None, None, 1, d), lambda i, *_: (layer, fixed_row, 0, col))


def _batch_mod_spec(nb, d, layer, col):
    return pl.BlockSpec((None, nb, 1, d), lambda j: (layer, 0, 0, col))


def _in_proj_call(x, g, bmods, layer, w_bf, aw, lw):
    nb, t, d = x.shape
    assert nb == SUBLANES, "time-major rows put the batch on the sublanes"
    tt = min(TIME_TILE, t)
    ncol = w_bf.shape[1]
    ncb = lw // LANES
    bt = lambda w: pl.BlockSpec((nb, tt, w), lambda j: (0, j, 0))
    tm = pl.BlockSpec((ncb, tt * SUBLANES, LANES), lambda j: (0, j, 0))
    return pl.pallas_call(
        functools.partial(_in_proj_kernel, aw=aw, lw=lw),
        grid=(t // tt,),
        in_specs=[bt(d), pl.BlockSpec((1, d), lambda j: (0, 0)),
                  _batch_mod_spec(nb, d, layer, 0), _batch_mod_spec(nb, d, layer, 1),
                  pl.BlockSpec((d, ncol), lambda j: (0, 0))],
        out_specs=[bt(aw), bt(aw), bt(aw), tm, tm],
        out_shape=[jax.ShapeDtypeStruct((nb, t, aw), BF16)] * 3
        + [jax.ShapeDtypeStruct((ncb, t * SUBLANES, LANES), F32)] * 2,
        compiler_params=_params(("arbitrary",), 56),
        name="in_proj",
    )(x, g, bmods, bmods, w_bf)


def _softmax_pv(parts):
    m = None
    for s, _ in parts:
        mi = jnp.max(s, axis=-1, keepdims=True)
        m = mi if m is None else jnp.maximum(m, mi)
    l = None
    acc = None
    for s, v in parts:
        p = jnp.exp2(s - m)
        li = jnp.sum(p, axis=-1, keepdims=True)
        ai = jnp.dot(p.astype(BF16), v, preferred_element_type=F32)
        l = li if l is None else l + li
        acc = ai if acc is None else acc + ai
    return acc * (1.0 / l)


def _qk(q, k):
    return lax.dot_general(q, k, (((1,), (1,)), ((), ())), preferred_element_type=F32)


def _build_na_bias(rpb_ref, bias_ref):
    lane = lax.broadcasted_iota(I32, (GRID_W, LANES), 1)
    q = lax.broadcasted_iota(I32, (GRID_W, LANES), 0)
    kcol = lane & (GRID_W - 1)
    cs = jnp.clip(q - NA_KW // 2, 0, GRID_W - NA_KW)
    ok = (kcol >= cs) & (kcol < cs + NA_KW)
    low = lane < GRID_W
    for delta in range(NA_KH):
        for h in range(N_HEADS):
            for jp in range(NA_KH // 2):
                halves = []
                for j in (2 * jp, 2 * jp + 1):
                    dr = j - delta + NA_KH - 1
                    w = jnp.broadcast_to(rpb_ref[h, dr:dr + 1, :], (GRID_W, LANES))
                    base = (j % 2) * GRID_W - (NA_KW - 1)
                    halves.append(pltpu.roll(w, base % LANES, 1, stride=1, stride_axis=0))
                t = jnp.where(low, halves[0], halves[1])
                bias_ref[delta, h, :, jp * LANES:(jp + 1) * LANES] = jnp.where(ok, t * LOG2E, NEG_INF)


NA_ROWS_PER_STEP = 8


def _stack_pair(q2):
    first = lax.broadcasted_iota(I32, q2.shape, 1) < HEAD_DIM
    zero = jnp.zeros_like(q2)
    return jnp.concatenate([jnp.where(first, q2, zero), jnp.where(first, zero, q2)], axis=0)


def _unstack_pair(o2):
    nq = o2.shape[0] // 2
    first = lax.broadcasted_iota(I32, (nq, o2.shape[1]), 1) < HEAD_DIM
    return jnp.where(first, o2[:nq], o2[nq:])


def _na_kernel(q_ref, kl_ref, vl_ref, kc_ref, vc_ref, rpb_ref, o_ref, bias_ref, s_ref, p_ref, *, rows, rb):
    @pl.when((pl.program_id(0) == 0) & (pl.program_id(1) == 0))
    def _():
        _build_na_bias(rpb_ref, bias_ref)

    nwin = NA_KH * GRID_W
    npair = N_HEADS // 2
    for rho in range(rb):
        r = pl.program_id(1) * rb + rho
        rs = jnp.clip(r - NA_KH // 2, 0, rows - NA_KH)
        delta = r - rs
        k0 = pl.multiple_of(rs * GRID_W, GRID_W)
        qrows = slice(rho * GRID_W, (rho + 1) * GRID_W)
        for pair in range(npair):
            sl = slice(pair * 2 * HEAD_DIM, (pair + 1) * 2 * HEAD_DIM)
            keys = jnp.concatenate([kl_ref[pl.ds(k0, nwin), sl], kc_ref[:, sl]], axis=0)
            s = _qk(_stack_pair(q_ref[qrows, sl]), keys)
            for sub in range(2):
                h = pair * 2 + sub
                rr = slice(sub * GRID_W, (sub + 1) * GRID_W)
                s_ref[rho, h, :, 0:nwin] = s[rr, 0:nwin] + bias_ref[delta, h]
                s_ref[rho, h, :, nwin:] = s[rr, nwin:]
        s = s_ref[rho]
        p = jnp.exp2(s - jnp.max(s, axis=-1, keepdims=True))
        inv = 1.0 / jnp.sum(p, axis=-1, keepdims=True)
        p_ref[rho] = p.astype(BF16)
        for pair in range(npair):
            sl = slice(pair * 2 * HEAD_DIM, (pair + 1) * 2 * HEAD_DIM)
            vals = jnp.concatenate([vl_ref[pl.ds(k0, nwin), sl], vc_ref[:, sl]], axis=0)
            p2 = p_ref[rho, 2 * pair:2 * pair + 2].reshape(2 * GRID_W, p_ref.shape[-1])
            o2 = jnp.dot(p2, vals, preferred_element_type=F32) * inv[2 * pair:2 * pair + 2].reshape(2 * GRID_W, 1)
            o_ref[qrows, sl] = _unstack_pair(o2).astype(o_ref.dtype)


def _na_call(q, kl, vl, kc, vc, rpb, batch):
    s = q.shape[0] // batch
    l = kc.shape[0] // batch
    w = q.shape[1]
    rows = s // GRID_W
    assert rows >= NA_KH
    rb = NA_ROWS_PER_STEP if rows % NA_ROWS_PER_STEP == 0 else 1
    nkeys = NA_KH * GRID_W + l
    nrb = rows // rb
    return pl.pallas_call(
        functools.partial(_na_kernel, rows=rows, rb=rb),
        grid=(batch, nrb),
        in_specs=[pl.BlockSpec((rb * GRID_W, w), lambda b, r: (b * nrb + r, 0)),
                  pl.BlockSpec((s, w), lambda b, r: (b, 0)),
                  pl.BlockSpec((s, w), lambda b, r: (b, 0)),
                  pl.BlockSpec((l, w), lambda b, r: (b, 0)),
                  pl.BlockSpec((l, w), lambda b, r: (b, 0)),
                  pl.BlockSpec(rpb.shape, lambda b, r: (0, 0, 0))],
        out_specs=pl.BlockSpec((rb * GRID_W, w), lambda b, r: (b * nrb + r, 0)),
        out_shape=jax.ShapeDtypeStruct(q.shape, BF16),
        scratch_shapes=[pltpu.VMEM((NA_KH, N_HEADS, GRID_W, NA_KH * GRID_W), F32),
                        pltpu.VMEM((rb, N_HEADS, GRID_W, nkeys), F32),
                        pltpu.VMEM((rb, N_HEADS, GRID_W, nkeys), BF16)],
        compiler_params=_params(("arbitrary", "arbitrary"), 48),
        name="na_attention",
    )(q, kl, vl, kc, vc, rpb)


def _pad_rpb(rpb):
    h, nr, nc = rpb.shape
    return jnp.zeros((h, 2 * NA_KH, LANES), F32).at[:, :nr, :nc].set(rpb.astype(F32))


def _ctx_attn_kernel(q_ref, k_ref, v_ref, o_ref):
    rows = q_ref.shape[0]
    lane = lax.broadcasted_iota(I32, (rows, 2 * HEAD_DIM), 1)
    first = lane < HEAD_DIM
    for pair in range(N_HEADS // 2):
        sl = slice(pair * 2 * HEAD_DIM, (pair + 1) * 2 * HEAD_DIM)
        q2 = q_ref[:, sl]
        k = k_ref[:, sl]
        v = v_ref[:, sl]
        outs = []
        for sub in range(2):
            qh = jnp.where(first if sub == 0 else ~first, q2, jnp.zeros_like(q2))
            outs.append(_softmax_pv([(_qk(qh, k), v)]))
        o_ref[:, sl] = jnp.where(first, outs[0], outs[1]).astype(o_ref.dtype)


def _ctx_attn_call(q, k, v, batch):
    l = q.shape[0] // batch
    w = q.shape[1]
    spec = pl.BlockSpec((l, w), lambda b: (b, 0))
    return pl.pallas_call(
        _ctx_attn_kernel,
        grid=(batch,),
        in_specs=[spec, spec, spec],
        out_specs=spec,
        out_shape=jax.ShapeDtypeStruct(q.shape, BF16),
        compiler_params=_params(("arbitrary",), 24),
        name="ctx_attention",
    )(q, k, v)


def _softplus(x):
    return jnp.maximum(x, 0.0) + jnp.log1p(jnp.exp(-jnp.abs(x)))


LRU_TIME_CHUNK = 128


def _lru_kernel(xl_hbm, xc_hbm, gl_ref, gc_ref, cw_ref, cb_ref, wg_ref, bg_ref, lam_ref, ol_ref, oc_ref,
                padl_ref, padc_ref, sem, *, tchunk):
    c = pl.program_id(0)
    slot = c % 2
    nb = SUBLANES
    halo_lo, halo_hi = nb, 2 * nb
    zero = jnp.zeros((nb, LANES), F32)

    def fetches(block, slot):
        return [pltpu.make_async_copy(x_hbm.at[block], pad_ref.at[slot, pl.ds(halo_lo, x_hbm.shape[1])], sem.at[slot, k])
                for k, (x_hbm, pad_ref) in enumerate(((xc_hbm, padc_ref), (xl_hbm, padl_ref)))]

    @pl.when(c == 0)
    def _():
        for x_hbm, pad_ref in ((xc_hbm, padc_ref), (xl_hbm, padl_ref)):
            n = x_hbm.shape[1]
            for s in range(2):
                pad_ref[s, 0:halo_lo, :] = zero
                pad_ref[s, halo_lo + n:halo_lo + n + halo_hi, :] = jnp.zeros((halo_hi, LANES), F32)
        for cp in fetches(0, 0):
            cp.start()

    @pl.when(c + 1 < pl.num_programs(0))
    def _():
        for cp in fetches(c + 1, 1 - slot):
            cp.start()

    for cp in fetches(c, slot):
        cp.wait()

    la = [(-0.5 * LRU_C * LOG2E) * _softplus(-lam_ref[d:d + 1, :]) for d in range(2)]
    cb = cb_ref[...]
    cw = [cw_ref[j:j + 1, :] for j in range(CONV_W)]
    rows = tchunk * nb

    def segment(pad_ref, g_ref, o_ref, d, h):
        n = o_ref.shape[0] // rows

        def body(i, h):
            r0 = pl.multiple_of((i if d == 0 else n - 1 - i) * rows, rows)
            u = cb
            for j in range(CONV_W):
                u = u + pad_ref[slot, pl.ds(r0 + j * nb, rows), :] * cw[j]
            ub = u.astype(BF16)
            ta = jnp.tanh(jnp.dot(ub, wg_ref[d, 0], preferred_element_type=F32) + bg_ref[d, 0])
            ti = jnp.tanh(jnp.dot(ub, wg_ref[d, 1], preferred_element_type=F32) + bg_ref[d, 1])
            a = jnp.exp2(la[d] * ta + la[d])
            z = 1.0 - a * a
            b = jnp.where(z > 0.0, z * lax.rsqrt(z), 0.0) * (0.5 * ti + 0.5) * u
            hs = [None] * tchunk
            for t in (range(tchunk) if d == 0 else reversed(range(tchunk))):
                h = a[t * nb:(t + 1) * nb] * h + b[t * nb:(t + 1) * nb]
                hs[t] = h
            hcat = jnp.concatenate(hs, axis=0)
            if d == 0:
                o_ref[pl.ds(r0, rows), :] = hcat
            else:
                y = o_ref[pl.ds(r0, rows), :] + hcat
                o_ref[pl.ds(r0, rows), :] = jax.nn.gelu(g_ref[pl.ds(r0, rows), :]) * y
            return h

        return lax.fori_loop(0, n, body, h)

    for d in range(2):
        h = segment(padc_ref, gc_ref, oc_ref, d, zero)
        segment(padl_ref, gl_ref, ol_ref, d, h)


def _lru_call(xr_l, gr_l, xr_c, gr_c, conv_w, conv_b, wg, bg, lam):
    ncb, rl, _ = xr_l.shape
    rc = xr_c.shape[1]
    tchunk = min(LRU_TIME_CHUNK, rl // SUBLANES, rc // SUBLANES)
    halo = 3 * SUBLANES
    lat = pl.BlockSpec((None, rl, LANES), lambda c: (c, 0, 0))
    ctx = pl.BlockSpec((None, rc, LANES), lambda c: (c, 0, 0))
    per_block = lambda a: jnp.moveaxis(a.reshape(a.shape[:-1] + (ncb, LANES)), -2, 0)
    return pl.pallas_call(
        functools.partial(_lru_kernel, tchunk=tchunk),
        grid=(ncb,),
        in_specs=[pl.BlockSpec(memory_space=pl.ANY), pl.BlockSpec(memory_space=pl.ANY),
                  lat, ctx,
                  pl.BlockSpec((None, CONV_W, LANES), lambda c: (c, 0, 0)),
                  pl.BlockSpec((None, 1, LANES), lambda c: (c, 0, 0)),
                  pl.BlockSpec((2, 2, None, LANES, LANES), lambda c: (0, 0, c, 0, 0)),
                  pl.BlockSpec((None, 2, 2, 1, LANES), lambda c: (c, 0, 0, 0, 0)),
                  pl.BlockSpec((None, 2, LANES), lambda c: (c, 0, 0))],
        out_specs=[lat, ctx],
        out_shape=[jax.ShapeDtypeStruct(xr_l.shape, F32), jax.ShapeDtypeStruct(xr_c.shape, F32)],
        scratch_shapes=[pltpu.VMEM((2, rl + halo, LANES), F32), pltpu.VMEM((2, rc + halo, LANES), F32),
                        pltpu.SemaphoreType.DMA((2, 2))],
        compiler_params=_params(("arbitrary",), 58),
        name="rglru",
    )(xr_l, xr_c, gr_l, gr_c, per_block(conv_w), per_block(conv_b.reshape(1, -1)), wg,
      per_block(bg[:, :, None, :]), per_block(lam))


def _lru_gate_weights(wa, ba, wx, bx, half):
    nblk = wa.shape[1]
    lw = nblk * LRU_BLOCK
    per = half // LRU_BLOCK

    def dense(w):
        w = w.reshape(2, nblk // per, per, LRU_BLOCK, LRU_BLOCK)
        eye = jnp.eye(per, dtype=w.dtype)
        full = w[:, :, :, :, None, :] * eye[None, None, :, None, :, None]
        return full.reshape(2, nblk // per, half, half)

    wg = (0.5 * jnp.stack([dense(wa), dense(wx)], axis=1)).astype(BF16)
    bg = 0.5 * jnp.stack([ba, bx], axis=1).astype(F32)
    return wg, bg


def _out_proj_kernel(oa_ref, ol_ref, h_ref, g1_ref, w_ref, o_ref, *, aw):
    o_ref[...] = _mix_proj(oa_ref, ol_ref, h_ref, g1_ref, w_ref, aw)


def _out_proj_call(oa, ol, h, bmods, layer, w_bf):
    nb, t, d = h.shape
    aw = oa.shape[2]
    ncb = ol.shape[0]
    tt = min(TIME_TILE, t)
    bt = lambda w: pl.BlockSpec((nb, tt, w), lambda j: (0, j, 0))
    return pl.pallas_call(
        functools.partial(_out_proj_kernel, aw=aw),
        grid=(t // tt,),
        in_specs=[bt(aw), pl.BlockSpec((ncb, tt * SUBLANES, LANES), lambda j: (0, j, 0)), bt(d),
                  _batch_mod_spec(nb, d, layer, 2), pl.BlockSpec(w_bf.shape, lambda j: (0, 0))],
        out_specs=bt(d),
        out_shape=jax.ShapeDtypeStruct((nb, t, d), F32),
        compiler_params=_params(("arbitrary",), 40),
        name="out_proj",
    )(oa, ol, h, bmods, w_bf)


def _swiglu_step(a_bf, w1_ref, w3_ref, w2_ref, acc_ref):
    g = jnp.dot(a_bf, w1_ref[...].astype(BF16), preferred_element_type=F32)
    u = jnp.dot(a_bf, w3_ref[...].astype(BF16), preferred_element_type=F32)
    hmid = (g * jax.nn.sigmoid(g) * u).astype(BF16)
    acc_ref[...] += jnp.dot(hmid, w2_ref[...].astype(BF16), preferred_element_type=F32)


def _swiglu_streamed(a_ref, acc_ref, w_cur, w_next, first, has_next, tile, bufs, sem):
    w1b, w3b, w2b = bufs
    fc = w1b.shape[2]
    nc = w_cur[0].shape[1] // fc

    def copies(w, c, slot):
        return (pltpu.make_async_copy(w[0].at[:, pl.ds(c * fc, fc)], w1b.at[slot], sem.at[0, slot]),
                pltpu.make_async_copy(w[1].at[:, pl.ds(c * fc, fc)], w3b.at[slot], sem.at[1, slot]),
                pltpu.make_async_copy(w[2].at[pl.ds(c * fc, fc), :], w2b.at[slot], sem.at[2, slot]))

    @pl.when(first)
    def _():
        for cp in copies(w_cur, 0, (tile * nc) % 2):
            cp.start()

    acc_ref[...] = jnp.zeros_like(acc_ref)
    for c in range(nc):
        slot = (tile * nc + c) % 2
        if c + 1 < nc:
            for cp in copies(w_cur, c + 1, 1 - slot):
                cp.start()
        else:
            @pl.when(has_next)
            def _():
                for cp in copies(w_next, 0, 1 - slot):
                    cp.start()
        for cp in copies(w_cur, c, slot):
            cp.wait()
        _swiglu_step(a_ref[...], w1b.at[slot], w3b.at[slot], w2b.at[slot], acc_ref)


def _swiglu_buffers(d, dtype):
    return [pltpu.VMEM((2, d, FF_CHUNK), dtype), pltpu.VMEM((2, d, FF_CHUNK), dtype),
            pltpu.VMEM((2, FF_CHUNK, d), dtype), pltpu.SemaphoreType.DMA((3, 2))]


def _mix_proj(oa_ref, ol_ref, h_ref, g1_ref, wo_ref, aw):
    nb, tt, d = h_ref.shape
    o = jnp.dot(oa_ref[...].reshape(nb * tt, aw), wo_ref[0:aw, :], preferred_element_type=F32)
    ol = _from_time_major(ol_ref, nb, tt).astype(BF16)
    o = o + jnp.dot(ol, wo_ref[aw:, :], preferred_element_type=F32)
    return h_ref[...] + g1_ref[...] * o.reshape(nb, tt, d)


def _proj_ffn_kernel(oa_ref, ol_ref, h_ref, g1_ref, wo_ref, g_ref, sh_ref, sc_ref, g2_ref,
                     w1_hbm, w3_hbm, w2_hbm, o_ref, a_ref, acc_ref, w1b, w3b, w2b, sem, *, aw):
    j = pl.program_id(0)
    nb, tt, d = h_ref.shape
    h1 = _mix_proj(oa_ref, ol_ref, h_ref, g1_ref, wo_ref, aw)
    o_ref[...] = h1
    a_ref[...] = _rms_mod(h1, g_ref[...], sh_ref[...], sc_ref[...]).astype(BF16).reshape(nb * tt, d)
    w = (w1_hbm, w3_hbm, w2_hbm)
    _swiglu_streamed(a_ref, acc_ref, w, w, j == 0, j + 1 < pl.num_programs(0), j, (w1b, w3b, w2b), sem)
    o_ref[...] += g2_ref[...] * acc_ref[...].reshape(nb, tt, d)


def _proj_ffn_call(oa, ol, h, g, bmods, layer, wo_bf, w1, w3, w2):
    nb, t, d = h.shape
    aw = oa.shape[2]
    ncb = ol.shape[0]
    dff = w1.shape[1]
    tt = min(ROW_TILE_FFN // nb, t)
    assert dff % FF_CHUNK == 0
    bt = lambda w: pl.BlockSpec((nb, tt, w), lambda j: (0, j, 0))
    bm = lambda col: pl.BlockSpec((None, nb, 1, d), lambda j: (layer, 0, 0, col))
    hbm = pl.BlockSpec(memory_space=pl.ANY)
    return pl.pallas_call(
        functools.partial(_proj_ffn_kernel, aw=aw),
        grid=(t // tt,),
        in_specs=[bt(aw), pl.BlockSpec((ncb, tt * SUBLANES, LANES), lambda j: (0, j, 0)), bt(d), bm(2),
                  pl.BlockSpec(wo_bf.shape, lambda j: (0, 0), pipeline_mode=pl.Buffered(1)),
                  pl.BlockSpec((1, d), lambda j: (0, 0)), bm(3), bm(4), bm(5), hbm, hbm, hbm],
        out_specs=bt(d),
        out_shape=jax.ShapeDtypeStruct((nb, t, d), F32),
        scratch_shapes=[pltpu.VMEM((nb * tt, d), BF16), pltpu.VMEM((nb * tt, d), F32)]
        + _swiglu_buffers(d, w1.dtype),
        compiler_params=_params(("arbitrary",), 58),
        name="proj_ffn",
    )(oa, ol, h, bmods, wo_bf, g, bmods, bmods, bmods, w1, w3, w2)


def _to_row_tiles(x, o_ref):
    t, d = x.shape
    ns = d // LANES
    for s in range(ns):
        o_ref[pl.ds(s, t, stride=ns), :] = x[:, s * LANES:(s + 1) * LANES]


def _from_row_tiles(x_ref, idx, t, ns):
    return jnp.concatenate([x_ref[idx + (pl.ds(s, t, stride=ns), slice(None))] for s in range(ns)], axis=1)


def _gffn_kernel(te_ref, nu_ref, x_ref, w1_hbm, w3_hbm, w2_hbm, o_ref, a_ref, acc_ref, w1b, w3b, w2b, sem):
    i = pl.program_id(0)
    tm, d = a_ref.shape
    nu = nu_ref[0]

    @pl.when(i < nu)
    def _():
        a_ref[...] = _from_row_tiles(x_ref, (), tm, d // LANES).astype(BF16)
        e = te_ref[i]
        e_next = te_ref[jnp.minimum(i + 1, pl.num_programs(0) - 1)]
        _swiglu_streamed(a_ref, acc_ref, (w1_hbm.at[e], w3_hbm.at[e], w2_hbm.at[e]),
                         (w1_hbm.at[e_next], w3_hbm.at[e_next], w2_hbm.at[e_next]),
                         i == 0, i + 1 < nu, i, (w1b, w3b, w2b), sem)
        _to_row_tiles(acc_ref[...], o_ref)

    @pl.when(i >= nu)
    def _():
        o_ref[...] = jnp.zeros_like(o_ref)


def _gffn_call(tile_expert, n_used, xg, w1, w3, w2):
    d = w1.shape[1]
    ns = d // LANES
    rg = xg.shape[0] // ns
    dff = w1.shape[2]
    tm = ROW_TILE_FFN
    assert dff % FF_CHUNK == 0

    def tile(i, nu):
        return jnp.maximum(jnp.minimum(i, nu[0] - 1), 0)

    hbm = pl.BlockSpec(memory_space=pl.ANY)
    grid_spec = pltpu.PrefetchScalarGridSpec(
        num_scalar_prefetch=2,
        grid=(rg // tm,),
        in_specs=[pl.BlockSpec((tm * ns, LANES), lambda i, te, nu: (tile(i, nu), 0)), hbm, hbm, hbm],
        out_specs=pl.BlockSpec((tm * ns, LANES), lambda i, te, nu: (i, 0)),
        scratch_shapes=[pltpu.VMEM((tm, d), BF16), pltpu.VMEM((tm, d), F32)] + _swiglu_buffers(d, w1.dtype))
    return pl.pallas_call(
        _gffn_kernel,
        grid_spec=grid_spec,
        out_shape=jax.ShapeDtypeStruct((rg * ns, LANES), F32),
        compiler_params=_params(("arbitrary",), 56),
        name="ffn_grouped",
    )(tile_expert, n_used, xg, w1, w3, w2)


def _route_kernel(x_ref, g_ref, sh_ref, sc_ref, wr_ref, br_ref, a_ref, tab_ref, gcol_ref, cnt_ref, carry_ref):
    i = pl.program_id(0)
    tm = x_ref.shape[0]

    @pl.when(i == 0)
    def _():
        carry_ref[...] = jnp.zeros_like(carry_ref)

    a = _rms_mod(x_ref[...], g_ref[...], sh_ref[...], sc_ref[...])
    _to_row_tiles(a, a_ref)
    lane = lax.broadcasted_iota(I32, (tm, LANES), 1).astype(F32)
    logits = jnp.dot(a, wr_ref[...], preferred_element_type=F32, precision=lax.Precision.HIGHEST) + br_ref[...]
    logits = jnp.where(lane < N_EXPERTS, logits, NEG_INF)
    m1 = jnp.max(logits, axis=-1, keepdims=True)
    i1 = jnp.min(jnp.where(logits == m1, lane, float(LANES)), axis=-1, keepdims=True)
    rest = jnp.where(lane == i1, 2.0 * NEG_INF, logits)
    m2 = jnp.max(rest, axis=-1, keepdims=True)
    i2 = jnp.min(jnp.where(rest == m2, lane, float(LANES)), axis=-1, keepdims=True)
    e = jnp.exp(m2 - m1)
    w1 = 1.0 / (1.0 + e)
    w2 = e * w1
    sel1 = lane == i1
    sel2 = lane == i2
    onehot = jnp.where(sel1 | sel2, 1.0, 0.0)
    row = lax.broadcasted_iota(I32, (tm, tm), 0)
    col = lax.broadcasted_iota(I32, (tm, tm), 1)
    before = jnp.where(col < row, 1.0, 0.0).astype(BF16)
    cum = jnp.dot(before, onehot.astype(BF16), preferred_element_type=F32) + carry_ref[...]
    r1 = jnp.sum(jnp.where(sel1, cum, 0.0), axis=-1, keepdims=True)
    r2 = jnp.sum(jnp.where(sel2, cum, 0.0), axis=-1, keepdims=True)
    carry_ref[...] += jnp.sum(onehot, axis=0, keepdims=True)
    cnt_ref[...] = carry_ref[...]
    cols = (jnp.where(lane == 0, i1, 0.0) + jnp.where(lane == 1, i2, 0.0)
            + jnp.where(lane == 2, r1, 0.0) + jnp.where(lane == 3, r2, 0.0))
    tab_ref[...] = cols.T[0:SUBLANES, :]
    gcol_ref[...] = (jnp.where(lane == 0, w1, 0.0) + jnp.where(lane == 1, w2, 0.0))[:, 0:SUBLANES]


def _route_call(h, g, mods, layer, wr, br, rows_per_mod):
    n, d = h.shape
    tm = min(ROW_TILE, n)
    ms = functools.partial(_mod_spec, d, layer=layer, rows_per_mod=rows_per_mod, tm=tm, fixed_row=None)
    return pl.pallas_call(
        _route_kernel,
        grid=(n // tm,),
        in_specs=[pl.BlockSpec((tm, d), lambda i: (i, 0)),
                  pl.BlockSpec((1, d), lambda i: (0, 0)), ms(col=3), ms(col=4),
                  pl.BlockSpec((d, LANES), lambda i: (0, 0)),
                  pl.BlockSpec((1, LANES), lambda i: (0, 0))],
        out_specs=[pl.BlockSpec((tm * (d // LANES), LANES), lambda i: (i, 0)),
                   pl.BlockSpec((SUBLANES, tm), lambda i: (0, i)),
                   pl.BlockSpec((tm, SUBLANES), lambda i: (i, 0)),
                   pl.BlockSpec((1, LANES), lambda i: (0, 0))],
        out_shape=[jax.ShapeDtypeStruct((n * (d // LANES), LANES), F32),
                   jax.ShapeDtypeStruct((SUBLANES, n), F32),
                   jax.ShapeDtypeStruct((n, SUBLANES), F32),
                   jax.ShapeDtypeStruct((1, LANES), F32)],
        scratch_shapes=[pltpu.VMEM((1, LANES), F32)],
        compiler_params=_params(("arbitrary",), 40),
        name="moe_route",
    )(h, g, mods, mods, wr, br)


DMA_UNROLL = 4


FILL_CHUNK = 64


def _dispatch_kernel(off_ref, fill_ref, tab_ref, a_ref, xg_out, pos_ref, sem, *, ns):
    tm = tab_ref.shape[1]

    def rows_copy(src, dst, nrows):
        return pltpu.make_async_copy(a_ref.at[pl.ds(pl.multiple_of(src * ns, ns), nrows * ns)],
                                     xg_out.at[pl.ds(pl.multiple_of(dst * ns, ns), nrows * ns)], sem)

    def row_copy(src, dst):
        return rows_copy(src, dst, 1)

    @pl.when(pl.program_id(0) == pl.num_programs(0) - 1)
    def _():
        nfill = fill_ref.shape[0] // 2
        for e in range(nfill):
            start = fill_ref[e]
            length = fill_ref[nfill + e]
            nbig = length // FILL_CHUNK
            nsmall = length - nbig * FILL_CHUNK
            small0 = start + nbig * FILL_CHUNK

            def big(j, carry, start=start):
                rows_copy(0, start + j * FILL_CHUNK, FILL_CHUNK).start()
                return carry

            def small(j, carry, small0=small0):
                row_copy(0, small0 + j).start()
                return carry

            lax.fori_loop(0, nbig, big, 0)
            lax.fori_loop(0, nsmall, small, 0)
            lax.fori_loop(0, nbig, lambda j, c: (rows_copy(0, 0, FILL_CHUNK).wait(), c)[1], 0)
            lax.fori_loop(0, nsmall, lambda j, c: (row_copy(0, 0).wait(), c)[1], 0)

    def issue(j, carry):
        for u in range(DMA_UNROLL):
            t = j * DMA_UNROLL + u
            for k in range(2):
                p = off_ref[tab_ref[k, t]] + tab_ref[2 + k, t]
                pos_ref[k, t] = p
                row_copy(t, p).start()
        return carry

    lax.fori_loop(0, tm // DMA_UNROLL, issue, 0)

    def drain(j, carry):
        for _ in range(2 * DMA_UNROLL):
            row_copy(0, 0).wait()
        return carry

    lax.fori_loop(0, tm // DMA_UNROLL, drain, 0)


def _dispatch_call(off, fill, tab_i, a, rows_g, ns):
    n = a.shape[0] // ns
    tm = min(ROW_TILE, n)
    assert tm >= FILL_CHUNK
    grid_spec = pltpu.PrefetchScalarGridSpec(
        num_scalar_prefetch=2,
        grid=(n // tm,),
        in_specs=[pl.BlockSpec((4, tm), lambda i, off, fill: (0, i), memory_space=pltpu.SMEM),
                  pl.BlockSpec((tm * ns, LANES), lambda i, off, fill: (i, 0))],
        out_specs=[pl.BlockSpec(memory_space=pl.ANY),
                   pl.BlockSpec((2, tm), lambda i, off, fill: (0, i), memory_space=pltpu.SMEM)],
        scratch_shapes=[pltpu.SemaphoreType.DMA(())])
    return pl.pallas_call(
        functools.partial(_dispatch_kernel, ns=ns),
        grid_spec=grid_spec,
        out_shape=[jax.ShapeDtypeStruct((rows_g * ns, LANES), a.dtype), jax.ShapeDtypeStruct((2, n), I32)],
        compiler_params=_params(("arbitrary",), 16),
        name="moe_dispatch",
    )(off, fill, tab_i, a)


def _combine_kernel(pos_ref, posn_ref, y_hbm, h_ref, gcol_ref, g2_ref, fg_ref, o_ref, buf_ref, sem):
    i = pl.program_id(0)
    n = pl.num_programs(0)
    tm, d = h_ref.shape
    ns = d // LANES

    def row_copy(src, slot, k, t):
        return pltpu.make_async_copy(y_hbm.at[pl.ds(pl.multiple_of(src * ns, ns), ns)],
                                     buf_ref.at[slot, k, pl.ds(pl.multiple_of(t * ns, ns), ns)], sem.at[slot])

    def issue(p_ref, slot):
        def body(j, carry):
            for u in range(DMA_UNROLL):
                t = j * DMA_UNROLL + u
                for k in range(2):
                    row_copy(p_ref[k, t], slot, k, t).start()
            return carry
        lax.fori_loop(0, tm // DMA_UNROLL, body, 0)

    @pl.when(i == 0)
    def _():
        issue(pos_ref, 0)

    @pl.when(i + 1 < n)
    def _():
        issue(posn_ref, (i + 1) % 2)

    slot = i % 2

    def drain(j, carry):
        for _ in range(2 * DMA_UNROLL):
            row_copy(0, slot, 0, 0).wait()
        return carry

    lax.fori_loop(0, tm // DMA_UNROLL, drain, 0)
    y = (gcol_ref[:, 0:1] * _from_row_tiles(buf_ref, (slot, 0), tm, ns)
         + gcol_ref[:, 1:2] * _from_row_tiles(buf_ref, (slot, 1), tm, ns))
    hn = h_ref[...] + g2_ref[...] * y
    o_ref[...] = hn * lax.rsqrt(jnp.mean(hn * hn, axis=-1, keepdims=True) + EPS) * fg_ref[...]


def _combine_call(pos, y, h, gcol, mods, layer, final_g, rows_per_mod):
    n, d = h.shape
    tm = min(GATHER_TILE, n)
    nt = n // tm
    ms = functools.partial(_mod_spec, d, layer=layer, rows_per_mod=rows_per_mod, tm=tm, fixed_row=None)
    return pl.pallas_call(
        _combine_kernel,
        grid=(nt,),
        in_specs=[pl.BlockSpec((2, tm), lambda i: (0, i), memory_space=pltpu.SMEM),
                  pl.BlockSpec((2, tm), lambda i: (0, jnp.minimum(i + 1, nt - 1)), memory_space=pltpu.SMEM),
                  pl.BlockSpec(memory_space=pl.ANY),
                  pl.BlockSpec((tm, d), lambda i: (i, 0)),
                  pl.BlockSpec((tm, SUBLANES), lambda i: (i, 0)),
                  ms(col=5),
                  pl.BlockSpec((1, d), lambda i: (0, 0))],
        out_specs=pl.BlockSpec((tm, d), lambda i: (i, 0)),
        out_shape=jax.ShapeDtypeStruct((n, d), F32),
        scratch_shapes=[pltpu.VMEM((2, 2, tm * (d // LANES), LANES), F32), pltpu.SemaphoreType.DMA((2,))],
        compiler_params=_params(("arbitrary",), 24),
        name="moe_combine",
    )(pos, pos, y, h, gcol, mods, final_g)


def _moe_layer(h, g, mods, layer, router, router_b, w1, w3, w2, final_g, rows_per_mod):
    n, d = h.shape
    wr = jnp.zeros((d, LANES), F32).at[:, :N_EXPERTS].set(router)
    br = jnp.zeros((1, LANES), F32).at[0, :N_EXPERTS].set(router_b)
    a, tab, gcol, cnt = _route_call(h, g, mods, layer, wr, br, rows_per_mod)
    tm = ROW_TILE_FFN
    n_tiles = (2 * n) // tm + N_EXPERTS
    counts = cnt[0, :N_EXPERTS].astype(I32)
    tiles = (counts + tm - 1) // tm
    ends = jnp.cumsum(tiles)
    off = ((ends - tiles) * tm).astype(I32)
    tile_expert = jnp.minimum(jnp.sum(jnp.arange(n_tiles, dtype=I32)[:, None] >= ends[None, :], axis=1),
                              N_EXPERTS - 1).astype(I32)
    n_used = ends[-1:].astype(I32)
    assert d // LANES == SUBLANES, "one token must fill one (8, 128) tile of the row-tile layout"
    used = ends[-1:] * tm
    fill = jnp.concatenate([off + counts, used, tiles * tm - counts, n_tiles * tm - used]).astype(I32)
    xg, pos = _dispatch_call(off, fill, tab[0:4].astype(I32), a, n_tiles * tm, d // LANES)
    y = _gffn_call(tile_expert, n_used, xg, w1, w3, w2)
    return _combine_call(pos, y, h, gcol, mods, layer, final_g, rows_per_mod)


def kernel(x, c, ctx, c_ctx, ada_w, ada_b, mix_norm_g, ffn_norm_g, w_in, w_out, na_rpb, conv_w, conv_b, lru_wa, lru_ba, lru_wx, lru_bx, lru_lam, ffn_w1, ffn_w3, ffn_w2, moe_router, moe_router_b, moe_w1, moe_w3, moe_w2, final_g):
    batch, seq, d = x.shape
    ctx_len = ctx.shape[1]
    depth = ada_w.shape[0]
    lw = conv_w.shape[-1]
    aw = (w_in.shape[-1] - 2 * lw) // 3
    ctx_row = batch

    mod_rows = -(-(batch + 1) // SUBLANES) * SUBLANES
    c_all = jnp.zeros((mod_rows, d), F32).at[:batch].set(c).at[ctx_row].set(c_ctx)
    mods = _ada_call(c_all, ada_w, ada_b).reshape(depth, mod_rows, 1, 6 * d)

    bmods_lat = mods[:, :batch]
    bmods_ctx = jnp.broadcast_to(mods[:, ctx_row:ctx_row + 1], bmods_lat.shape)
    flat = lambda a: a.reshape(a.shape[0] * a.shape[1], a.shape[2])
    h_lat, h_ctx = x, ctx
    out = None
    for i in range(depth):
        last = i == depth - 1
        w_in_bf = w_in[i].astype(BF16)
        w_out_bf = w_out[i].astype(BF16)
        g_mix = mix_norm_g[i].reshape(1, d)
        g_ffn = ffn_norm_g[i].reshape(1, d)
        q_l, k_l, v_l, xr_l, gr_l = _in_proj_call(h_lat, g_mix, bmods_lat, i, w_in_bf, aw, lw)
        q_c, k_c, v_c, xr_c, gr_c = _in_proj_call(h_ctx, g_mix, bmods_ctx, i, w_in_bf, aw, lw)
        oa_l = _na_call(flat(q_l), flat(k_l), flat(v_l), flat(k_c), flat(v_c), _pad_rpb(na_rpb[i]), batch)
        wg, bg = _lru_gate_weights(lru_wa[i], lru_ba[i], lru_wx[i], lru_bx[i], LANES)
        ol_l, ol_c = _lru_call(xr_l, gr_l, xr_c, gr_c, conv_w[i], conv_b[i], wg, bg, lru_lam[i])
        oa_l = oa_l.reshape(batch, seq, aw)
        j = i // 2
        if i % 2 == 0:
            ffn_w = (ffn_w1[j], ffn_w3[j], ffn_w2[j])
            h_lat = _proj_ffn_call(oa_l, ol_l, h_lat, g_ffn, bmods_lat, i, w_out_bf, *ffn_w)
            if not last:
                oa_c = _ctx_attn_call(flat(q_c), flat(k_c), flat(v_c), batch).reshape(batch, ctx_len, aw)
                h_ctx = _proj_ffn_call(oa_c, ol_c, h_ctx, g_ffn, bmods_ctx, i, w_out_bf, *ffn_w)
        else:
            assert last, "the routed layer is fused with the final norm"
            h_lat = _out_proj_call(oa_l, ol_l, h_lat, bmods_lat, i, w_out_bf)
            out = _moe_layer(flat(h_lat), g_ffn, mods, i, moe_router[j], moe_router_b[j],
                             moe_w1[j], moe_w3[j], moe_w2[j], final_g.reshape(1, d), seq)
    return out.reshape(batch, seq, d)
```

```python
import functools

import jax
import jax.numpy as jnp
from jax import lax
from jax.experimental import pallas as pl
from jax.experimental.pallas import tpu as pltpu

F32 = jnp.float32
BF16 = jnp.bfloat16
I32 = jnp.int32

GRID_W = 64
HEAD_DIM = 64
N_HEADS = 8
NA_KH = 8
NA_KW = 16
LRU_BLOCK = 64
LRU_C = 8.0
CONV_W = 4
N_EXPERTS = 8
EPS = 1e-6
NEG_INF = -1e30
LOG2E = 1.4426950408889634
LANES = 128
SUBLANES = 8
VMEM_BYTES = 64 * 1024 * 1024

FF_CHUNK = 512
ROW_TILE_FFN = 1024
ROW_TILE = 512
GATHER_TILE = 256


def _params(semantics, vmem_mb):
    return pltpu.CompilerParams(dimension_semantics=semantics,
                                vmem_limit_bytes=min(vmem_mb * 1024 * 1024, VMEM_BYTES - 4 * 1024 * 1024))


def _rms_mod(x, g, sh, sc):
    y = x * lax.rsqrt(jnp.mean(x * x, axis=-1, keepdims=True) + EPS)
    return (y * g) * (1.0 + sc) + sh


def _ada_kernel(c_ref, w_ref, b_ref, o_ref):
    c = c_ref[...]
    s = c * jax.nn.sigmoid(c)
    o_ref[...] = jnp.dot(s, w_ref[...], preferred_element_type=F32,
                         precision=lax.Precision.HIGHEST) + b_ref[...]


def _ada_call(c_all, ada_w, ada_b):
    depth, d, n = ada_w.shape
    rows = c_all.shape[0]
    tn = 1024
    return pl.pallas_call(
        _ada_kernel,
        grid=(depth, n // tn),
        in_specs=[pl.BlockSpec((rows, d), lambda l, j: (0, 0)),
                  pl.BlockSpec((None, d, tn), lambda l, j: (l, 0, j)),
                  pl.BlockSpec((None, 1, tn), lambda l, j: (l, 0, j))],
        out_specs=pl.BlockSpec((None, rows, tn), lambda l, j: (l, 0, j)),
        out_shape=jax.ShapeDtypeStruct((depth, rows, n), F32),
        compiler_params=_params(("arbitrary", "arbitrary"), 24),
        name="ada_mod",
    )(c_all, ada_w, ada_b.reshape(depth, 1, n))


TIME_TILE = 128


def _to_time_major(r, o_ref, nb, tt):
    for c in range(o_ref.shape[0]):
        for b in range(nb):
            o_ref[c, pl.ds(b, tt, stride=SUBLANES), :] = r[b * tt:(b + 1) * tt, c * LANES:(c + 1) * LANES]


def _from_time_major(x_ref, nb, tt):
    return jnp.concatenate(
        [jnp.concatenate([x_ref[c, pl.ds(b, tt, stride=SUBLANES), :] for c in range(x_ref.shape[0])], axis=1)
         for b in range(nb)], axis=0)


def _in_proj_kernel(x_ref, g_ref, sh_ref, sc_ref, w_ref, q_ref, k_ref, v_ref, xr_ref, gr_ref, *, aw, lw):
    nb, tt, d = x_ref.shape
    a = _rms_mod(x_ref[...], g_ref[...], sh_ref[...], sc_ref[...]).astype(BF16).reshape(nb * tt, d)
    r = jnp.dot(a, w_ref[...], preferred_element_type=F32)
    q_ref[...] = (r[:, 0:aw] * (LOG2E * HEAD_DIM ** -0.5)).astype(BF16).reshape(nb, tt, aw)
    k_ref[...] = r[:, aw:2 * aw].astype(BF16).reshape(nb, tt, aw)
    v_ref[...] = r[:, 2 * aw:3 * aw].astype(BF16).reshape(nb, tt, aw)
    _to_time_major(r[:, 3 * aw:3 * aw + lw], xr_ref, nb, tt)
    _to_time_major(r[:, 3 * aw + lw:3 * aw + 2 * lw], gr_ref, nb, tt)


def _mod_spec(d, col, layer, rows_per_mod, tm, fixed_row):
    if fixed_row is None:
        assert rows_per_mod % tm == 0, "a row tile must not straddle two batch elements"
        return pl.BlockSpec((None, None, 1, d), lambda i, *_: (layer, (i * tm) // rows_per_mod, 0, col))
    return pl.BlockSpec((None, None, 1, d), lambda i, *_: (layer, fixed_row, 0, col))


def _batch_mod_spec(nb, d, layer, col):
    return pl.BlockSpec((None, nb, 1, d), lambda j: (layer, 0, 0, col))


def _in_proj_call(x, g, bmods, layer, w_bf, aw, lw):
    nb, t, d = x.shape
    assert nb == SUBLANES, "time-major rows put the batch on the sublanes"
    tt = min(TIME_TILE, t)
    ncol = w_bf.shape[1]
    ncb = lw // LANES
    bt = lambda w: pl.BlockSpec((nb, tt, w), lambda j: (0, j, 0))
    tm = pl.BlockSpec((ncb, tt * SUBLANES, LANES), lambda j: (0, j, 0))
    return pl.pallas_call(
        functools.partial(_in_proj_kernel, aw=aw, lw=lw),
        grid=(t // tt,),
        in_specs=[bt(d), pl.BlockSpec((1, d), lambda j: (0, 0)),
                  _batch_mod_spec(nb, d, layer, 0), _batch_mod_spec(nb, d, layer, 1),
                  pl.BlockSpec((d, ncol), lambda j: (0, 0))],
        out_specs=[bt(aw), bt(aw), bt(aw), tm, tm],
        out_shape=[jax.ShapeDtypeStruct((nb, t, aw), BF16)] * 3
        + [jax.ShapeDtypeStruct((ncb, t * SUBLANES, LANES), F32)] * 2,
        compiler_params=_params(("arbitrary",), 56),
        name="in_proj",
    )(x, g, bmods, bmods, w_bf)


def _softmax_pv(parts):
    m = None
    for s, _ in parts:
        mi = jnp.max(s, axis=-1, keepdims=True)
        m = mi if m is None else jnp.maximum(m, mi)
    l = None
    acc = None
    for s, v in parts:
        p = jnp.exp2(s - m)
        li = jnp.sum(p, axis=-1, keepdims=True)
        ai = jnp.dot(p.astype(BF16), v, preferred_element_type=F32)
        l = li if l is None else l + li
        acc = ai if acc is None else acc + ai
    return acc * (1.0 / l)


def _qk(q, k):
    return lax.dot_general(q, k, (((1,), (1,)), ((), ())), preferred_element_type=F32)


def _build_na_bias(rpb_ref, bias_ref):
    lane = lax.broadcasted_iota(I32, (GRID_W, LANES), 1)
    q = lax.broadcasted_iota(I32, (GRID_W, LANES), 0)
    kcol = lane & (GRID_W - 1)
    cs = jnp.clip(q - NA_KW // 2, 0, GRID_W - NA_KW)
    ok = (kcol >= cs) & (kcol < cs + NA_KW)
    low = lane < GRID_W
    for delta in range(NA_KH):
        for h in range(N_HEADS):
            for jp in range(NA_KH // 2):
                halves = []
                for j in (2 * jp, 2 * jp + 1):
                    dr = j - delta + NA_KH - 1
                    w = jnp.broadcast_to(rpb_ref[h, dr:dr + 1, :], (GRID_W, LANES))
                    base = (j % 2) * GRID_W - (NA_KW - 1)
                    halves.append(pltpu.roll(w, base % LANES, 1, stride=1, stride_axis=0))
                t = jnp.where(low, halves[0], halves[1])
                bias_ref[delta, h, :, jp * LANES:(jp + 1) * LANES] = jnp.where(ok, t * LOG2E, NEG_INF)


NA_ROWS_PER_STEP = 8


def _stack_pair(q2):
    first = lax.broadcasted_iota(I32, q2.shape, 1) < HEAD_DIM
    zero = jnp.zeros_like(q2)
    return jnp.concatenate([jnp.where(first, q2, zero), jnp.where(first, zero, q2)], axis=0)


def _unstack_pair(o2):
    nq = o2.shape[0] // 2
    first = lax.broadcasted_iota(I32, (nq, o2.shape[1]), 1) < HEAD_DIM
    return jnp.where(first, o2[:nq], o2[nq:])


def _na_kernel(q_ref, kl_ref, vl_ref, kc_ref, vc_ref, rpb_ref, o_ref, bias_ref, s_ref, p_ref, *, rows, rb):
    @pl.when((pl.program_id(0) == 0) & (pl.program_id(1) == 0))
    def _():
        _build_na_bias(rpb_ref, bias_ref)

    nwin = NA_KH * GRID_W
    npair = N_HEADS // 2
    for rho in range(rb):
        r = pl.program_id(1) * rb + rho
        rs = jnp.clip(r - NA_KH // 2, 0, rows - NA_KH)
        delta = r - rs
        k0 = pl.multiple_of(rs * GRID_W, GRID_W)
        qrows = slice(rho * GRID_W, (rho + 1) * GRID_W)
        for pair in range(npair):
            sl = slice(pair * 2 * HEAD_DIM, (pair + 1) * 2 * HEAD_DIM)
            keys = jnp.concatenate([kl_ref[pl.ds(k0, nwin), sl], kc_ref[:, sl]], axis=0)
            s = _qk(_stack_pair(q_ref[qrows, sl]), keys)
            for sub in range(2):
                h = pair * 2 + sub
                rr = slice(sub * GRID_W, (sub + 1) * GRID_W)
                s_ref[rho, h, :, 0:nwin] = s[rr, 0:nwin] + bias_ref[delta, h]
                s_ref[rho, h, :, nwin:] = s[rr, nwin:]
        s = s_ref[rho]
        p = jnp.exp2(s - jnp.max(s, axis=-1, keepdims=True))
        inv = 1.0 / jnp.sum(p, axis=-1, keepdims=True)
        p_ref[rho] = p.astype(BF16)
        for pair in range(npair):
            sl = slice(pair * 2 * HEAD_DIM, (pair + 1) * 2 * HEAD_DIM)
            vals = jnp.concatenate([vl_ref[pl.ds(k0, nwin), sl], vc_ref[:, sl]], axis=0)
            p2 = p_ref[rho, 2 * pair:2 * pair + 2].reshape(2 * GRID_W, p_ref.shape[-1])
            o2 = jnp.dot(p2, vals, preferred_element_type=F32) * inv[2 * pair:2 * pair + 2].reshape(2 * GRID_W, 1)
            o_ref[qrows, sl] = _unstack_pair(o2).astype(o_ref.dtype)


def _na_call(q, kl, vl, kc, vc, rpb, batch):
    s = q.shape[0] // batch
    l = kc.shape[0] // batch
    w = q.shape[1]
    rows = s // GRID_W
    assert rows >= NA_KH
    rb = NA_ROWS_PER_STEP if rows % NA_ROWS_PER_STEP == 0 else 1
    nkeys = NA_KH * GRID_W + l
    nrb = rows // rb
    return pl.pallas_call(
        functools.partial(_na_kernel, rows=rows, rb=rb),
        grid=(batch, nrb),
        in_specs=[pl.BlockSpec((rb * GRID_W, w), lambda b, r: (b * nrb + r, 0)),
                  pl.BlockSpec((s, w), lambda b, r: (b, 0)),
                  pl.BlockSpec((s, w), lambda b, r: (b, 0)),
                  pl.BlockSpec((l, w), lambda b, r: (b, 0)),
                  pl.BlockSpec((l, w), lambda b, r: (b, 0)),
                  pl.BlockSpec(rpb.shape, lambda b, r: (0, 0, 0))],
        out_specs=pl.BlockSpec((rb * GRID_W, w), lambda b, r: (b * nrb + r, 0)),
        out_shape=jax.ShapeDtypeStruct(q.shape, BF16),
        scratch_shapes=[pltpu.VMEM((NA_KH, N_HEADS, GRID_W, NA_KH * GRID_W), F32),
                        pltpu.VMEM((rb, N_HEADS, GRID_W, nkeys), F32),
                        pltpu.VMEM((rb, N_HEADS, GRID_W, nkeys), BF16)],
        compiler_params=_params(("arbitrary", "arbitrary"), 48),
        name="na_attention",
    )(q, kl, vl, kc, vc, rpb)


def _pad_rpb(rpb):
    h, nr, nc = rpb.shape
    return jnp.zeros((h, 2 * NA_KH, LANES), F32).at[:, :nr, :nc].set(rpb.astype(F32))


def _ctx_attn_kernel(q_ref, k_ref, v_ref, o_ref):
    rows = q_ref.shape[0]
    lane = lax.broadcasted_iota(I32, (rows, 2 * HEAD_DIM), 1)
    first = lane < HEAD_DIM
    for pair in range(N_HEADS // 2):
        sl = slice(pair * 2 * HEAD_DIM, (pair + 1) * 2 * HEAD_DIM)
        q2 = q_ref[:, sl]
        k = k_ref[:, sl]
        v = v_ref[:, sl]
        outs = []
        for sub in range(2):
            qh = jnp.where(first if sub == 0 else ~first, q2, jnp.zeros_like(q2))
            outs.append(_softmax_pv([(_qk(qh, k), v)]))
        o_ref[:, sl] = jnp.where(first, outs[0], outs[1]).astype(o_ref.dtype)


def _ctx_attn_call(q, k, v, batch):
    l = q.shape[0] // batch
    w = q.shape[1]
    spec = pl.BlockSpec((l, w), lambda b: (b, 0))
    return pl.pallas_call(
        _ctx_attn_kernel,
        grid=(batch,),
        in_specs=[spec, spec, spec],
        out_specs=spec,
        out_shape=jax.ShapeDtypeStruct(q.shape, BF16),
        compiler_params=_params(("arbitrary",), 24),
        name="ctx_attention",
    )(q, k, v)


def _softplus(x):
    return jnp.maximum(x, 0.0) + jnp.log1p(jnp.exp(-jnp.abs(x)))


LRU_TIME_CHUNK = 128


def _lru_kernel(xl_hbm, xc_hbm, gl_ref, gc_ref, cw_ref, cb_ref, wg_ref, bg_ref, lam_ref, ol_ref, oc_ref,
                padl_ref, padc_ref, sem, *, tchunk):
    c = pl.program_id(0)
    slot = c % 2
    nb = SUBLANES
    halo_lo, halo_hi = nb, 2 * nb
    zero = jnp.zeros((nb, LANES), F32)

    def fetches(block, slot):
        return [pltpu.make_async_copy(x_hbm.at[block], pad_ref.at[slot, pl.ds(halo_lo, x_hbm.shape[1])], sem.at[slot, k])
                for k, (x_hbm, pad_ref) in enumerate(((xc_hbm, padc_ref), (xl_hbm, padl_ref)))]

    @pl.when(c == 0)
    def _():
        for x_hbm, pad_ref in ((xc_hbm, padc_ref), (xl_hbm, padl_ref)):
            n = x_hbm.shape[1]
            for s in range(2):
                pad_ref[s, 0:halo_lo, :] = zero
                pad_ref[s, halo_lo + n:halo_lo + n + halo_hi, :] = jnp.zeros((halo_hi, LANES), F32)
        for cp in fetches(0, 0):
            cp.start()

    @pl.when(c + 1 < pl.num_programs(0))
    def _():
        for cp in fetches(c + 1, 1 - slot):
            cp.start()

    for cp in fetches(c, slot):
        cp.wait()

    la = [(-0.5 * LRU_C * LOG2E) * _softplus(-lam_ref[d:d + 1, :]) for d in range(2)]
    cb = cb_ref[...]
    cw = [cw_ref[j:j + 1, :] for j in range(CONV_W)]
    rows = tchunk * nb

    def segment(pad_ref, g_ref, o_ref, d, h):
        n = o_ref.shape[0] // rows

        def body(i, h):
            r0 = pl.multiple_of((i if d == 0 else n - 1 - i) * rows, rows)
            u = cb
            for j in range(CONV_W):
                u = u + pad_ref[slot, pl.ds(r0 + j * nb, rows), :] * cw[j]
            ub = u.astype(BF16)
            ta = jnp.tanh(jnp.dot(ub, wg_ref[d, 0], preferred_element_type=F32) + bg_ref[d, 0])
            ti = jnp.tanh(jnp.dot(ub, wg_ref[d, 1], preferred_element_type=F32) + bg_ref[d, 1])
            a = jnp.exp2(la[d] * ta + la[d])
            z = 1.0 - a * a
            b = jnp.where(z > 0.0, z * lax.rsqrt(z), 0.0) * (0.5 * ti + 0.5) * u
            hs = [None] * tchunk
            for t in (range(tchunk) if d == 0 else reversed(range(tchunk))):
                h = a[t * nb:(t + 1) * nb] * h + b[t * nb:(t + 1) * nb]
                hs[t] = h
            hcat = jnp.concatenate(hs, axis=0)
            if d == 0:
                o_ref[pl.ds(r0, rows), :] = hcat
            else:
                y = o_ref[pl.ds(r0, rows), :] + hcat
                o_ref[pl.ds(r0, rows), :] = jax.nn.gelu(g_ref[pl.ds(r0, rows), :]) * y
            return h

        return lax.fori_loop(0, n, body, h)

    for d in range(2):
        h = segment(padc_ref, gc_ref, oc_ref, d, zero)
        segment(padl_ref, gl_ref, ol_ref, d, h)


def _lru_call(xr_l, gr_l, xr_c, gr_c, conv_w, conv_b, wg, bg, lam):
    ncb, rl, _ = xr_l.shape
    rc = xr_c.shape[1]
    tchunk = min(LRU_TIME_CHUNK, rl // SUBLANES, rc // SUBLANES)
    halo = 3 * SUBLANES
    lat = pl.BlockSpec((None, rl, LANES), lambda c: (c, 0, 0))
    ctx = pl.BlockSpec((None, rc, LANES), lambda c: (c, 0, 0))
    per_block = lambda a: jnp.moveaxis(a.reshape(a.shape[:-1] + (ncb, LANES)), -2, 0)
    return pl.pallas_call(
        functools.partial(_lru_kernel, tchunk=tchunk),
        grid=(ncb,),
        in_specs=[pl.BlockSpec(memory_space=pl.ANY), pl.BlockSpec(memory_space=pl.ANY),
                  lat, ctx,
                  pl.BlockSpec((None, CONV_W, LANES), lambda c: (c, 0, 0)),
                  pl.BlockSpec((None, 1, LANES), lambda c: (c, 0, 0)),
                  pl.BlockSpec((2, 2, None, LANES, LANES), lambda c: (0, 0, c, 0, 0)),
                  pl.BlockSpec((None, 2, 2, 1, LANES), lambda c: (c, 0, 0, 0, 0)),
                  pl.BlockSpec((None, 2, LANES), lambda c: (c, 0, 0))],
        out_specs=[lat, ctx],
        out_shape=[jax.ShapeDtypeStruct(xr_l.shape, F32), jax.ShapeDtypeStruct(xr_c.shape, F32)],
        scratch_shapes=[pltpu.VMEM((2, rl + halo, LANES), F32), pltpu.VMEM((2, rc + halo, LANES), F32),
                        pltpu.SemaphoreType.DMA((2, 2))],
        compiler_params=_params(("arbitrary",), 58),
        name="rglru",
    )(xr_l, xr_c, gr_l, gr_c, per_block(conv_w), per_block(conv_b.reshape(1, -1)), wg,
      per_block(bg[:, :, None, :]), per_block(lam))


def _lru_gate_weights(wa, ba, wx, bx, half):
    nblk = wa.shape[1]
    lw = nblk * LRU_BLOCK
    per = half // LRU_BLOCK

    def dense(w):
        w = w.reshape(2, nblk // per, per, LRU_BLOCK, LRU_BLOCK)
        eye = jnp.eye(per, dtype=w.dtype)
        full = w[:, :, :, :, None, :] * eye[None, None, :, None, :, None]
        return full.reshape(2, nblk // per, half, half)

    wg = (0.5 * jnp.stack([dense(wa), dense(wx)], axis=1)).astype(BF16)
    bg = 0.5 * jnp.stack([ba, bx], axis=1).astype(F32)
    return wg, bg


def _out_proj_kernel(oa_ref, ol_ref, h_ref, g1_ref, w_ref, o_ref, *, aw):
    o_ref[...] = _mix_proj(oa_ref, ol_ref, h_ref, g1_ref, w_ref, aw)


def _out_proj_call(oa, ol, h, bmods, layer, w_bf):
    nb, t, d = h.shape
    aw = oa.shape[2]
    ncb = ol.shape[0]
    tt = min(TIME_TILE, t)
    bt = lambda w: pl.BlockSpec((nb, tt, w), lambda j: (0, j, 0))
    return pl.pallas_call(
        functools.partial(_out_proj_kernel, aw=aw),
        grid=(t // tt,),
        in_specs=[bt(aw), pl.BlockSpec((ncb, tt * SUBLANES, LANES), lambda j: (0, j, 0)), bt(d),
                  _batch_mod_spec(nb, d, layer, 2), pl.BlockSpec(w_bf.shape, lambda j: (0, 0))],
        out_specs=bt(d),
        out_shape=jax.ShapeDtypeStruct((nb, t, d), F32),
        compiler_params=_params(("arbitrary",), 40),
        name="out_proj",
    )(oa, ol, h, bmods, w_bf)


def _swiglu_step(a_bf, w1_ref, w3_ref, w2_ref, acc_ref):
    fc = w1_ref.shape[-1]
    halves = []
    for lo in range(0, fc, fc // 2):
        cols = slice(lo, lo + fc // 2)
        g = jnp.dot(a_bf, w1_ref[:, cols].astype(BF16), preferred_element_type=F32)
        u = jnp.dot(a_bf, w3_ref[:, cols].astype(BF16), preferred_element_type=F32)
        halves.append((g * jax.nn.sigmoid(g) * u).astype(BF16))
    hmid = jnp.concatenate(halves, axis=1)
    acc_ref[...] += jnp.dot(hmid, w2_ref[...].astype(BF16), preferred_element_type=F32)


def _swiglu_streamed(a_ref, acc_ref, w_cur, w_next, first, has_next, tile, bufs, sem):
    w1b, w3b, w2b = bufs
    fc = w1b.shape[2]
    nc = w_cur[0].shape[1] // fc

    def copies(w, c, slot):
        return (pltpu.make_async_copy(w[0].at[:, pl.ds(c * fc, fc)], w1b.at[slot], sem.at[0, slot]),
                pltpu.make_async_copy(w[1].at[:, pl.ds(c * fc, fc)], w3b.at[slot], sem.at[1, slot]),
                pltpu.make_async_copy(w[2].at[pl.ds(c * fc, fc), :], w2b.at[slot], sem.at[2, slot]))

    @pl.when(first)
    def _():
        for cp in copies(w_cur, 0, (tile * nc) % 2):
            cp.start()

    acc_ref[...] = jnp.zeros_like(acc_ref)
    for c in range(nc):
        slot = (tile * nc + c) % 2
        if c + 1 < nc:
            for cp in copies(w_cur, c + 1, 1 - slot):
                cp.start()
        else:
            @pl.when(has_next)
            def _():
                for cp in copies(w_next, 0, 1 - slot):
                    cp.start()
        for cp in copies(w_cur, c, slot):
            cp.wait()
        _swiglu_step(a_ref[...], w1b.at[slot], w3b.at[slot], w2b.at[slot], acc_ref)


def _swiglu_buffers(d, dtype):
    return [pltpu.VMEM((2, d, FF_CHUNK), dtype), pltpu.VMEM((2, d, FF_CHUNK), dtype),
            pltpu.VMEM((2, FF_CHUNK, d), dtype), pltpu.SemaphoreType.DMA((3, 2))]


def _mix_proj(oa_ref, ol_ref, h_ref, g1_ref, wo_ref, aw):
    nb, tt, d = h_ref.shape
    o = jnp.dot(oa_ref[...].reshape(nb * tt, aw), wo_ref[0:aw, :], preferred_element_type=F32)
    ol = _from_time_major(ol_ref, nb, tt).astype(BF16)
    o = o + jnp.dot(ol, wo_ref[aw:, :], preferred_element_type=F32)
    return h_ref[...] + g1_ref[...] * o.reshape(nb, tt, d)


def _proj_ffn_kernel(oa_ref, ol_ref, h_ref, g1_ref, wo_ref, g_ref, sh_ref, sc_ref, g2_ref,
                     w1_hbm, w3_hbm, w2_hbm, o_ref, a_ref, acc_ref, w1b, w3b, w2b, sem, *, aw):
    j = pl.program_id(0)
    nb, tt, d = h_ref.shape
    h1 = _mix_proj(oa_ref, ol_ref, h_ref, g1_ref, wo_ref, aw)
    o_ref[...] = h1
    a_ref[...] = _rms_mod(h1, g_ref[...], sh_ref[...], sc_ref[...]).astype(BF16).reshape(nb * tt, d)
    w = (w1_hbm, w3_hbm, w2_hbm)
    _swiglu_streamed(a_ref, acc_ref, w, w, j == 0, j + 1 < pl.num_programs(0), j, (w1b, w3b, w2b), sem)
    o_ref[...] += g2_ref[...] * acc_ref[...].reshape(nb, tt, d)


def _proj_ffn_call(oa, ol, h, g, bmods, layer, wo_bf, w1, w3, w2):
    nb, t, d = h.shape
    aw = oa.shape[2]
    ncb = ol.shape[0]
    dff = w1.shape[1]
    tt = min(ROW_TILE_FFN // nb, t)
    assert dff % FF_CHUNK == 0
    bt = lambda w: pl.BlockSpec((nb, tt, w), lambda j: (0, j, 0))
    bm = lambda col: pl.BlockSpec((None, nb, 1, d), lambda j: (layer, 0, 0, col))
    hbm = pl.BlockSpec(memory_space=pl.ANY)
    return pl.pallas_call(
        functools.partial(_proj_ffn_kernel, aw=aw),
        grid=(t // tt,),
        in_specs=[bt(aw), pl.BlockSpec((ncb, tt * SUBLANES, LANES), lambda j: (0, j, 0)), bt(d), bm(2),
                  pl.BlockSpec(wo_bf.shape, lambda j: (0, 0), pipeline_mode=pl.Buffered(1)),
                  pl.BlockSpec((1, d), lambda j: (0, 0)), bm(3), bm(4), bm(5), hbm, hbm, hbm],
        out_specs=bt(d),
        out_shape=jax.ShapeDtypeStruct((nb, t, d), F32),
        scratch_shapes=[pltpu.VMEM((nb * tt, d), BF16), pltpu.VMEM((nb * tt, d), F32)]
        + _swiglu_buffers(d, w1.dtype),
        compiler_params=_params(("arbitrary",), 58),
        name="proj_ffn",
    )(oa, ol, h, bmods, wo_bf, g, bmods, bmods, bmods, w1, w3, w2)


def _to_row_tiles(x, o_ref):
    t, d = x.shape
    ns = d // LANES
    for s in range(ns):
        o_ref[pl.ds(s, t, stride=ns), :] = x[:, s * LANES:(s + 1) * LANES]


def _from_row_tiles(x_ref, idx, t, ns):
    return jnp.concatenate([x_ref[idx + (pl.ds(s, t, stride=ns), slice(None))] for s in range(ns)], axis=1)


def _gffn_kernel(te_ref, nu_ref, x_ref, w1_hbm, w3_hbm, w2_hbm, o_ref, a_ref, acc_ref, w1b, w3b, w2b, sem):
    i = pl.program_id(0)
    tm, d = a_ref.shape
    nu = nu_ref[0]

    @pl.when(i < nu)
    def _():
        a_ref[...] = _from_row_tiles(x_ref, (), tm, d // LANES).astype(BF16)
        e = te_ref[i]
        e_next = te_ref[jnp.minimum(i + 1, pl.num_programs(0) - 1)]
        _swiglu_streamed(a_ref, acc_ref, (w1_hbm.at[e], w3_hbm.at[e], w2_hbm.at[e]),
                         (w1_hbm.at[e_next], w3_hbm.at[e_next], w2_hbm.at[e_next]),
                         i == 0, i + 1 < nu, i, (w1b, w3b, w2b), sem)
        _to_row_tiles(acc_ref[...], o_ref)

    @pl.when(i >= nu)
    def _():
        o_ref[...] = jnp.zeros_like(o_ref)


def _gffn_call(tile_expert, n_used, xg, w1, w3, w2):
    d = w1.shape[1]
    ns = d // LANES
    rg = xg.shape[0] // ns
    dff = w1.shape[2]
    tm = ROW_TILE_FFN
    assert dff % FF_CHUNK == 0

    def tile(i, nu):
        return jnp.maximum(jnp.minimum(i, nu[0] - 1), 0)

    hbm = pl.BlockSpec(memory_space=pl.ANY)
    grid_spec = pltpu.PrefetchScalarGridSpec(
        num_scalar_prefetch=2,
        grid=(rg // tm,),
        in_specs=[pl.BlockSpec((tm * ns, LANES), lambda i, te, nu: (tile(i, nu), 0)), hbm, hbm, hbm],
        out_specs=pl.BlockSpec((tm * ns, LANES), lambda i, te, nu: (i, 0)),
        scratch_shapes=[pltpu.VMEM((tm, d), BF16), pltpu.VMEM((tm, d), F32)] + _swiglu_buffers(d, w1.dtype))
    return pl.pallas_call(
        _gffn_kernel,
        grid_spec=grid_spec,
        out_shape=jax.ShapeDtypeStruct((rg * ns, LANES), F32),
        compiler_params=_params(("arbitrary",), 56),
        name="ffn_grouped",
    )(tile_expert, n_used, xg, w1, w3, w2)


def _route_kernel(x_ref, g_ref, sh_ref, sc_ref, wr_ref, br_ref, a_ref, tab_ref, gcol_ref, cnt_ref, carry_ref):
    i = pl.program_id(0)
    tm = x_ref.shape[0]

    @pl.when(i == 0)
    def _():
        carry_ref[...] = jnp.zeros_like(carry_ref)

    a = _rms_mod(x_ref[...], g_ref[...], sh_ref[...], sc_ref[...])
    _to_row_tiles(a, a_ref)
    lane = lax.broadcasted_iota(I32, (tm, LANES), 1).astype(F32)
    logits = jnp.dot(a, wr_ref[...], preferred_element_type=F32, precision=lax.Precision.HIGHEST) + br_ref[...]
    logits = jnp.where(lane < N_EXPERTS, logits, NEG_INF)
    m1 = jnp.max(logits, axis=-1, keepdims=True)
    i1 = jnp.min(jnp.where(logits == m1, lane, float(LANES)), axis=-1, keepdims=True)
    rest = jnp.where(lane == i1, 2.0 * NEG_INF, logits)
    m2 = jnp.max(rest, axis=-1, keepdims=True)
    i2 = jnp.min(jnp.where(rest == m2, lane, float(LANES)), axis=-1, keepdims=True)
    e = jnp.exp(m2 - m1)
    w1 = 1.0 / (1.0 + e)
    w2 = e * w1
    sel1 = lane == i1
    sel2 = lane == i2
    onehot = jnp.where(sel1 | sel2, 1.0, 0.0)
    row = lax.broadcasted_iota(I32, (tm, tm), 0)
    col = lax.broadcasted_iota(I32, (tm, tm), 1)
    before = jnp.where(col < row, 1.0, 0.0).astype(BF16)
    cum = jnp.dot(before, onehot.astype(BF16), preferred_element_type=F32) + carry_ref[...]
    r1 = jnp.sum(jnp.where(sel1, cum, 0.0), axis=-1, keepdims=True)
    r2 = jnp.sum(jnp.where(sel2, cum, 0.0), axis=-1, keepdims=True)
    carry_ref[...] += jnp.sum(onehot, axis=0, keepdims=True)
    cnt_ref[...] = carry_ref[...]
    cols = (jnp.where(lane == 0, i1, 0.0) + jnp.where(lane == 1, i2, 0.0)
            + jnp.where(lane == 2, r1, 0.0) + jnp.where(lane == 3, r2, 0.0))
    tab_ref[...] = cols.T[0:SUBLANES, :]
    gcol_ref[...] = (jnp.where(lane == 0, w1, 0.0) + jnp.where(lane == 1, w2, 0.0))[:, 0:SUBLANES]


def _route_call(h, g, mods, layer, wr, br, rows_per_mod):
    n, d = h.shape
    tm = min(ROW_TILE, n)
    ms = functools.partial(_mod_spec, d, layer=layer, rows_per_mod=rows_per_mod, tm=tm, fixed_row=None)
    return pl.pallas_call(
        _route_kernel,
        grid=(n // tm,),
        in_specs=[pl.BlockSpec((tm, d), lambda i: (i, 0)),
                  pl.BlockSpec((1, d), lambda i: (0, 0)), ms(col=3), ms(col=4),
                  pl.BlockSpec((d, LANES), lambda i: (0, 0)),
                  pl.BlockSpec((1, LANES), lambda i: (0, 0))],
        out_specs=[pl.BlockSpec((tm * (d // LANES), LANES), lambda i: (i, 0)),
                   pl.BlockSpec((SUBLANES, tm), lambda i: (0, i)),
                   pl.BlockSpec((tm, SUBLANES), lambda i: (i, 0)),
                   pl.BlockSpec((1, LANES), lambda i: (0, 0))],
        out_shape=[jax.ShapeDtypeStruct((n * (d // LANES), LANES), F32),
                   jax.ShapeDtypeStruct((SUBLANES, n), F32),
                   jax.ShapeDtypeStruct((n, SUBLANES), F32),
                   jax.ShapeDtypeStruct((1, LANES), F32)],
        scratch_shapes=[pltpu.VMEM((1, LANES), F32)],
        compiler_params=_params(("arbitrary",), 40),
        name="moe_route",
    )(h, g, mods, mods, wr, br)


DMA_UNROLL = 4


FILL_CHUNK = 64


def _dispatch_kernel(off_ref, fill_ref, tab_ref, a_ref, xg_out, pos_ref, sem, *, ns):
    tm = tab_ref.shape[1]

    def rows_copy(src, dst, nrows):
        return pltpu.make_async_copy(a_ref.at[pl.ds(pl.multiple_of(src * ns, ns), nrows * ns)],
                                     xg_out.at[pl.ds(pl.multiple_of(dst * ns, ns), nrows * ns)], sem)

    def row_copy(src, dst):
        return rows_copy(src, dst, 1)

    @pl.when(pl.program_id(0) == pl.num_programs(0) - 1)
    def _():
        nfill = fill_ref.shape[0] // 2
        for e in range(nfill):
            start = fill_ref[e]
            length = fill_ref[nfill + e]
            nbig = length // FILL_CHUNK
            nsmall = length - nbig * FILL_CHUNK
            small0 = start + nbig * FILL_CHUNK

            def big(j, carry, start=start):
                rows_copy(0, start + j * FILL_CHUNK, FILL_CHUNK).start()
                return carry

            def small(j, carry, small0=small0):
                row_copy(0, small0 + j).start()
                return carry

            lax.fori_loop(0, nbig, big, 0)
            lax.fori_loop(0, nsmall, small, 0)
            lax.fori_loop(0, nbig, lambda j, c: (rows_copy(0, 0, FILL_CHUNK).wait(), c)[1], 0)
            lax.fori_loop(0, nsmall, lambda j, c: (row_copy(0, 0).wait(), c)[1], 0)

    def issue(j, carry):
        for u in range(DMA_UNROLL):
            t = j * DMA_UNROLL + u
            for k in range(2):
                p = off_ref[tab_ref[k, t]] + tab_ref[2 + k, t]
                pos_ref[k, t] = p
                row_copy(t, p).start()
        return carry

    lax.fori_loop(0, tm // DMA_UNROLL, issue, 0)

    def drain(j, carry):
        for _ in range(2 * DMA_UNROLL):
            row_copy(0, 0).wait()
        return carry

    lax.fori_loop(0, tm // DMA_UNROLL, drain, 0)


def _dispatch_call(off, fill, tab_i, a, rows_g, ns):
    n = a.shape[0] // ns
    tm = min(ROW_TILE, n)
    assert tm >= FILL_CHUNK
    grid_spec = pltpu.PrefetchScalarGridSpec(
        num_scalar_prefetch=2,
        grid=(n // tm,),
        in_specs=[pl.BlockSpec((4, tm), lambda i, off, fill: (0, i), memory_space=pltpu.SMEM),
                  pl.BlockSpec((tm * ns, LANES), lambda i, off, fill: (i, 0))],
        out_specs=[pl.BlockSpec(memory_space=pl.ANY),
                   pl.BlockSpec((2, tm), lambda i, off, fill: (0, i), memory_space=pltpu.SMEM)],
        scratch_shapes=[pltpu.SemaphoreType.DMA(())])
    return pl.pallas_call(
        functools.partial(_dispatch_kernel, ns=ns),
        grid_spec=grid_spec,
        out_shape=[jax.ShapeDtypeStruct((rows_g * ns, LANES), a.dtype), jax.ShapeDtypeStruct((2, n), I32)],
        compiler_params=_params(("arbitrary",), 16),
        name="moe_dispatch",
    )(off, fill, tab_i, a)


def _combine_kernel(pos_ref, posn_ref, y_hbm, h_ref, gcol_ref, g2_ref, fg_ref, o_ref, buf_ref, sem):
    i = pl.program_id(0)
    n = pl.num_programs(0)
    tm, d = h_ref.shape
    ns = d // LANES

    def row_copy(src, slot, k, t):
        return pltpu.make_async_copy(y_hbm.at[pl.ds(pl.multiple_of(src * ns, ns), ns)],
                                     buf_ref.at[slot, k, pl.ds(pl.multiple_of(t * ns, ns), ns)], sem.at[slot])

    def issue(p_ref, slot):
        def body(j, carry):
            for u in range(DMA_UNROLL):
                t = j * DMA_UNROLL + u
                for k in range(2):
                    row_copy(p_ref[k, t], slot, k, t).start()
            return carry
        lax.fori_loop(0, tm // DMA_UNROLL, body, 0)

    @pl.when(i == 0)
    def _():
        issue(pos_ref, 0)

    @pl.when(i + 1 < n)
    def _():
        issue(posn_ref, (i + 1) % 2)

    slot = i % 2

    def drain(j, carry):
        for _ in range(2 * DMA_UNROLL):
            row_copy(0, slot, 0, 0).wait()
        return carry

    lax.fori_loop(0, tm // DMA_UNROLL, drain, 0)
    y = (gcol_ref[:, 0:1] * _from_row_tiles(buf_ref, (slot, 0), tm, ns)
         + gcol_ref[:, 1:2] * _from_row_tiles(buf_ref, (slot, 1), tm, ns))
    hn = h_ref[...] + g2_ref[...] * y
    o_ref[...] = hn * lax.rsqrt(jnp.mean(hn * hn, axis=-1, keepdims=True) + EPS) * fg_ref[...]


def _combine_call(pos, y, h, gcol, mods, layer, final_g, rows_per_mod):
    n, d = h.shape
    tm = min(GATHER_TILE, n)
    nt = n // tm
    ms = functools.partial(_mod_spec, d, layer=layer, rows_per_mod=rows_per_mod, tm=tm, fixed_row=None)
    return pl.pallas_call(
        _combine_kernel,
        grid=(nt,),
        in_specs=[pl.BlockSpec((2, tm), lambda i: (0, i), memory_space=pltpu.SMEM),
                  pl.BlockSpec((2, tm), lambda i: (0, jnp.minimum(i + 1, nt - 1)), memory_space=pltpu.SMEM),
                  pl.BlockSpec(memory_space=pl.ANY),
                  pl.BlockSpec((tm, d), lambda i: (i, 0)),
                  pl.BlockSpec((tm, SUBLANES), lambda i: (i, 0)),
                  ms(col=5),
                  pl.BlockSpec((1, d), lambda i: (0, 0))],
        out_specs=pl.BlockSpec((tm, d), lambda i: (i, 0)),
        out_shape=jax.ShapeDtypeStruct((n, d), F32),
        scratch_shapes=[pltpu.VMEM((2, 2, tm * (d // LANES), LANES), F32), pltpu.SemaphoreType.DMA((2,))],
        compiler_params=_params(("arbitrary",), 24),
        name="moe_combine",
    )(pos, pos, y, h, gcol, mods, final_g)


def _moe_layer(h, g, mods, layer, router, router_b, w1, w3, w2, final_g, rows_per_mod):
    n, d = h.shape
    wr = jnp.zeros((d, LANES), F32).at[:, :N_EXPERTS].set(router)
    br = jnp.zeros((1, LANES), F32).at[0, :N_EXPERTS].set(router_b)
    a, tab, gcol, cnt = _route_call(h, g, mods, layer, wr, br, rows_per_mod)
    tm = ROW_TILE_FFN
    n_tiles = (2 * n) // tm + N_EXPERTS
    counts = cnt[0, :N_EXPERTS].astype(I32)
    tiles = (counts + tm - 1) // tm
    ends = jnp.cumsum(tiles)
    off = ((ends - tiles) * tm).astype(I32)
    tile_expert = jnp.minimum(jnp.sum(jnp.arange(n_tiles, dtype=I32)[:, None] >= ends[None, :], axis=1),
                              N_EXPERTS - 1).astype(I32)
    n_used = ends[-1:].astype(I32)
    assert d // LANES == SUBLANES, "one token must fill one (8, 128) tile of the row-tile layout"
    used = ends[-1:] * tm
    fill = jnp.concatenate([off + counts, used, tiles * tm - counts, n_tiles * tm - used]).astype(I32)
    xg, pos = _dispatch_call(off, fill, tab[0:4].astype(I32), a, n_tiles * tm, d // LANES)
    y = _gffn_call(tile_expert, n_used, xg, w1, w3, w2)
    return _combine_call(pos, y, h, gcol, mods, layer, final_g, rows_per_mod)


def kernel(x, c, ctx, c_ctx, ada_w, ada_b, mix_norm_g, ffn_norm_g, w_in, w_out, na_rpb, conv_w, conv_b, lru_wa, lru_ba, lru_wx, lru_bx, lru_lam, ffn_w1, ffn_w3, ffn_w2, moe_router, moe_router_b, moe_w1, moe_w3, moe_w2, final_g):
    batch, seq, d = x.shape
    ctx_len = ctx.shape[1]
    depth = ada_w.shape[0]
    lw = conv_w.shape[-1]
    aw = (w_in.shape[-1] - 2 * lw) // 3
    ctx_row = batch

    mod_rows = -(-(batch + 1) // SUBLANES) * SUBLANES
    c_all = jnp.zeros((mod_rows, d), F32).at[:batch].set(c).at[ctx_row].set(c_ctx)
    mods = _ada_call(c_all, ada_w, ada_b).reshape(depth, mod_rows, 1, 6 * d)

    bmods_lat = mods[:, :batch]
    bmods_ctx = jnp.broadcast_to(mods[:, ctx_row:ctx_row + 1], bmods_lat.shape)
    flat = lambda a: a.reshape(a.shape[0] * a.shape[1], a.shape[2])
    h_lat, h_ctx = x, ctx
    out = None
    for i in range(depth):
        last = i == depth - 1
        w_in_bf = w_in[i].astype(BF16)
        w_out_bf = w_out[i].astype(BF16)
        g_mix = mix_norm_g[i].reshape(1, d)
        g_ffn = ffn_norm_g[i].reshape(1, d)
        q_l, k_l, v_l, xr_l, gr_l = _in_proj_call(h_lat, g_mix, bmods_lat, i, w_in_bf, aw, lw)
        q_c, k_c, v_c, xr_c, gr_c = _in_proj_call(h_ctx, g_mix, bmods_ctx, i, w_in_bf, aw, lw)
        oa_l = _na_call(flat(q_l), flat(k_l), flat(v_l), flat(k_c), flat(v_c), _pad_rpb(na_rpb[i]), batch)
        wg, bg = _lru_gate_weights(lru_wa[i], lru_ba[i], lru_wx[i], lru_bx[i], LANES)
        ol_l, ol_c = _lru_call(xr_l, gr_l, xr_c, gr_c, conv_w[i], conv_b[i], wg, bg, lru_lam[i])
        oa_l = oa_l.reshape(batch, seq, aw)
        j = i // 2
        if i % 2 == 0:
            ffn_w = (ffn_w1[j], ffn_w3[j], ffn_w2[j])
            h_lat = _proj_ffn_call(oa_l, ol_l, h_lat, g_ffn, bmods_lat, i, w_out_bf, *ffn_w)
            if not last:
                oa_c = _ctx_attn_call(flat(q_c), flat(k_c), flat(v_c), batch).reshape(batch, ctx_len, aw)
                h_ctx = _proj_ffn_call(oa_c, ol_c, h_ctx, g_ffn, bmods_ctx, i, w_out_bf, *ffn_w)
        else:
            assert last, "the routed layer is fused with the final norm"
            h_lat = _out_proj_call(oa_l, ol_l, h_lat, bmods_lat, i, w_out_bf)
            out = _moe_layer(flat(h_lat), g_ffn, mods, i, moe_router[j], moe_router_b[j],
                             moe_w1[j], moe_w3[j], moe_w2[j], final_g.reshape(1, d), seq)
    return out.reshape(batch, seq, d)
```
